```python
import math
import jax, jax.numpy as jnp
from jax import lax
import numpy as np

D_MODEL = 2048
BATCH = 8
SEQ = 2048
DEPTH = 2

GRID_W = 64
CTX_LEN = 256
HY_W = D_MODEL // 4
ML_W = 3 * D_MODEL // 8
RT_W = 3 * D_MODEL // 8
HY_EMB = 33
HY_BANDS = (HY_EMB - 1) // 2
HY_FF = 64
HY_DECAY_TARGET = 1e-2
HY_SHORT_PCT = 0.3
HY_LONG_PCT = 1.5
ML_HEADS = 4
ML_DH = ML_W // ML_HEADS
ML_CHUNK = 128
RT_HEADS = 4
RT_DV = RT_W // RT_HEADS
RT_DK = RT_DV // 2
RT_CHUNK = 128
ROPE_BASE = 10000.0
D_FF = 4 * D_MODEL
N_MOD = 6
EPS = 1e-6
HY_COLS = 3 * HY_W
ML_COLS = 4 * ML_W
MLG_COLS = 4 * ML_HEADS
RT_COLS = 2 * RT_HEADS * RT_DK + 2 * RT_W
N_IN = HY_COLS + ML_COLS + MLG_COLS + RT_COLS
SPLITS = [HY_COLS, HY_COLS + ML_COLS, HY_COLS + ML_COLS + MLG_COLS]

kernel_name = "hybrid_hyena_mlstm_retention_dit_prefix"


def rms_norm(x, g):
    xf = x.astype(jnp.float32)
    y = xf * lax.rsqrt(jnp.mean(xf * xf, axis=-1, keepdims=True) + EPS)
    return (y * g.astype(jnp.float32)).astype(x.dtype)


def _head_rms(h):
    return h * lax.rsqrt(jnp.mean(h * h, axis=-1, keepdims=True) + EPS)


def modulate(h, shift, scale):
    return h * (1.0 + scale) + shift


def short_conv3(u, w):
    up = jnp.pad(u, ((0, 0), (1, 1), (0, 0)))
    return up[:, :-2] * w[0] + up[:, 1:-1] * w[1] + up[:, 2:] * w[2]


def _heads(a, n_heads, dh):
    B, L, _ = a.shape
    return a.reshape(B, L, n_heads, dh).transpose(0, 2, 1, 3).astype(jnp.float32)


def _chunks(a, T):
    B, H, L = a.shape[:3]
    return jnp.moveaxis(a.reshape(B, H, L // T, T, *a.shape[3:]), 2, 0)


def _unchunk(a):
    nc, B, H, T = a.shape[:4]
    return jnp.moveaxis(a, 0, 2).reshape(B, H, nc * T, *a.shape[4:])


def hyena_filter(L, w1, b1, w2, b2, w3, freq):
    f32 = jnp.float32
    t = jnp.linspace(0.0, 1.0, L, dtype=f32)[:, None]
    w = (2.0 * math.pi / L) * jnp.arange(L, dtype=f32)[:, None]
    bands = jnp.linspace(1e-4, HY_BANDS - 1, HY_BANDS, dtype=f32)[None, :]
    z = jnp.concatenate([t, jnp.cos(bands * w), -jnp.sin(bands * w)], axis=-1)
    fr = freq.astype(f32)
    hdn = jnp.sin(fr * (z @ w1.astype(f32) + b1.astype(f32)))
    hdn = jnp.sin(fr * (hdn @ w2.astype(f32) + b2.astype(f32)))
    h = (hdn @ w3.astype(f32)).reshape(L, 2, HY_W)
    deltas = jnp.abs(jnp.linspace(math.log(HY_DECAY_TARGET) / HY_LONG_PCT,
                                  math.log(HY_DECAY_TARGET) / HY_SHORT_PCT, HY_W, dtype=f32))
    h = h * jnp.exp(-t * deltas)[:, None, :]
    hf, hb = h[:, 0], h[:, 1]
    filt = jnp.concatenate([hf, jnp.zeros((1, HY_W), f32), hb[:0:-1]], axis=0)
    return filt / jnp.sum(jnp.abs(filt), axis=0, keepdims=True)


def fft_long_conv(u, filt, bias):
    L = u.shape[1]
    uf = jnp.fft.rfft(u.astype(jnp.float32), n=2 * L, axis=1)
    ff = jnp.fft.rfft(filt, n=2 * L, axis=0)
    y = jnp.fft.irfft(uf * ff[None], n=2 * L, axis=1)[:, :L]
    return (y + u.astype(jnp.float32) * bias.astype(jnp.float32)).astype(u.dtype)


def hyena_mixer(p, conv_w, filt, bias):
    u = short_conv3(p, conv_w)
    x0, x1, v = jnp.split(u, 3, axis=-1)
    return fft_long_conv(v * x1, filt, bias) * x0


def mlstm_scan(q, k, v, log_i, log_f, state):
    T = ML_CHUNK
    causal = jnp.tril(jnp.ones((T, T), dtype=bool))

    def step(carry, xs):
        C, n, m = carry
        qc, kc, vc, ic, fc = xs
        b = jnp.cumsum(fc, axis=-1)
        dmat = jnp.where(causal, b[..., :, None] - b[..., None, :] + ic[..., None, :], -jnp.inf)
        m_inter = b + m[..., None]
        m_t = jnp.maximum(m_inter, jnp.max(dmat, axis=-1))
        s = jnp.einsum("bhtd,bhsd->bhts", qc, kc) * jnp.exp(dmat - m_t[..., None])
        w_inter = jnp.exp(m_inter - m_t)
        num = jnp.einsum("bhts,bhse->bhte", s, vc) + w_inter[..., None] * jnp.einsum("bhtd,bhde->bhte", qc, C)
        den = jnp.sum(s, axis=-1) + w_inter * jnp.einsum("bhtd,bhd->bht", qc, n)
        h = num / jnp.maximum(jnp.abs(den), jnp.exp(-m_t))[..., None]
        b_end = b[..., -1]
        g = b_end[..., None] - b + ic
        m_new = jnp.maximum(b_end + m, jnp.max(g, axis=-1))
        wk = jnp.exp(g - m_new[..., None])
        carry_decay = jnp.exp(b_end + m - m_new)
        C_new = carry_decay[..., None, None] * C + jnp.einsum("bhsd,bhse->bhde", kc * wk[..., None], vc)
        n_new = carry_decay[..., None] * n + jnp.einsum("bhsd,bhs->bhd", kc, wk)
        return (C_new, n_new, m_new), h

    xs = (_chunks(q, T), _chunks(k, T), _chunks(v, T), _chunks(log_i, T), _chunks(log_f, T))
    state, hs = lax.scan(step, state, xs)
    return _unchunk(hs), state


def _mlstm_prep(p, graw, conv_w, gate_b):
    qk, v, o = jnp.split(p, [2 * ML_W, 3 * ML_W], axis=-1)
    qk = jax.nn.silu(short_conv3(qk, conv_w))
    q, k = jnp.split(qk, 2, axis=-1)
    q = _heads(q, ML_HEADS, ML_DH)
    k = _heads(k, ML_HEADS, ML_DH) * (ML_DH ** -0.5)
    v = _heads(v, ML_HEADS, ML_DH)
    B, L, _ = p.shape
    gates = (graw + gate_b).astype(jnp.float32).reshape(B, L, 4, ML_HEADS).transpose(2, 0, 3, 1)
    g_fwd = (gates[0], jax.nn.log_sigmoid(gates[1]))
    g_bwd = (gates[2], jax.nn.log_sigmoid(gates[3]))
    return q, k, v, o, g_fwd, g_bwd


def _mlstm_bidir(q, k, v, g_fwd, g_bwd, init_f, init_b):
    h_f, st_f = mlstm_scan(q, k, v, g_fwd[0], g_fwd[1], init_f)
    fl = lambda a: jnp.flip(a, axis=2)
    h_b, st_b = mlstm_scan(fl(q), fl(k), fl(v), fl(g_bwd[0]), fl(g_bwd[1]), init_b)
    return h_f + fl(h_b), st_f, st_b


def _mlstm_out(h, o, g):
    B, H, L, dh = h.shape
    hn = _head_rms(h) * g.astype(jnp.float32).reshape(H, 1, dh)
    return (hn.transpose(0, 2, 1, 3).reshape(B, L, H * dh) * jax.nn.sigmoid(o.astype(jnp.float32))).astype(o.dtype)


def mlstm_group(pc, gc, px, gx, conv_w, gate_b, norm_g):
    B = pc.shape[0]
    f32 = jnp.float32
    zero = (jnp.zeros((B, ML_HEADS, ML_DH, ML_DH), f32), jnp.zeros((B, ML_HEADS, ML_DH), f32),
            jnp.zeros((B, ML_HEADS), f32))
    qc, kc, vc, oc, gfc, gbc = _mlstm_prep(pc, gc, conv_w, gate_b)
    hc, st_f, st_b = _mlstm_bidir(qc, kc, vc, gfc, gbc, zero, zero)
    qx, kx, vx, ox, gfx, gbx = _mlstm_prep(px, gx, conv_w, gate_b)
    hx, _, _ = _mlstm_bidir(qx, kx, vx, gfx, gbx, st_f, st_b)
    return _mlstm_out(hc, oc, norm_g), _mlstm_out(hx, ox, norm_g)


def axial_rope(L):
    rows = L // GRID_W
    r = jnp.repeat(jnp.arange(rows, dtype=jnp.float32), GRID_W)
    col = jnp.tile(jnp.arange(GRID_W, dtype=jnp.float32), rows)
    nf = RT_DK // 4
    inv = ROPE_BASE ** (-jnp.arange(nf, dtype=jnp.float32) / nf)
    ang = jnp.concatenate([r[:, None] * inv, col[:, None] * inv], axis=-1)
    return jnp.cos(ang), jnp.sin(ang)


def apply_rope(a, cos, sin):
    a1, a2 = jnp.split(a, 2, axis=-1)
    return jnp.concatenate([a1 * cos - a2 * sin, a1 * sin + a2 * cos], axis=-1)


def retention_scan(q, k, v, log_g, S):
    T = RT_CHUNK
    idx = jnp.arange(T, dtype=jnp.float32)
    rel = idx[:, None] - idx[None, :]
    dmask = jnp.where(rel >= 0, jnp.exp(log_g[:, None, None] * jnp.maximum(rel, 0.0)), 0.0)
    q_decay = jnp.exp(log_g[:, None] * (idx + 1.0))[..., None]
    k_decay = jnp.exp(log_g[:, None] * (T - 1.0 - idx))[..., None]
    c_decay = jnp.exp(log_g * T)[:, None, None]

    def step(S, xs):
        qc, kc, vc = xs
        s = jnp.einsum("bhtd,bhsd->bhts", qc, kc) * dmask
        o = jnp.einsum("bhts,bhse->bhte", s, vc) + q_decay * jnp.einsum("bhtd,bhde->bhte", qc, S)
        S = c_decay * S + jnp.einsum("bhsd,bhse->bhde", kc * k_decay, vc)
        return S, o

    S, os_ = lax.scan(step, S, (_chunks(q, T), _chunks(k, T), _chunks(v, T)))
    return _unchunk(os_), S


def _ret_prep(p, rope):
    nqk = RT_HEADS * RT_DK
    q, k, v, g = jnp.split(p, [nqk, 2 * nqk, 2 * nqk + RT_W], axis=-1)
    q = _heads(q, RT_HEADS, RT_DK)
    k = _heads(k, RT_HEADS, RT_DK)
    v = _heads(v, RT_HEADS, RT_DV)
    if rope is not None:
        q = apply_rope(q, rope[0], rope[1])
        k = apply_rope(k, rope[0], rope[1])
    return q * (RT_DK ** -0.5), k, v, g


def _ret_bidir(q, k, v, lg_f, lg_b, init_f, init_b):
    o_f, S_f = retention_scan(q, k, v, lg_f, init_f)
    fl = lambda a: jnp.flip(a, axis=2)
    o_b, S_b = retention_scan(fl(q), fl(k), fl(v), lg_b, init_b)
    return o_f + fl(o_b), S_f, S_b


def _ret_out(o, g):
    B, H, L, dv = o.shape
    on = _head_rms(o).transpose(0, 2, 1, 3).reshape(B, L, H * dv)
    return (on * jax.nn.silu(g.astype(jnp.float32))).astype(g.dtype)


def retention_group(pc, px, log_decay, rope):
    lg_f = -jnp.exp(log_decay[0].astype(jnp.float32))
    lg_b = -jnp.exp(log_decay[1].astype(jnp.float32))
    B = pc.shape[0]
    zero = jnp.zeros((B, RT_HEADS, RT_DK, RT_DV), jnp.float32)
    qc, kc, vc, gc = _ret_prep(pc, None)
    oc, S_f, S_b = _ret_bidir(qc, kc, vc, lg_f, lg_b, zero, zero)
    qx, kx, vx, gx = _ret_prep(px, rope)
    ox, _, _ = _ret_bidir(qx, kx, vx, lg_f, lg_b, S_f, S_b)
    return _ret_out(oc, gc), _ret_out(ox, gx)


def sq_relu_mlp(h, w1, w2):
    return jnp.square(jax.nn.relu(h @ w1)) @ w2


def setup_inputs(seed: int = 0) -> dict:
    key = jax.random.key(seed)
    ks = jax.random.split(key, 32)
    f32 = jnp.float32
    nrm = lambda k, shape, s: jax.random.normal(k, shape, f32) * s
    D = D_MODEL
    i_b = nrm(ks[17], (DEPTH, 2, ML_HEADS), 0.1)
    f_b = jnp.linspace(3.0, 6.0, ML_HEADS, dtype=f32) + nrm(ks[18], (DEPTH, 2, ML_HEADS), 0.1)
    ml_gate_b = jnp.stack([i_b[:, 0], f_b[:, 0], i_b[:, 1], f_b[:, 1]], axis=1).reshape(DEPTH, MLG_COLS)
    base = jnp.log(-jnp.log(1.0 - 2.0 ** (-5.0 - jnp.arange(RT_HEADS, dtype=f32))))
    return {
        "x": nrm(ks[0], (BATCH, SEQ, D), 1.0),
        "c": nrm(ks[1], (BATCH, D), 1.0),
        "ctx": nrm(ks[2], (BATCH, CTX_LEN, D), 1.0),
        "c_ctx": nrm(ks[3], (D,), 1.0),
        "norm1_g": 1.0 + nrm(ks[4], (DEPTH, D), 0.02),
        "norm2_g": 1.0 + nrm(ks[5], (DEPTH, D), 0.02),
        "w_mod": nrm(ks[6], (DEPTH, D, N_MOD * D), D ** -0.5),
        "b_mod": nrm(ks[7], (DEPTH, N_MOD * D), 0.02),
        "w_in": nrm(ks[8], (DEPTH, D, N_IN), D ** -0.5),
        "hy_conv_w": nrm(ks[9], (DEPTH, 3, HY_COLS), 3 ** -0.5),
        "hy_f_w1": nrm(ks[10], (DEPTH, HY_EMB, HY_FF), HY_EMB ** -0.5),
        "hy_f_b1": nrm(ks[11], (DEPTH, HY_FF), 0.02),
        "hy_f_w2": nrm(ks[12], (DEPTH, HY_FF, HY_FF), HY_FF ** -0.5),
        "hy_f_b2": nrm(ks[13], (DEPTH, HY_FF), 0.02),
        "hy_f_w3": nrm(ks[14], (DEPTH, HY_FF, 2 * HY_W), HY_FF ** -0.5),
        "hy_f_freq": 1.0 + nrm(ks[15], (DEPTH, HY_FF), 0.02),
        "hy_bias": nrm(ks[16], (DEPTH, HY_W), 0.5),
        "ml_conv_w": nrm(ks[19], (DEPTH, 3, 2 * ML_W), 3 ** -0.5),
        "ml_gate_b": ml_gate_b,
        "ml_norm_g": 1.0 + nrm(ks[20], (DEPTH, ML_W), 0.02),
        "rt_log_decay": base + nrm(ks[21], (DEPTH, 2, RT_HEADS), 0.05),
        "w_out": nrm(ks[22], (DEPTH, D, D), D ** -0.5),
        "w_ff1": nrm(ks[23], (DEPTH, D, D_FF), D ** -0.5),
        "w_ff2": nrm(ks[24], (DEPTH, D_FF, D), D_FF ** -0.5),
        "final_g": 1.0 + nrm(ks[25], (D,), 0.02),
    }


def reference(x, c, ctx, c_ctx, norm1_g, norm2_g, w_mod, b_mod, w_in, hy_conv_w, hy_f_w1, hy_f_b1,
              hy_f_w2, hy_f_b2, hy_f_w3, hy_f_freq, hy_bias, ml_conv_w, ml_gate_b, ml_norm_g,
              rt_log_decay, w_out, w_ff1, w_ff2, final_g):
    L = x.shape[1]
    Lc = ctx.shape[1]
    rope = axial_rope(L)
    s_lat = jax.nn.silu(c)
    s_ctx = jax.nn.silu(c_ctx)[None]
    h_ctx = ctx
    for l in range(DEPTH):
        need_ctx = l < DEPTH - 1
        mx = jnp.split((s_lat @ w_mod[l] + b_mod[l])[:, None, :], N_MOD, axis=-1)
        mc = jnp.split((s_ctx @ w_mod[l] + b_mod[l])[:, None, :], N_MOD, axis=-1)
        px = modulate(rms_norm(x, norm1_g[l]), mx[0], mx[1]) @ w_in[l]
        pc = modulate(rms_norm(h_ctx, norm1_g[l]), mc[0], mc[1]) @ w_in[l]
        hy_x, ml_x, mg_x, rt_x = jnp.split(px, SPLITS, axis=-1)
        hy_c, ml_c, mg_c, rt_c = jnp.split(pc, SPLITS, axis=-1)
        filt_x = hyena_filter(L, hy_f_w1[l], hy_f_b1[l], hy_f_w2[l], hy_f_b2[l], hy_f_w3[l], hy_f_freq[l])
        y_hy = hyena_mixer(hy_x, hy_conv_w[l], filt_x, hy_bias[l])
        yc_ml, y_ml = mlstm_group(ml_c, mg_c, ml_x, mg_x, ml_conv_w[l], ml_gate_b[l], ml_norm_g[l])
        yc_rt, y_rt = retention_group(rt_c, rt_x, rt_log_decay[l], rope)
        x = x + mx[2] * (jnp.concatenate([y_hy, y_ml, y_rt], axis=-1) @ w_out[l])
        if need_ctx:
            filt_c = hyena_filter(Lc, hy_f_w1[l], hy_f_b1[l], hy_f_w2[l], hy_f_b2[l], hy_f_w3[l], hy_f_freq[l])
            yc_hy = hyena_mixer(hy_c, hy_conv_w[l], filt_c, hy_bias[l])
            h_ctx = h_ctx + mc[2] * (jnp.concatenate([yc_hy, yc_ml, yc_rt], axis=-1) @ w_out[l])
            h_ctx = h_ctx + mc[5] * sq_relu_mlp(modulate(rms_norm(h_ctx, norm2_g[l]), mc[3], mc[4]),
                                                w_ff1[l], w_ff2[l])
        x = x + mx[5] * sq_relu_mlp(modulate(rms_norm(x, norm2_g[l]), mx[3], mx[4]), w_ff1[l], w_ff2[l])
    return rms_norm(x, final_g)
```

```python
import functools
import math

import jax
import jax.numpy as jnp
import numpy as np
from jax import lax
from jax.experimental import pallas as pl
from jax.experimental.pallas import tpu as pltpu

F32 = jnp.float32
BF16 = jnp.bfloat16
HIGHEST = lax.Precision.HIGHEST

GRID_W = 64
HY_EMB = 33
HY_FF = 64
HY_DECAY_TARGET = 1e-2
HY_SHORT_PCT = 0.3
HY_LONG_PCT = 1.5
N_HEADS = 4
CHUNK = 128
ROPE_BASE = 10000.0
N_MOD = 6
EPS = 1e-6

LANE = 128
HEAD_PAD = 256
RTK_PAD = 128
MOD_ROWS = 16
VMEM_LIMIT = 56 * 1024 * 1024


def _cparams(sem):
    return pltpu.CompilerParams(dimension_semantics=sem, vmem_limit_bytes=VMEM_LIMIT)


def _sigmoid(x):
    return 1.0 / (1.0 + jnp.exp(-x))


def _silu(x):
    return x * _sigmoid(x)


def _log_sigmoid(x):
    return jnp.minimum(x, 0.0) - jnp.log(1.0 + jnp.exp(-jnp.abs(x)))


def _dot(a, b):
    return jnp.dot(a, b, preferred_element_type=F32)


def _dot_hi(a, b):
    return jnp.dot(a, b, precision=HIGHEST, preferred_element_type=F32)


def _dot_nt(a, b):
    return lax.dot_general(a, b, (((1,), (1,)), ((), ())), preferred_element_type=F32)


def _dot_tn(a, b):
    return lax.dot_general(a, b, (((0,), (0,)), ((), ())), preferred_element_type=F32)


def _resident(shape, index_map):
    return pl.BlockSpec(shape, index_map, pipeline_mode=pl.Buffered(1))


def _pick_tile(n, target):
    t = min(n, target)
    while n % t:
        t //= 2
    return t


def _mod_kernel(s_ref, w_ref, b_ref, o_ref):
    s = _silu(s_ref[...]).astype(BF16)
    w = w_ref[0].astype(BF16)
    o_ref[0] = _dot(s, w) + b_ref[0]


def _modulation(s_in, w_mod, b_mod):
    depth, d, n = w_mod.shape
    tn = _pick_tile(n, 1024)
    return pl.pallas_call(
        _mod_kernel,
        out_shape=jax.ShapeDtypeStruct((depth, MOD_ROWS, n), F32),
        grid=(depth, n // tn),
        in_specs=[
            pl.BlockSpec((MOD_ROWS, d), lambda l, j: (0, 0)),
            pl.BlockSpec((1, d, tn), lambda l, j: (l, 0, j)),
            pl.BlockSpec((1, 1, tn), lambda l, j: (l, 0, j)),
        ],
        out_specs=pl.BlockSpec((1, MOD_ROWS, tn), lambda l, j: (l, 0, j)),
        compiler_params=_cparams(("arbitrary", "arbitrary")),
        name="modulation",
    )(s_in, w_mod, b_mod.reshape(depth, 1, n))


def _norm_mod(x, g, shift, scale):
    ms = jnp.mean(x * x, axis=-1, keepdims=True)
    y = x * lax.rsqrt(ms + EPS) * g
    return y * (1.0 + scale) + shift


def _inproj_kernel(x_ref, g_ref, shift_ref, scale_ref, w_ref, wg_ref, p_ref, gate_ref, h_scr):
    @pl.when(pl.program_id(1) == 0)
    def _():
        h = _norm_mod(x_ref[...], g_ref[...], shift_ref[0], scale_ref[0]).astype(BF16)
        h_scr[...] = h
        gate_ref[...] = _dot(h, wg_ref[...])

    p_ref[...] = _dot(h_scr[...], w_ref[...]).astype(BF16)


def _in_proj(x2, norm_g, mod3, group_of_tile, w_p, w_g, tm):
    m, d = x2.shape
    n = w_p.shape[1]
    tn = _pick_tile(n, 512)
    return pl.pallas_call(
        _inproj_kernel,
        out_shape=(jax.ShapeDtypeStruct((m, n), BF16), jax.ShapeDtypeStruct((m, LANE), F32)),
        grid=(m // tm, n // tn),
        in_specs=[
            pl.BlockSpec((tm, d), lambda i, j: (i, 0)),
            pl.BlockSpec((1, d), lambda i, j: (0, 0)),
            pl.BlockSpec((1, 1, d), lambda i, j: (group_of_tile(i) * N_MOD + 0, 0, 0)),
            pl.BlockSpec((1, 1, d), lambda i, j: (group_of_tile(i) * N_MOD + 1, 0, 0)),
            pl.BlockSpec((d, tn), lambda i, j: (0, j)),
            pl.BlockSpec((d, LANE), lambda i, j: (0, 0)),
        ],
        out_specs=(pl.BlockSpec((tm, tn), lambda i, j: (i, j)),
                   pl.BlockSpec((tm, LANE), lambda i, j: (i, 0))),
        scratch_shapes=[pltpu.VMEM((tm, d), BF16)],
        compiler_params=_cparams(("arbitrary", "arbitrary")),
        name="in_proj",
    )(x2, norm_g.reshape(1, d), mod3, mod3, w_p, w_g)


def _outproj_kernel(x_ref, yh_ref, ym_ref, yr_ref, wh_ref, wm_ref, wr_ref, gate_ref, o_ref):
    acc = _dot(yh_ref[...], wh_ref[...])
    acc = acc + _dot(ym_ref[...], wm_ref[...])
    acc = acc + _dot(yr_ref[...], wr_ref[...])
    o_ref[...] = x_ref[...] + gate_ref[0] * acc


def _out_proj(x2, y_hy, y_ml, y_rt, wo_hy, wo_ml, wo_rt, mod3, group_of_tile, tm):
    m, d = x2.shape
    kh, km, kr = wo_hy.shape[0], wo_ml.shape[0], wo_rt.shape[0]
    return pl.pallas_call(
        _outproj_kernel,
        out_shape=jax.ShapeDtypeStruct((m, d), F32),
        grid=(m // tm,),
        in_specs=[
            pl.BlockSpec((tm, d), lambda i: (i, 0)),
            pl.BlockSpec((tm, kh), lambda i: (i, 0)),
            pl.BlockSpec((tm, km), lambda i: (i, 0)),
            pl.BlockSpec((tm, kr), lambda i: (i, 0)),
            _resident((kh, d), lambda i: (0, 0)),
            _resident((km, d), lambda i: (0, 0)),
            _resident((kr, d), lambda i: (0, 0)),
            pl.BlockSpec((1, 1, d), lambda i: (group_of_tile(i) * N_MOD + 2, 0, 0)),
        ],
        out_specs=pl.BlockSpec((tm, d), lambda i: (i, 0)),
        input_output_aliases={0: 0},
        compiler_params=_cparams(("arbitrary",)),
        name="out_proj",
    )(x2, y_hy, y_ml, y_rt, wo_hy, wo_ml, wo_rt, mod3)


def _ffn_kernel(x_ref, g_ref, shift_ref, scale_ref, gate_ref, w1_ref, w2_ref, fg_ref, o_ref, h_scr,
                *, final_norm):
    j = pl.program_id(1)

    @pl.when(j == 0)
    def _():
        h_scr[...] = _norm_mod(x_ref[...], g_ref[...], shift_ref[0], scale_ref[0]).astype(BF16)

    hid = jnp.maximum(_dot(h_scr[...], w1_ref[...]), 0.0)
    part = _dot((hid * hid).astype(BF16), w2_ref[...])

    @pl.when(j == 0)
    def _():
        o_ref[...] = part

    @pl.when(j > 0)
    def _():
        o_ref[...] += part

    @pl.when(j == pl.num_programs(1) - 1)
    def _():
        y = x_ref[...] + gate_ref[0] * o_ref[...]
        if final_norm:
            ms = jnp.mean(y * y, axis=-1, keepdims=True)
            y = y * lax.rsqrt(ms + EPS) * fg_ref[...]
        o_ref[...] = y


def _ffn(x2, norm_g, mod3, group_of_tile, w1, w2, final_g, final_norm, tm, tf):
    m, d = x2.shape
    f = w1.shape[1]
    return pl.pallas_call(
        functools.partial(_ffn_kernel, final_norm=final_norm),
        out_shape=jax.ShapeDtypeStruct((m, d), F32),
        grid=(m // tm, f // tf),
        in_specs=[
            pl.BlockSpec((tm, d), lambda i, j: (i, 0)),
            pl.BlockSpec((1, d), lambda i, j: (0, 0)),
            pl.BlockSpec((1, 1, d), lambda i, j: (group_of_tile(i) * N_MOD + 3, 0, 0)),
            pl.BlockSpec((1, 1, d), lambda i, j: (group_of_tile(i) * N_MOD + 4, 0, 0)),
            pl.BlockSpec((1, 1, d), lambda i, j: (group_of_tile(i) * N_MOD + 5, 0, 0)),
            pl.BlockSpec((d, tf), lambda i, j: (0, j)),
            pl.BlockSpec((tf, d), lambda i, j: (j, 0)),
            pl.BlockSpec((1, d), lambda i, j: (0, 0)),
        ],
        out_specs=pl.BlockSpec((tm, d), lambda i, j: (i, 0)),
        scratch_shapes=[pltpu.VMEM((tm, d), BF16)],
        input_output_aliases={0: 0},
        compiler_params=_cparams(("arbitrary", "arbitrary")),
        name="ffn",
    )(x2, norm_g.reshape(1, d), mod3, mod3, mod3, w1, w2, final_g.reshape(1, d))


def _hy_filter_kernel(z_ref, t_ref, w1_ref, b1_ref, w2_ref, b2_ref, w3f_ref, w3b_ref, fr_ref, dl_ref,
                      hf_ref, hb_ref):
    fr = fr_ref[...]
    hdn = jnp.sin(fr * (_dot_hi(z_ref[...], w1_ref[...]) + b1_ref[...]))
    hdn = jnp.sin(fr * (_dot_hi(hdn, w2_ref[...]) + b2_ref[...]))
    win = jnp.exp(-t_ref[...] * dl_ref[...])
    hf = _dot_hi(hdn, w3f_ref[...]) * win
    hb = _dot_hi(hdn, w3b_ref[...]) * win
    row = lax.broadcasted_iota(jnp.int32, hb.shape, 0)
    hb = jnp.where(row == 0, 0.0, hb)
    inv = 1.0 / (jnp.sum(jnp.abs(hf), axis=0, keepdims=True) + jnp.sum(jnp.abs(hb), axis=0, keepdims=True))
    hf_ref[...] = hf * inv
    hb_ref[...] = hb * inv


def _hyena_filter(seq, w1, b1, w2, b2, w3, freq):
    hy_w = w3.shape[1] // 2
    t = jnp.linspace(0.0, 1.0, seq, dtype=F32)[:, None]
    w = (2.0 * math.pi / seq) * jnp.arange(seq, dtype=F32)[:, None]
    nb = (HY_EMB - 1) // 2
    bands = jnp.linspace(1e-4, nb - 1, nb, dtype=F32)[None, :]
    z = jnp.concatenate([t, jnp.cos(bands * w), -jnp.sin(bands * w)], axis=-1)
    z = jnp.pad(z, ((0, 0), (0, LANE - HY_EMB)))
    pad_ff = LANE - HY_FF
    w1p = jnp.pad(w1, ((0, LANE - HY_EMB), (0, pad_ff)))
    w2p = jnp.pad(w2, ((0, pad_ff), (0, pad_ff)))
    w3p = jnp.pad(w3, ((0, pad_ff), (0, 0))).reshape(LANE, 2, hy_w)
    padv = lambda v: jnp.pad(v, (0, pad_ff)).reshape(1, LANE)
    deltas = jnp.abs(jnp.linspace(math.log(HY_DECAY_TARGET) / HY_LONG_PCT,
                                  math.log(HY_DECAY_TARGET) / HY_SHORT_PCT, hy_w, dtype=F32))[None, :]
    out = jax.ShapeDtypeStruct((seq, hy_w), F32)
    return pl.pallas_call(
        _hy_filter_kernel,
        out_shape=(out, out),
        compiler_params=pltpu.CompilerParams(vmem_limit_bytes=VMEM_LIMIT),
        name="hyena_filter",
    )(z, t, w1p, padv(b1), w2p, padv(b2), w3p[:, 0], w3p[:, 1], padv(freq), deltas)


def _dft_mats(seq):
    k = jnp.arange(seq, dtype=jnp.int32)
    mm = ((2 * k[:, None] + 1) * (2 * k[None, :] + 1)) % (8 * seq)
    ang = mm.astype(F32) * (2.0 * math.pi / (8 * seq))
    theta = (k.astype(F32) + 0.5) * (math.pi / (2 * seq))
    return (jnp.cos(ang).astype(BF16), jnp.sin(ang).astype(BF16),
            jnp.cos(theta)[:, None], jnp.sin(theta)[:, None])


def _split_bf16(a):
    hi = a.astype(BF16)
    lo = (a - hi.astype(F32)).astype(BF16)
    return hi, lo


def _hy_spec_kernel(hf_ref, hb_ref, c_ref, s_ref, ct_ref, st_ref, hre_ref, him_ref):
    cm, sm = c_ref[...], s_ref[...]
    seq = cm.shape[0]

    def xform(h):
        hi, lo = _split_bf16(h)
        return _dot(cm, hi) + _dot(cm, lo), _dot(sm, hi) + _dot(sm, lo)

    a, b = xform(hf_ref[...])
    a2, b2 = xform(hb_ref[...])
    ct, st = ct_ref[...], st_ref[...]
    scale = 1.0 / seq
    hre_ref[...] = ((a + a2) * ct + (b + b2) * st) * scale
    him_ref[...] = ((a - a2) * st - (b - b2) * ct) * scale


def _hyena_spectrum(hf, hb, cm, sm, ct, st, tc):
    seq, c = hf.shape
    out = jax.ShapeDtypeStruct((seq, c), F32)
    col = pl.BlockSpec((seq, tc), lambda j: (0, j))
    return pl.pallas_call(
        _hy_spec_kernel,
        out_shape=(out, out),
        grid=(c // tc,),
        in_specs=[col, col,
                  _resident((seq, seq), lambda j: (0, 0)), _resident((seq, seq), lambda j: (0, 0)),
                  pl.BlockSpec((seq, 1), lambda j: (0, 0)), pl.BlockSpec((seq, 1), lambda j: (0, 0))],
        out_specs=(col, col),
        compiler_params=_cparams(("arbitrary",)),
        name="hyena_spectrum",
    )(hf, hb, cm, sm, ct, st)


def _short_conv(p, w):
    n = p.shape[0]
    row = lax.broadcasted_iota(jnp.int32, p.shape, 0)
    prev = jnp.where(row == 0, 0.0, pltpu.roll(p, 1, axis=0))
    nxt = jnp.where(row == n - 1, 0.0, pltpu.roll(p, n - 1, axis=0))
    return prev * w[0:1] + p * w[1:2] + nxt * w[2:3]


def _hy_conv_kernel(p0_ref, p1_ref, pv_ref, w0_ref, w1_ref, wv_ref, hre_ref, him_ref, bias_ref,
                    c_ref, s_ref, o_ref):
    x1 = _short_conv(p1_ref[...].astype(F32), w1_ref[...])
    v = _short_conv(pv_ref[...].astype(F32), wv_ref[...])
    u = v * x1
    ub = u.astype(BF16)
    cm, sm = c_ref[...], s_ref[...]
    zc = _dot(cm, ub)
    zs = _dot(sm, ub)
    hre, him = hre_ref[...], him_ref[...]
    yre = (hre * zc + him * zs).astype(BF16)
    yim = (him * zc - hre * zs).astype(BF16)
    y = _dot(cm, yre) - _dot(sm, yim)
    x0 = _short_conv(p0_ref[...].astype(F32), w0_ref[...])
    o_ref[...] = ((y + u * bias_ref[...]) * x0).astype(BF16)


def _hyena_conv(p, seq, conv_w, hre, him, bias, cm, sm, tc):
    m = p.shape[0]
    hy_w = hre.shape[1]
    nb = hy_w // tc
    pspec = lambda off: pl.BlockSpec((seq, tc), lambda j, b, off=off: (b, off * nb + j))
    wspec = lambda off: pl.BlockSpec((3, tc), lambda j, b, off=off: (0, off * nb + j))
    col = pl.BlockSpec((seq, tc), lambda j, b: (0, j))
    return pl.pallas_call(
        _hy_conv_kernel,
        out_shape=jax.ShapeDtypeStruct((m, hy_w), BF16),
        grid=(nb, m // seq),
        in_specs=[pspec(0), pspec(1), pspec(2), wspec(0), wspec(1), wspec(2), col, col,
                  pl.BlockSpec((1, tc), lambda j, b: (0, j)),
                  _resident((seq, seq), lambda j, b: (0, 0)), _resident((seq, seq), lambda j, b: (0, 0))],
        out_specs=pl.BlockSpec((seq, tc), lambda j, b: (b, j)),
        compiler_params=_cparams(("arbitrary", "arbitrary")),
        name="hyena_conv",
    )(p, p, p, conv_w, conv_w, conv_w, hre, him, bias.reshape(1, hy_w), cm, sm)


def _tri(n, upper):
    r = lax.broadcasted_iota(jnp.int32, (n, n), 0)
    c = lax.broadcasted_iota(jnp.int32, (n, n), 1)
    return (r <= c) if upper else (r >= c)


def _mlstm_chunk(q, k, v, lf_cum_col, li_col, lf_cum_row, li_row, c_ref, m_ref, reverse):
    t = q.shape[0]
    m_prev = m_ref[...]
    c_prev = c_ref[...]
    mask = _tri(t, upper=reverse)
    end = 0 if reverse else t - 1
    b_end = lf_cum_col[end:end + 1, :]
    dm = jnp.where(mask, lf_cum_col - lf_cum_row + li_row, -jnp.inf)
    m_inter = lf_cum_col + m_prev
    m_t = jnp.maximum(m_inter, jnp.max(dm, axis=-1, keepdims=True))
    s = _dot_nt(q, k) * jnp.exp(dm - m_t)
    w_inter = jnp.exp(m_inter - m_t)
    num = _dot(s.astype(BF16), v) + w_inter * _dot(q, c_prev.astype(BF16))
    den = num[:, ML_ONES_COL:ML_ONES_COL + 1]
    h = num / jnp.maximum(jnp.abs(den), jnp.exp(-m_t))
    g = b_end - lf_cum_col + li_col
    m_new = jnp.maximum(b_end + m_prev, jnp.max(g, axis=0, keepdims=True))
    wk = jnp.exp(g - m_new)
    kw = (k.astype(F32) * wk).astype(BF16)
    c_ref[...] = jnp.exp(b_end + m_prev - m_new) * c_prev + _dot_tn(kw, v)
    m_ref[...] = m_new
    return h


ML_ONES_COL = 192


def _mlstm_kernel(qx_ref, kx_ref, vx_ref, ox_ref, gcx_ref, grx_ref,
                  qc_ref, kc_ref, vc_ref, oc_ref, gcc_ref, grc_ref,
                  wq_ref, wk_ref, gbc_ref, gbr_ref, ng_ref,
                  yx_ref, yc_ref,
                  qsx, ksx, vsx, qsc, ksc, vsc, hfx, hbx, hfc, hbc, c_scr, m_scr,
                  *, head_dim, write_ctx):
    hb_idx = pl.program_id(1)
    t = CHUNK
    width = qx_ref.shape[1]
    hpb = width // HEAD_PAD
    kscale = head_dim ** -0.5
    lane = lax.broadcasted_iota(jnp.int32, (1, width), 1) % HEAD_PAD

    def prep(q_ref, k_ref, v_ref, qs, ks, vs):
        qs[...] = _silu(_short_conv(q_ref[...].astype(F32), wq_ref[...])).astype(BF16)
        ks[...] = (_silu(_short_conv(k_ref[...].astype(F32), wk_ref[...])) * kscale).astype(BF16)
        vs[...] = jnp.where(lane == ML_ONES_COL, 1.0, v_ref[...].astype(F32)).astype(BF16)

    prep(qx_ref, kx_ref, vx_ref, qsx, ksx, vsx)
    prep(qc_ref, kc_ref, vc_ref, qsc, ksc, vsc)
    c_scr[...] = jnp.zeros_like(c_scr)
    m_scr[...] = jnp.zeros_like(m_scr)

    tri_lo = _tri(t, upper=False).astype(F32)
    tri_up = _tri(t, upper=True).astype(F32)
    gb_col = gbc_ref[...]
    gb_row = gbr_ref[...]

    def gates(gcol_ref, grow_ref, ci, reverse):
        rows = pl.ds(pl.multiple_of(ci * t, t), t)
        gc = gcol_ref[rows, :] + gb_col
        gr = grow_ref[ci] + gb_row
        cum_c = _dot_hi(tri_up if reverse else tri_lo, _log_sigmoid(gc))
        cum_r = _dot_hi(_log_sigmoid(gr), tri_lo if reverse else tri_up)
        return gc, gr, cum_c, cum_r

    def run(ci_f, ci_b, q_s, k_s, v_s, gcol_ref, grow_ref, hf, hb):
        for reverse, ci, hout in ((False, ci_f, hf), (True, ci_b, hb)):
            gc, gr, cum_c, cum_r = gates(gcol_ref, grow_ref, ci, reverse)
            rows = pl.ds(pl.multiple_of(ci * t, t), t)
            for hh in range(hpb):
                head = hb_idx * hpb + hh
                icol = (2 * N_HEADS if reverse else 0) + head
                fcol = icol + N_HEADS
                sel_c = lax.broadcasted_iota(jnp.int32, (1, LANE), 1)
                sel_r = lax.broadcasted_iota(jnp.int32, (4 * N_HEADS, 1), 0)
                pick_c = lambda a, c: jnp.sum(jnp.where(sel_c == c, a, 0.0), axis=1, keepdims=True)
                pick_r = lambda a, c: jnp.sum(jnp.where(sel_r == c, a, 0.0), axis=0, keepdims=True)
                cols = slice(hh * HEAD_PAD, (hh + 1) * HEAD_PAD)
                chain = hh * 2 + (1 if reverse else 0)
                h = _mlstm_chunk(q_s[rows, cols], k_s[rows, cols], v_s[rows, cols],
                                 pick_c(cum_c, fcol), pick_c(gc, icol), pick_r(cum_r, fcol), pick_r(gr, icol),
                                 c_scr.at[chain], m_scr.at[chain], reverse)
                hout[rows, cols] = h

    nc_c = qc_ref.shape[0] // t
    nc_x = qx_ref.shape[0] // t

    def body_c(i, carry):
        run(i, nc_c - 1 - i, qsc, ksc, vsc, gcc_ref, grc_ref, hfc, hbc)
        return carry

    def body_x(i, carry):
        run(i, nc_x - 1 - i, qsx, ksx, vsx, gcx_ref, grx_ref, hfx, hbx)
        return carry

    lax.fori_loop(0, nc_c, body_c, 0)
    lax.fori_loop(0, nc_x, body_x, 0)

    def finish(hf, hb, o_ref, y_ref):
        h = jnp.where(lane < head_dim, hf[...] + hb[...], 0.0)
        outs = []
        for hh in range(hpb):
            hs = h[:, hh * HEAD_PAD:(hh + 1) * HEAD_PAD]
            ms = jnp.sum(hs * hs, axis=-1, keepdims=True) * (1.0 / head_dim)
            outs.append(hs * lax.rsqrt(ms + EPS))
        hn = outs[0] if hpb == 1 else jnp.concatenate(outs, axis=1)
        y_ref[...] = (hn * ng_ref[...] * _sigmoid(o_ref[...].astype(F32))).astype(BF16)

    finish(hfx, hbx, ox_ref, yx_ref)
    if write_ctx:
        finish(hfc, hbc, oc_ref, yc_ref)
    else:
        yc_ref[...] = jnp.zeros_like(yc_ref)


def _mlstm(px, pc, gx, gc, seq, seq_c, col0, conv_w, gate_b, norm_g, head_dim, write_ctx, hpb):
    batch = px.shape[0] // seq
    width = hpb * HEAD_PAD
    nhb = N_HEADS // hpb
    total = N_HEADS * HEAD_PAD
    cb = col0 // width
    seg = total // width
    t = CHUNK

    def pspecs(s):
        return [pl.BlockSpec((s, width), lambda b, h, k=k: (b, cb + k * seg + h)) for k in range(4)]

    def gspecs(s):
        return [pl.BlockSpec((s, LANE), lambda b, h: (b, 0)),
                pl.BlockSpec((s // t, 4 * N_HEADS, t), lambda b, h: (b, 0, 0))]

    def rowform(g, s):
        return g[:, :4 * N_HEADS].reshape(-1, t, 4 * N_HEADS).transpose(0, 2, 1)

    gb = gate_b.astype(F32)
    gb_col = jnp.pad(gb, (0, LANE - gb.shape[0])).reshape(1, LANE)
    gb_row = gb.reshape(-1, 1)
    scr = lambda s, dt: pltpu.VMEM((s, width), dt)
    kern = functools.partial(_mlstm_kernel, head_dim=head_dim, write_ctx=write_ctx)
    return pl.pallas_call(
        kern,
        out_shape=(jax.ShapeDtypeStruct((batch * seq, total), BF16),
                   jax.ShapeDtypeStruct((batch * seq_c, total), BF16)),
        grid=(batch, nhb),
        in_specs=(pspecs(seq) + gspecs(seq) + pspecs(seq_c) + gspecs(seq_c) + [
            pl.BlockSpec((3, width), lambda b, h: (0, h)),
            pl.BlockSpec((3, width), lambda b, h: (0, nhb + h)),
            pl.BlockSpec((1, LANE), lambda b, h: (0, 0)),
            pl.BlockSpec((4 * N_HEADS, 1), lambda b, h: (0, 0)),
            pl.BlockSpec((1, width), lambda b, h: (0, h)),
        ]),
        out_specs=(pl.BlockSpec((seq, width), lambda b, h: (b, h)),
                   pl.BlockSpec((seq_c, width), lambda b, h: (b, h))),
        scratch_shapes=[scr(seq, BF16), scr(seq, BF16), scr(seq, BF16),
                        scr(seq_c, BF16), scr(seq_c, BF16), scr(seq_c, BF16),
                        scr(seq, F32), scr(seq, F32), scr(seq_c, F32), scr(seq_c, F32),
                        pltpu.VMEM((2 * hpb, HEAD_PAD, HEAD_PAD), F32),
                        pltpu.VMEM((2 * hpb, 1, 1), F32)],
        compiler_params=_cparams(("arbitrary", "arbitrary")),
        name="mlstm",
    )(px, px, px, px, gx, rowform(gx, seq), pc, pc, pc, pc, gc, rowform(gc, seq_c),
      conv_w, conv_w, gb_col, gb_row, norm_g)


def _ret_chunk(q, k, v, dmask, q_decay, k_decay, c_decay, s_ref):
    s_prev = s_ref[...]
    s = _dot_nt(q, k) * dmask
    o = _dot(s.astype(BF16), v) + q_decay * _dot(q, s_prev.astype(BF16))
    kd = (k.astype(F32) * k_decay).astype(BF16)
    s_ref[...] = c_decay * s_prev + _dot_tn(kd, v)
    return o


def _ret_kernel(ld_ref, qx_ref, kx_ref, vx_ref, gx_ref, qc_ref, kc_ref, vc_ref, gc_ref, cos_ref, sin_ref,
                yx_ref, yc_ref,
                qsx, ksx, qsc, ksc, ofx, obx, ofc, obc, s_scr,
                *, key_dim, val_dim, write_ctx):
    hb_idx = pl.program_id(1)
    t = CHUNK
    hpb = vx_ref.shape[1] // HEAD_PAD
    qscale = key_dim ** -0.5

    cosf, sinf = cos_ref[...], sin_ref[...]
    for hh in range(hpb):
        cols = slice(hh * RTK_PAD, (hh + 1) * RTK_PAD)
        q = qx_ref[:, cols].astype(F32)
        k = kx_ref[:, cols].astype(F32)
        qsx[:, cols] = ((q * cosf + pltpu.roll(q, RTK_PAD // 2, axis=1) * sinf) * qscale).astype(BF16)
        ksx[:, cols] = (k * cosf + pltpu.roll(k, RTK_PAD // 2, axis=1) * sinf).astype(BF16)
    qsc[...] = (qc_ref[...].astype(F32) * qscale).astype(BF16)
    ksc[...] = kc_ref[...]
    s_scr[...] = jnp.zeros_like(s_scr)

    r = lax.broadcasted_iota(jnp.int32, (t, t), 0)
    c = lax.broadcasted_iota(jnp.int32, (t, t), 1)
    idx = lax.broadcasted_iota(jnp.int32, (t, 1), 0).astype(F32)
    consts = []
    for hh in range(hpb):
        for reverse in (False, True):
            lg = -jnp.exp(jnp.full((1, 1), ld_ref[1 if reverse else 0, hb_idx * hpb + hh], F32))
            rel = (c - r) if reverse else (r - c)
            dmask = jnp.where(rel >= 0, jnp.exp(lg * jnp.maximum(rel, 0).astype(F32)), 0.0)
            pos = (t - 1.0 - idx) if reverse else idx
            consts.append((dmask, jnp.exp(lg * (pos + 1.0)), jnp.exp(lg * (t - 1.0 - pos)), jnp.exp(lg * t)))

    def run(ci_f, ci_b, q_s, k_s, v_ref, of, ob):
        for reverse, ci, oout in ((False, ci_f, of), (True, ci_b, ob)):
            rows = pl.ds(pl.multiple_of(ci * t, t), t)
            for hh in range(hpb):
                chain = hh * 2 + (1 if reverse else 0)
                dmask, q_decay, k_decay, c_decay = consts[chain]
                kcols = slice(hh * RTK_PAD, (hh + 1) * RTK_PAD)
                vcols = slice(hh * HEAD_PAD, (hh + 1) * HEAD_PAD)
                oout[rows, vcols] = _ret_chunk(q_s[rows, kcols], k_s[rows, kcols], v_ref[rows, vcols],
                                               dmask, q_decay, k_decay, c_decay, s_scr.at[chain])

    nc_c = qc_ref.shape[0] // t
    nc_x = qx_ref.shape[0] // t

    def body_c(i, carry):
        run(i, nc_c - 1 - i, qsc, ksc, vc_ref, ofc, obc)
        return carry

    def body_x(i, carry):
        run(i, nc_x - 1 - i, qsx, ksx, vx_ref, ofx, obx)
        return carry

    lax.fori_loop(0, nc_c, body_c, 0)
    lax.fori_loop(0, nc_x, body_x, 0)

    def finish(of, ob, g_ref, y_ref):
        o = of[...] + ob[...]
        outs = []
        for hh in range(hpb):
            os_ = o[:, hh * HEAD_PAD:(hh + 1) * HEAD_PAD]
            ms = jnp.sum(os_ * os_, axis=-1, keepdims=True) * (1.0 / val_dim)
            outs.append(os_ * lax.rsqrt(ms + EPS))
        on = outs[0] if hpb == 1 else jnp.concatenate(outs, axis=1)
        y_ref[...] = (on * _silu(g_ref[...].astype(F32))).astype(BF16)

    finish(ofx, obx, gx_ref, yx_ref)
    if write_ctx:
        finish(ofc, obc, gc_ref, yc_ref)
    else:
        yc_ref[...] = jnp.zeros_like(yc_ref)


def _retention(px, pc, seq, seq_c, col0, log_decay, cosf, sinf, key_dim, val_dim, write_ctx, hpb):
    batch = px.shape[0] // seq
    kw, vw = hpb * RTK_PAD, hpb * HEAD_PAD
    nhb = N_HEADS // hpb
    ktot, vtot = N_HEADS * RTK_PAD, N_HEADS * HEAD_PAD
    q0, k0, v0, g0 = col0, col0 + ktot, col0 + 2 * ktot, col0 + 2 * ktot + vtot

    def pspecs(s):
        return [pl.BlockSpec((s, kw), lambda b, h: (b, q0 // kw + h)),
                pl.BlockSpec((s, kw), lambda b, h: (b, k0 // kw + h)),
                pl.BlockSpec((s, vw), lambda b, h: (b, v0 // vw + h)),
                pl.BlockSpec((s, vw), lambda b, h: (b, g0 // vw + h))]

    kern = functools.partial(_ret_kernel, key_dim=key_dim, val_dim=val_dim, write_ctx=write_ctx)
    return pl.pallas_call(
        kern,
        out_shape=(jax.ShapeDtypeStruct((batch * seq, vtot), BF16),
                   jax.ShapeDtypeStruct((batch * seq_c, vtot), BF16)),
        grid=(batch, nhb),
        in_specs=([pl.BlockSpec(memory_space=pltpu.SMEM)] + pspecs(seq) + pspecs(seq_c) + [
            pl.BlockSpec((seq, RTK_PAD), lambda b, h: (0, 0)),
            pl.BlockSpec((seq, RTK_PAD), lambda b, h: (0, 0)),
        ]),
        out_specs=(pl.BlockSpec((seq, vw), lambda b, h: (b, h)),
                   pl.BlockSpec((seq_c, vw), lambda b, h: (b, h))),
        scratch_shapes=[pltpu.VMEM((seq, kw), BF16), pltpu.VMEM((seq, kw), BF16),
                        pltpu.VMEM((seq_c, kw), BF16), pltpu.VMEM((seq_c, kw), BF16),
                        pltpu.VMEM((seq, vw), F32), pltpu.VMEM((seq, vw), F32),
                        pltpu.VMEM((seq_c, vw), F32), pltpu.VMEM((seq_c, vw), F32),
                        pltpu.VMEM((2 * hpb, RTK_PAD, HEAD_PAD), F32)],
        compiler_params=_cparams(("arbitrary", "arbitrary")),
        name="retention",
    )(log_decay.astype(F32), px, px, px, px, pc, pc, pc, pc, cosf, sinf)


def _rope_tables(seq, key_dim):
    rows = seq // GRID_W
    r = jnp.repeat(jnp.arange(rows, dtype=F32), GRID_W)
    col = jnp.tile(jnp.arange(GRID_W, dtype=F32), rows)
    nf = key_dim // 4
    inv = ROPE_BASE ** (-jnp.arange(nf, dtype=F32) / nf)
    ang = jnp.concatenate([r[:, None] * inv, col[:, None] * inv], axis=-1)
    half = RTK_PAD // 2
    pad = lambda a: jnp.pad(a, ((0, 0), (0, half - a.shape[1])))
    cos, sin = pad(jnp.cos(ang)), pad(jnp.sin(ang))
    return jnp.concatenate([cos, cos], axis=1), jnp.concatenate([-sin, sin], axis=1)


def _pad_heads(w, n_heads, width):
    lead = w.shape[:-1]
    dh = w.shape[-1] // n_heads
    w = w.reshape(*lead, n_heads, dh)
    w = jnp.pad(w, [(0, 0)] * len(lead) + [(0, 0), (0, width - dh)])
    return w.reshape(*lead, n_heads * width)


def _pad_rope_heads(w, n_heads):
    lead = w.shape[:-1]
    half = w.shape[-1] // n_heads // 2
    w = w.reshape(*lead, n_heads, 2, half)
    w = jnp.pad(w, [(0, 0)] * len(lead) + [(0, 0), (0, 0), (0, RTK_PAD // 2 - half)])
    return w.reshape(*lead, n_heads * RTK_PAD)


def kernel(x, c, ctx, c_ctx, norm1_g, norm2_g, w_mod, b_mod, w_in, hy_conv_w, hy_f_w1, hy_f_b1, hy_f_w2, hy_f_b2, hy_f_w3, hy_f_freq, hy_bias, ml_conv_w, ml_gate_b, ml_norm_g, rt_log_decay, w_out, w_ff1, w_ff2, final_g):
    batch, seq, d = x.shape
    seq_c = ctx.shape[1]
    depth = w_mod.shape[0]
    hy_w = d // 4
    ml_w = 3 * d // 8
    rt_w = 3 * d // 8
    ml_dh = ml_w // N_HEADS
    rt_dv = rt_w // N_HEADS
    rt_dk = rt_dv // 2
    hy_cols, ml_cols, mlg_cols = 3 * hy_w, 4 * ml_w, 4 * N_HEADS
    nqk = N_HEADS * rt_dk

    s_in = jnp.concatenate([c, c_ctx[None], jnp.zeros((MOD_ROWS - batch - 1, d), F32)], axis=0)
    mod = _modulation(s_in, w_mod, b_mod)

    tm_x = _pick_tile(seq, 1024)
    tm_c = _pick_tile(seq_c, 1024)
    grp_x = lambda tm: (lambda i: (i * tm) // seq)
    grp_c = lambda tm: (lambda i: batch + 0 * i)

    cm_x, sm_x, ct_x, st_x = _dft_mats(seq)
    cm_c, sm_c, ct_c, st_c = _dft_mats(seq_c)
    cosf, sinf = _rope_tables(seq, rt_dk)
    tc = _pick_tile(hy_w, 256)
    tc_spec = _pick_tile(hy_w, LANE)

    xs = x.reshape(batch * seq, d)
    hc = ctx.reshape(batch * seq_c, d)
    for l in range(depth):
        need_ctx = l < depth - 1
        mod3 = mod[l].reshape(MOD_ROWS * N_MOD, 1, d)
        wl = w_in[l]
        s0, s1, s2 = hy_cols, hy_cols + ml_cols, hy_cols + ml_cols + mlg_cols
        w_ml = wl[:, s0:s1]
        w_rt = wl[:, s2:]
        w_p = jnp.concatenate(
            [wl[:, :s0]]
            + [_pad_heads(w_ml[:, k * ml_w:(k + 1) * ml_w], N_HEADS, HEAD_PAD) for k in range(4)]
            + [_pad_rope_heads(w_rt[:, :nqk], N_HEADS), _pad_rope_heads(w_rt[:, nqk:2 * nqk], N_HEADS),
               _pad_heads(w_rt[:, 2 * nqk:2 * nqk + rt_w], N_HEADS, HEAD_PAD),
               _pad_heads(w_rt[:, 2 * nqk + rt_w:], N_HEADS, HEAD_PAD)], axis=1).astype(BF16)
        w_g = jnp.pad(wl[:, s1:s2], ((0, 0), (0, LANE - mlg_cols))).astype(BF16)
        ml_col0 = hy_cols
        rt_col0 = hy_cols + 4 * N_HEADS * HEAD_PAD
        ml_cw = jnp.concatenate([_pad_heads(ml_conv_w[l][:, :ml_w], N_HEADS, HEAD_PAD),
                                 _pad_heads(ml_conv_w[l][:, ml_w:], N_HEADS, HEAD_PAD)], axis=1)
        ml_ng = _pad_heads(ml_norm_g[l][None, :], N_HEADS, HEAD_PAD)
        wo = w_out[l]
        wo_hy = wo[:hy_w].astype(BF16)
        wo_ml = _pad_heads(wo[hy_w:hy_w + ml_w].T, N_HEADS, HEAD_PAD).T.astype(BF16)
        wo_rt = _pad_heads(wo[hy_w + ml_w:].T, N_HEADS, HEAD_PAD).T.astype(BF16)
        w1 = w_ff1[l].astype(BF16)
        w2 = w_ff2[l].astype(BF16)

        px, gx = _in_proj(xs, norm1_g[l], mod3, grp_x(tm_x), w_p, w_g, tm_x)
        pc, gc = _in_proj(hc, norm1_g[l], mod3, grp_c(tm_c), w_p, w_g, tm_c)

        hf, hb = _hyena_filter(seq, hy_f_w1[l], hy_f_b1[l], hy_f_w2[l], hy_f_b2[l], hy_f_w3[l], hy_f_freq[l])
        hre, him = _hyena_spectrum(hf, hb, cm_x, sm_x, ct_x, st_x, tc_spec)
        y_hy = _hyena_conv(px, seq, hy_conv_w[l], hre, him, hy_bias[l], cm_x, sm_x, tc)
        y_ml, yc_ml = _mlstm(px, pc, gx, gc, seq, seq_c, ml_col0, ml_cw, ml_gate_b[l], ml_ng,
                             ml_dh, need_ctx, hpb=1)
        y_rt, yc_rt = _retention(px, pc, seq, seq_c, rt_col0, rt_log_decay[l], cosf, sinf,
                                 rt_dk, rt_dv, need_ctx, hpb=1)
        tm_o = _pick_tile(seq, 512)
        xs = _out_proj(xs, y_hy, y_ml, y_rt, wo_hy, wo_ml, wo_rt, mod3, grp_x(tm_o), tm_o)
        if need_ctx:
            hfc, hbc = _hyena_filter(seq_c, hy_f_w1[l], hy_f_b1[l], hy_f_w2[l], hy_f_b2[l], hy_f_w3[l],
                                     hy_f_freq[l])
            hre_c, him_c = _hyena_spectrum(hfc, hbc, cm_c, sm_c, ct_c, st_c, tc_spec)
            yc_hy = _hyena_conv(pc, seq_c, hy_conv_w[l], hre_c, him_c, hy_bias[l], cm_c, sm_c, tc)
            tm_oc = _pick_tile(seq_c, 512)
            hc = _out_proj(hc, yc_hy, yc_ml, yc_rt, wo_hy, wo_ml, wo_rt, mod3, grp_c(tm_oc), tm_oc)
            hc = _ffn(hc, norm2_g[l], mod3, grp_c(tm_oc), w1, w2, final_g, False, tm_oc, 512)
        xs = _ffn(xs, norm2_g[l], mod3, grp_x(tm_o), w1, w2, final_g, l == depth - 1, tm_o, 512)
    return xs.reshape(batch, seq, d)
```

```python
import functools
import math

import jax
import jax.numpy as jnp
import numpy as np
from jax import lax
from jax.experimental import pallas as pl
from jax.experimental.pallas import tpu as pltpu

F32 = jnp.float32
BF16 = jnp.bfloat16

GRID_W = 64
HY_EMB = 33
HY_FF = 64
HY_DECAY_TARGET = 1e-2
HY_SHORT_PCT = 0.3
HY_LONG_PCT = 1.5
N_HEADS = 4
N_GATES = 4 * N_HEADS
CHUNK = 128
ROPE_BASE = 10000.0
N_MOD = 6
EPS = 1e-6

LANE = 128
MXU_W = 256
WIN = 2 * LANE
PAIR_W = 3 * LANE
MOD_ROWS = 16
VMEM_LIMIT = 56 * 1024 * 1024


def _cparams(sem):
    return pltpu.CompilerParams(dimension_semantics=sem, vmem_limit_bytes=VMEM_LIMIT)


def _sigmoid(x):
    return 1.0 / (1.0 + jnp.exp(-x))


def _silu(x):
    return x * _sigmoid(x)


def _log_sigmoid(x):
    return jnp.minimum(x, 0.0) - jnp.log(1.0 + jnp.exp(-jnp.abs(x)))


def _dot(a, b):
    return jnp.dot(a, b, preferred_element_type=F32)


def _dot_nt(a, b):
    return lax.dot_general(a, b, (((1,), (1,)), ((), ())), preferred_element_type=F32)


def _dot_tn(a, b):
    return lax.dot_general(a, b, (((0,), (0,)), ((), ())), preferred_element_type=F32)


def _split_bf16(a):
    hi = a.astype(BF16)
    lo = (a - hi.astype(F32)).astype(BF16)
    return hi, lo


def _dot_split(a, b_f32):
    hi, lo = _split_bf16(b_f32)
    return _dot(a, hi) + _dot(a, lo)


def _resident(shape, index_map):
    return pl.BlockSpec(shape, index_map, pipeline_mode=pl.Buffered(1))


def _pick_tile(n, target):
    t = min(n, target)
    while n % t:
        t //= 2
    return t


def _norm_mod(x, g, shift, scale):
    ms = jnp.mean(x * x, axis=-1, keepdims=True)
    y = x * lax.rsqrt(ms + EPS) * g
    return y * (1.0 + scale) + shift


def _mod_kernel(s_ref, w_ref, b_ref, o_ref):
    s = _silu(s_ref[...]).astype(BF16)
    w = w_ref[0].astype(BF16)
    o_ref[0] = _dot(s, w) + b_ref[0]


def _modulation(s_in, w_mod, b_mod):
    depth, d, n = w_mod.shape
    tn = _pick_tile(n, 1024)
    return pl.pallas_call(
        _mod_kernel,
        out_shape=jax.ShapeDtypeStruct((depth, MOD_ROWS, n), F32),
        grid=(depth, n // tn),
        in_specs=[
            pl.BlockSpec((MOD_ROWS, d), lambda l, j: (0, 0)),
            pl.BlockSpec((1, d, tn), lambda l, j: (l, 0, j)),
            pl.BlockSpec((1, 1, tn), lambda l, j: (l, 0, j)),
        ],
        out_specs=pl.BlockSpec((1, MOD_ROWS, tn), lambda l, j: (l, 0, j)),
        compiler_params=_cparams(("arbitrary", "arbitrary")),
        name="modulation",
    )(s_in, w_mod, b_mod.reshape(depth, 1, n))


def _inproj_kernel(x_ref, g_ref, shift_ref, scale_ref, w_ref, wg_ref, p_ref, gate_ref, h_scr):
    @pl.when(pl.program_id(1) == 0)
    def _():
        h = _norm_mod(x_ref[...], g_ref[...], shift_ref[0], scale_ref[0]).astype(BF16)
        h_scr[...] = h
        gate_ref[...] = _dot(h, wg_ref[...])

    p_ref[...] = _dot(h_scr[...], w_ref[...]).astype(BF16)


def _in_proj(x2, norm_g, mod3, group_of_tile, w_p, w_g, tm):
    m, d = x2.shape
    n = w_p.shape[1]
    tn = _pick_tile(n, 1024)
    return pl.pallas_call(
        _inproj_kernel,
        out_shape=(jax.ShapeDtypeStruct((m, n), BF16), jax.ShapeDtypeStruct((m, LANE), F32)),
        grid=(m // tm, n // tn),
        in_specs=[
            pl.BlockSpec((tm, d), lambda i, j: (i, 0)),
            pl.BlockSpec((1, d), lambda i, j: (0, 0)),
            pl.BlockSpec((1, 1, d), lambda i, j: (group_of_tile(i) * N_MOD + 0, 0, 0)),
            pl.BlockSpec((1, 1, d), lambda i, j: (group_of_tile(i) * N_MOD + 1, 0, 0)),
            pl.BlockSpec((d, tn), lambda i, j: (0, j)),
            pl.BlockSpec((d, LANE), lambda i, j: (0, 0)),
        ],
        out_specs=(pl.BlockSpec((tm, tn), lambda i, j: (i, j)),
                   pl.BlockSpec((tm, LANE), lambda i, j: (i, 0))),
        scratch_shapes=[pltpu.VMEM((tm, d), BF16)],
        compiler_params=_cparams(("arbitrary", "arbitrary")),
        name="in_proj",
    )(x2, norm_g.reshape(1, d), mod3, mod3, w_p, w_g)


def _outproj_kernel(x_ref, yh_ref, ym_ref, yr_ref, wh_ref, wm_ref, wr_ref, gate_ref,
                    g2_ref, shift_ref, scale_ref, o_ref, h2_ref):
    acc = _dot(yh_ref[...], wh_ref[...])
    acc = acc + _dot(ym_ref[...], wm_ref[...])
    acc = acc + _dot(yr_ref[...], wr_ref[...])
    xn = x_ref[...] + gate_ref[0] * acc
    o_ref[...] = xn
    h2_ref[...] = _norm_mod(xn, g2_ref[...], shift_ref[0], scale_ref[0]).astype(BF16)


def _out_proj(x2, y_hy, y_ml, y_rt, wo, norm2_g, mod3, group_of_tile, tm, alias):
    m, d = x2.shape
    kh, km, kr = y_hy.shape[1], y_ml.shape[1], y_rt.shape[1]
    mrow = lambda k: pl.BlockSpec((1, 1, d), lambda i: (group_of_tile(i) * N_MOD + k, 0, 0))
    return pl.pallas_call(
        _outproj_kernel,
        out_shape=(jax.ShapeDtypeStruct((m, d), F32), jax.ShapeDtypeStruct((m, d), BF16)),
        grid=(m // tm,),
        in_specs=[
            pl.BlockSpec((tm, d), lambda i: (i, 0)),
            pl.BlockSpec((tm, kh), lambda i: (i, 0)),
            pl.BlockSpec((tm, km), lambda i: (i, 0)),
            pl.BlockSpec((tm, kr), lambda i: (i, 0)),
            _resident((kh, d), lambda i: (0, 0)),
            _resident((km, d), lambda i: (0, 0)),
            _resident((kr, d), lambda i: (0, 0)),
            mrow(2),
            pl.BlockSpec((1, d), lambda i: (0, 0)),
            mrow(3), mrow(4),
        ],
        out_specs=(pl.BlockSpec((tm, d), lambda i: (i, 0)), pl.BlockSpec((tm, d), lambda i: (i, 0))),
        input_output_aliases=({0: 0} if alias else {}),
        compiler_params=_cparams(("arbitrary",)),
        name="out_proj",
    )(x2, y_hy, y_ml, y_rt, wo[:kh], wo[kh:kh + km], wo[kh + km:], mod3, norm2_g.reshape(1, d), mod3, mod3)


def _ffn_kernel(h_ref, x_ref, gate_ref, w1_ref, w2_ref, o_ref, hid_scr, *, n1):
    j = pl.program_id(1)
    tf = w1_ref.shape[1]

    @pl.when(j < n1)
    def _():
        a = jnp.maximum(_dot(h_ref[...], w1_ref[...]), 0.0)
        hid_scr[jnp.minimum(j, n1 - 1)] = (a * a).astype(BF16)

    @pl.when(j >= n1)
    def _():
        acc = _dot(hid_scr[0], w2_ref[0, 0:tf, :])
        for c in range(1, n1):
            acc = acc + _dot(hid_scr[c], w2_ref[0, c * tf:(c + 1) * tf, :])
        o_ref[...] = x_ref[...] + gate_ref[0] * acc


def _ffn(x2, h2, mod3, group_of_tile, w1, w2, tm, tf):
    m, d = x2.shape
    f = w1.shape[1]
    n1 = f // tf
    tn = MXU_W
    n2 = d // tn
    col = lambda j: jnp.maximum(j - n1, 0)
    return pl.pallas_call(
        functools.partial(_ffn_kernel, n1=n1),
        out_shape=jax.ShapeDtypeStruct((m, d), F32),
        grid=(m // tm, n1 + n2),
        in_specs=[
            pl.BlockSpec((tm, d), lambda i, j: (i, 0)),
            pl.BlockSpec((tm, tn), lambda i, j: (i, col(j))),
            pl.BlockSpec((1, 1, tn), lambda i, j: (group_of_tile(i) * N_MOD + 5, 0, col(j))),
            pl.BlockSpec((d, tf), lambda i, j: (0, jnp.minimum(j, n1 - 1))),
            pl.BlockSpec((1, f, tn), lambda i, j: (col(j), 0, 0)),
        ],
        out_specs=pl.BlockSpec((tm, tn), lambda i, j: (i, col(j))),
        scratch_shapes=[pltpu.VMEM((n1, tm, tf), BF16)],
        compiler_params=_cparams(("arbitrary", "arbitrary")),
        name="ffn",
    )(h2, x2, mod3, w1, w2)


def _final_norm_kernel(x_ref, g_ref, o_ref):
    x = x_ref[...]
    ms = jnp.mean(x * x, axis=-1, keepdims=True)
    o_ref[...] = x * lax.rsqrt(ms + EPS) * g_ref[...]


def _final_norm(x2, g, tm):
    m, d = x2.shape
    return pl.pallas_call(
        _final_norm_kernel,
        out_shape=jax.ShapeDtypeStruct((m, d), F32),
        grid=(m // tm,),
        in_specs=[pl.BlockSpec((tm, d), lambda i: (i, 0)), pl.BlockSpec((1, d), lambda i: (0, 0))],
        out_specs=pl.BlockSpec((tm, d), lambda i: (i, 0)),
        compiler_params=_cparams(("arbitrary",)),
        name="final_norm",
    )(x2, g.reshape(1, d))


def _hy_filter_kernel(z_ref, t_ref, w1_ref, b1_ref, w2_ref, b2_ref, w3f_ref, w3b_ref, fr_ref, dl_ref,
                      hf_ref, hb_ref):
    hp = lax.Precision.HIGHEST
    dot = lambda a, b: jnp.dot(a, b, precision=hp, preferred_element_type=F32)
    fr = fr_ref[...]
    hdn = jnp.sin(fr * (dot(z_ref[...], w1_ref[...]) + b1_ref[...]))
    hdn = jnp.sin(fr * (dot(hdn, w2_ref[...]) + b2_ref[...]))
    win = jnp.exp(-t_ref[...] * dl_ref[...])
    hf = dot(hdn, w3f_ref[...]) * win
    hb = dot(hdn, w3b_ref[...]) * win
    row = lax.broadcasted_iota(jnp.int32, hb.shape, 0)
    hb = jnp.where(row == 0, 0.0, hb)
    inv = 1.0 / (jnp.sum(jnp.abs(hf), axis=0, keepdims=True) + jnp.sum(jnp.abs(hb), axis=0, keepdims=True))
    hf_ref[...] = hf * inv
    hb_ref[...] = hb * inv


@functools.lru_cache(maxsize=None)
def _filter_tables(seq, hy_w):
    t = np.linspace(0.0, 1.0, seq, dtype=np.float32)[:, None]
    w = (np.float32(2.0 * math.pi / seq) * np.arange(seq, dtype=np.float32))[:, None]
    nb = (HY_EMB - 1) // 2
    bands = np.linspace(1e-4, nb - 1, nb, dtype=np.float32)[None, :]
    z = np.concatenate([t, np.cos(bands * w), -np.sin(bands * w)], axis=-1).astype(np.float32)
    z = np.pad(z, ((0, 0), (0, LANE - HY_EMB)))
    deltas = np.abs(np.linspace(math.log(HY_DECAY_TARGET) / HY_LONG_PCT,
                                math.log(HY_DECAY_TARGET) / HY_SHORT_PCT, hy_w, dtype=np.float32))[None, :]
    return z, t, deltas


def _hyena_filter(seq, w1, b1, w2, b2, w3, freq):
    hy_w = w3.shape[1] // 2
    z, t, deltas = _filter_tables(seq, hy_w)
    pad_ff = LANE - HY_FF
    w1p = jnp.pad(w1, ((0, LANE - HY_EMB), (0, pad_ff)))
    w2p = jnp.pad(w2, ((0, pad_ff), (0, pad_ff)))
    w3p = jnp.pad(w3, ((0, pad_ff), (0, 0)))
    padv = lambda v: jnp.pad(v, (0, pad_ff)).reshape(1, LANE)
    out = jax.ShapeDtypeStruct((seq, hy_w), F32)
    return pl.pallas_call(
        _hy_filter_kernel,
        out_shape=(out, out),
        compiler_params=pltpu.CompilerParams(vmem_limit_bytes=VMEM_LIMIT),
        name="hyena_filter",
    )(z, t, w1p, padv(b1), w2p, padv(b2), w3p[:, :hy_w], w3p[:, hy_w:], padv(freq), deltas)


@functools.lru_cache(maxsize=None)
def _dft_tables(seq):
    k = np.arange(seq, dtype=np.int64)
    mm = ((2 * k[:, None] + 1) * (2 * k[None, :] + 1)) % (8 * seq)
    ang = mm.astype(np.float64) * (2.0 * math.pi / (8 * seq))
    theta = (k.astype(np.float64) + 0.5) * (math.pi / (2 * seq))
    return (np.cos(ang).astype(BF16), np.sin(ang).astype(BF16),
            np.cos(theta).astype(np.float32)[:, None], np.sin(theta).astype(np.float32)[:, None])


def _hy_spec_kernel(hf_ref, hb_ref, c_ref, s_ref, ct_ref, st_ref, hre_ref, him_ref):
    cm, sm = c_ref[...], s_ref[...]
    seq = cm.shape[0]
    hf, hb = hf_ref[...], hb_ref[...]
    a, b = _dot_split(cm, hf), _dot_split(sm, hf)
    a2, b2 = _dot_split(cm, hb), _dot_split(sm, hb)
    ct, st = ct_ref[...], st_ref[...]
    scale = 1.0 / seq
    hre_ref[...] = ((a + a2) * ct + (b + b2) * st) * scale
    him_ref[...] = ((a - a2) * st - (b - b2) * ct) * scale


def _hyena_spectrum(hf, hb, cm, sm, ct, st, tc):
    seq, c = hf.shape
    out = jax.ShapeDtypeStruct((seq, c), F32)
    col = pl.BlockSpec((seq, tc), lambda j: (0, j))
    return pl.pallas_call(
        _hy_spec_kernel,
        out_shape=(out, out),
        grid=(c // tc,),
        in_specs=[col, col,
                  _resident((seq, seq), lambda j: (0, 0)), _resident((seq, seq), lambda j: (0, 0)),
                  pl.BlockSpec((seq, 1), lambda j: (0, 0)), pl.BlockSpec((seq, 1), lambda j: (0, 0))],
        out_specs=(col, col),
        compiler_params=_cparams(("arbitrary",)),
        name="hyena_spectrum",
    )(hf, hb, cm, sm, ct, st)


def _short_conv(p, w):
    n = p.shape[0]
    row = lax.broadcasted_iota(jnp.int32, p.shape, 0)
    prev = jnp.where(row == 0, 0.0, pltpu.roll(p, 1, axis=0))
    nxt = jnp.where(row == n - 1, 0.0, pltpu.roll(p, n - 1, axis=0))
    return prev * w[0:1] + p * w[1:2] + nxt * w[2:3]


def _hy_conv_kernel(p0_ref, p1_ref, pv_ref, w0_ref, w1_ref, wv_ref, hre_ref, him_ref, bias_ref,
                    c_ref, s_ref, o_ref):
    x1 = _short_conv(p1_ref[...].astype(F32), w1_ref[...])
    v = _short_conv(pv_ref[...].astype(F32), wv_ref[...])
    u = v * x1
    ub = u.astype(BF16)
    cm, sm = c_ref[...], s_ref[...]
    zc = _dot(cm, ub)
    zs = _dot(sm, ub)
    hre, him = hre_ref[...], him_ref[...]
    yre = (hre * zc + him * zs).astype(BF16)
    yim = (him * zc - hre * zs).astype(BF16)
    y = _dot(cm, yre) - _dot(sm, yim)
    x0 = _short_conv(p0_ref[...].astype(F32), w0_ref[...])
    o_ref[...] = ((y + u * bias_ref[...]) * x0).astype(BF16)


def _hyena_conv(p, seq, conv_w, hre, him, bias, cm, sm, tc):
    m = p.shape[0]
    hy_w = hre.shape[1]
    nb = hy_w // tc
    pspec = lambda off: pl.BlockSpec((seq, tc), lambda j, b, off=off: (b, off * nb + j))
    wspec = lambda off: pl.BlockSpec((3, tc), lambda j, b, off=off: (0, off * nb + j))
    col = pl.BlockSpec((seq, tc), lambda j, b: (0, j))
    return pl.pallas_call(
        _hy_conv_kernel,
        out_shape=jax.ShapeDtypeStruct((m, hy_w), BF16),
        grid=(nb, m // seq),
        in_specs=[pspec(0), pspec(1), pspec(2), wspec(0), wspec(1), wspec(2), col, col,
                  pl.BlockSpec((1, tc), lambda j, b: (0, j)),
                  _resident((seq, seq), lambda j, b: (0, 0)), _resident((seq, seq), lambda j, b: (0, 0))],
        out_specs=pl.BlockSpec((seq, tc), lambda j, b: (b, j)),
        compiler_params=_cparams(("arbitrary", "arbitrary")),
        name="hyena_conv",
    )(p, p, p, conv_w, conv_w, conv_w, hre, him, bias.reshape(1, hy_w), cm, sm)


def _tri(n, upper):
    r = lax.broadcasted_iota(jnp.int32, (n, n), 0)
    c = lax.broadcasted_iota(jnp.int32, (n, n), 1)
    return (r <= c) if upper else (r >= c)


def _head_windows(head_dim):
    assert 2 * head_dim == PAIR_W
    lane = lax.broadcasted_iota(jnp.int32, (1, WIN), 1)
    return ((slice(0, WIN), lane < head_dim, head_dim),
            (slice(LANE, LANE + WIN), lane >= WIN - head_dim, 0))


def _pair_finish(acc_f, acc_b, head_dim, mask_invalid):
    outs = []
    for j, (_, valid, _) in enumerate(_head_windows(head_dim)):
        h = acc_f[j] + acc_b[j]
        if mask_invalid:
            h = jnp.where(valid, h, 0.0)
        ms = jnp.sum(h * h, axis=-1, keepdims=True) * (1.0 / head_dim)
        outs.append(h * lax.rsqrt(ms + EPS))
    h0, h1 = outs
    return jnp.concatenate([h0[:, :LANE], h0[:, LANE:] + h1[:, :LANE], h1[:, LANE:]], axis=1)


def _scan_loops(nc_c, nc_x, run_c, run_x, unroll):
    def loop(nc, run):
        u = unroll if nc % unroll == 0 else 1

        def body(i, carry):
            for k in range(u):
                run(i * u + k, nc - 1 - (i * u + k))
            return carry

        lax.fori_loop(0, nc // u, body, 0)

    loop(nc_c, run_c)
    loop(nc_x, run_x)


def _mlstm_chunk(q, k, v, ones_lane, lf_cum_col, li_col, lf_cum_row, li_row, c_ref, m_ref, reverse):
    t = q.shape[0]
    mask = _tri(t, upper=reverse)
    end = 0 if reverse else t - 1
    b_end = lf_cum_col[end:end + 1, :]
    dm = jnp.where(mask, lf_cum_col - lf_cum_row + li_row, -jnp.inf)
    dmax = jnp.max(dm, axis=-1, keepdims=True)
    p = _dot_nt(q, k) * jnp.exp(dm - dmax)
    pv = _dot(p.astype(BF16), v)
    g = b_end - lf_cum_col + li_col
    gmax = jnp.max(g, axis=0, keepdims=True)

    m_prev = m_ref[...]
    c_prev = c_ref[...]
    m_inter = lf_cum_col + m_prev
    m_t = jnp.maximum(m_inter, dmax)
    num = jnp.exp(dmax - m_t) * pv + jnp.exp(m_inter - m_t) * _dot(q, c_prev.astype(BF16))
    den = num[:, ones_lane:ones_lane + 1]
    h = num / jnp.maximum(jnp.abs(den), jnp.exp(-m_t))
    m_new = jnp.maximum(b_end + m_prev, gmax)
    kw = (k.astype(F32) * jnp.exp(g - m_new)).astype(BF16)
    c_ref[...] = jnp.exp(b_end + m_prev - m_new) * c_prev + _dot_tn(kw, v)
    m_ref[...] = m_new
    return h


def _mlstm_kernel(qx_ref, kx_ref, vx_ref, ox_ref, gcx_ref, grx_ref,
                  qc_ref, kc_ref, vc_ref, oc_ref, gcc_ref, grc_ref,
                  wq_ref, wk_ref, gbc_ref, gbr_ref, ng_ref,
                  yx_ref, yc_ref,
                  qsx, ksx, vsx, qsc, ksc, vsc, hfx, hbx, hfc, hbc, c_scr, m_scr,
                  *, head_dim, write_ctx, unroll):
    pair = pl.program_id(1)
    t = CHUNK
    kscale = head_dim ** -0.5
    wins = _head_windows(head_dim)
    lane = lax.broadcasted_iota(jnp.int32, (1, WIN), 1)

    def prep(q_ref, k_ref, v_ref, qs, ks, vs):
        qa = _silu(_short_conv(q_ref[...].astype(F32), wq_ref[...]))
        ka = _silu(_short_conv(k_ref[...].astype(F32), wk_ref[...])) * kscale
        va = v_ref[...].astype(F32)
        for j, (win, valid, free) in enumerate(wins):
            qs[j] = jnp.where(valid, qa[:, win], 0.0).astype(BF16)
            ks[j] = jnp.where(valid, ka[:, win], 0.0).astype(BF16)
            vs[j] = jnp.where(lane == free, 1.0, jnp.where(valid, va[:, win], 0.0)).astype(BF16)

    prep(qx_ref, kx_ref, vx_ref, qsx, ksx, vsx)
    prep(qc_ref, kc_ref, vc_ref, qsc, ksc, vsc)
    c_scr[...] = jnp.zeros_like(c_scr)
    m_scr[...] = jnp.zeros_like(m_scr)

    tri_lo = _tri(t, upper=False).astype(BF16)
    tri_up = _tri(t, upper=True).astype(BF16)
    gb_col = gbc_ref[...]
    gb_row = gbr_ref[...]
    sel_c = lax.broadcasted_iota(jnp.int32, (1, LANE), 1)
    sel_r = lax.broadcasted_iota(jnp.int32, (N_GATES, 1), 0)
    pick_c = lambda a, c: jnp.sum(jnp.where(sel_c == c, a, 0.0), axis=1, keepdims=True)
    pick_r = lambda a, c: jnp.sum(jnp.where(sel_r == c, a, 0.0), axis=0, keepdims=True)

    def run(ci_f, ci_b, q_s, k_s, v_s, gcol_ref, grow_ref, hf, hb):
        for reverse, ci, hout in ((False, ci_f, hf), (True, ci_b, hb)):
            rows = pl.ds(pl.multiple_of(ci * t, t), t)
            gc = gcol_ref[rows, :] + gb_col
            gr = grow_ref[ci] + gb_row
            cum_c = _dot_split(tri_up if reverse else tri_lo, _log_sigmoid(gc))
            hi, lo = _split_bf16(_log_sigmoid(gr))
            tri_r = tri_lo if reverse else tri_up
            cum_r = _dot(hi, tri_r) + _dot(lo, tri_r)
            for j, (_, _, free) in enumerate(wins):
                head = pair * 2 + j
                icol = (2 * N_HEADS if reverse else 0) + head
                fcol = icol + N_HEADS
                chain = j * 2 + (1 if reverse else 0)
                hout[j, rows, :] = _mlstm_chunk(
                    q_s[j, rows, :], k_s[j, rows, :], v_s[j, rows, :], free,
                    pick_c(cum_c, fcol), pick_c(gc, icol), pick_r(cum_r, fcol), pick_r(gr, icol),
                    c_scr.at[chain], m_scr.at[chain], reverse)

    _scan_loops(qc_ref.shape[0] // t, qx_ref.shape[0] // t,
                lambda f, b: run(f, b, qsc, ksc, vsc, gcc_ref, grc_ref, hfc, hbc),
                lambda f, b: run(f, b, qsx, ksx, vsx, gcx_ref, grx_ref, hfx, hbx), unroll)

    def finish(hf, hb, o_ref, y_ref):
        hn = _pair_finish(hf, hb, head_dim, mask_invalid=True)
        y_ref[...] = (hn * ng_ref[...] * _sigmoid(o_ref[...].astype(F32))).astype(BF16)

    finish(hfx, hbx, ox_ref, yx_ref)
    if write_ctx:
        finish(hfc, hbc, oc_ref, yc_ref)
    else:
        yc_ref[...] = jnp.zeros_like(yc_ref)


def _mlstm(px, pc, gx, gc, seq, seq_c, col0, conv_w, gate_b, norm_g, head_dim, write_ctx):
    batch = px.shape[0] // seq
    total = N_HEADS * head_dim
    npair = total // PAIR_W
    cb = col0 // PAIR_W
    t = CHUNK

    def pspecs(s):
        return [pl.BlockSpec((s, PAIR_W), lambda b, h, k=k: (b, cb + k * npair + h)) for k in range(4)]

    def gspecs(s):
        return [pl.BlockSpec((s, LANE), lambda b, h: (b, 0)),
                pl.BlockSpec((s // t, N_GATES, t), lambda b, h: (b, 0, 0))]

    def rowform(g):
        return g[:, :N_GATES].reshape(-1, t, N_GATES).transpose(0, 2, 1)

    gb = gate_b.astype(F32)
    gb_col = jnp.pad(gb, (0, LANE - N_GATES)).reshape(1, LANE)
    gb_row = gb.reshape(-1, 1)
    scr = lambda s, dt: pltpu.VMEM((2, s, WIN), dt)
    kern = functools.partial(_mlstm_kernel, head_dim=head_dim, write_ctx=write_ctx, unroll=2)
    return pl.pallas_call(
        kern,
        out_shape=(jax.ShapeDtypeStruct((batch * seq, total), BF16),
                   jax.ShapeDtypeStruct((batch * seq_c, total), BF16)),
        grid=(batch, npair),
        in_specs=(pspecs(seq) + gspecs(seq) + pspecs(seq_c) + gspecs(seq_c) + [
            pl.BlockSpec((3, PAIR_W), lambda b, h: (0, h)),
            pl.BlockSpec((3, PAIR_W), lambda b, h: (0, npair + h)),
            pl.BlockSpec((1, LANE), lambda b, h: (0, 0)),
            pl.BlockSpec((N_GATES, 1), lambda b, h: (0, 0)),
            pl.BlockSpec((1, PAIR_W), lambda b, h: (0, h)),
        ]),
        out_specs=(pl.BlockSpec((seq, PAIR_W), lambda b, h: (b, h)),
                   pl.BlockSpec((seq_c, PAIR_W), lambda b, h: (b, h))),
        scratch_shapes=[scr(seq, BF16), scr(seq, BF16), scr(seq, BF16),
                        scr(seq_c, BF16), scr(seq_c, BF16), scr(seq_c, BF16),
                        scr(seq, F32), scr(seq, F32), scr(seq_c, F32), scr(seq_c, F32),
                        pltpu.VMEM((4, WIN, WIN), F32),
                        pltpu.VMEM((4, 1, 1), F32)],
        compiler_params=_cparams(("arbitrary", "arbitrary")),
        name="mlstm",
    )(px, px, px, px, gx, rowform(gx), pc, pc, pc, pc, gc, rowform(gc),
      conv_w, conv_w, gb_col, gb_row, norm_g.reshape(1, total))


def _ret_chunk(q, k, v, dmask, q_decay, k_decay, c_decay, s_ref):
    pv = _dot((_dot_nt(q, k) * dmask).astype(BF16), v)
    kd = (k.astype(F32) * k_decay).astype(BF16)
    s_prev = s_ref[...]
    s_ref[...] = c_decay * s_prev + _dot_tn(kd, v)
    return pv + q_decay * _dot(q, s_prev.astype(BF16))


def _ret_kernel(ld_ref, qx_ref, kx_ref, vx_ref, gx_ref, qc_ref, kc_ref, vc_ref, gc_ref, cos_ref, sgn_ref,
                yx_ref, yc_ref,
                qsx, ksx, vsx, qsc, ksc, vsc, ofx, obx, ofc, obc, s_scr,
                *, key_dim, val_dim, write_ctx, unroll):
    pair = pl.program_id(1)
    t = CHUNK
    qscale = key_dim ** -0.5
    half = key_dim // 2
    vwins = _head_windows(val_dim)
    lane = lax.broadcasted_iota(jnp.int32, (1, WIN), 1)
    klane = lax.broadcasted_iota(jnp.int32, (1, qx_ref.shape[1]), 1)
    first_half = (klane % key_dim) < half

    def rope(a):
        width = a.shape[1]
        partner = jnp.where(first_half, pltpu.roll(a, width - half, axis=1), pltpu.roll(a, half, axis=1))
        return a * cos_ref[...] + partner * sgn_ref[...]

    def prep(qa, ka, v_ref, qs, ks, vs):
        qw = jnp.where(pair == 0, qa[:, :WIN], qa[:, LANE:LANE + WIN])
        kw = jnp.where(pair == 0, ka[:, :WIN], ka[:, LANE:LANE + WIN])
        va = v_ref[...].astype(F32)
        for j, (win, valid, _) in enumerate(vwins):
            start = (2 * pair + j) * key_dim - LANE * pair
            kvalid = jnp.logical_and(lane >= start, lane < start + key_dim)
            qs[j] = jnp.where(kvalid, qw, 0.0).astype(BF16)
            ks[j] = jnp.where(kvalid, kw, 0.0).astype(BF16)
            vs[j] = jnp.where(valid, va[:, win], 0.0).astype(BF16)

    prep(rope(qx_ref[...].astype(F32)) * qscale, rope(kx_ref[...].astype(F32)), vx_ref, qsx, ksx, vsx)
    prep(qc_ref[...].astype(F32) * qscale, kc_ref[...].astype(F32), vc_ref, qsc, ksc, vsc)
    s_scr[...] = jnp.zeros_like(s_scr)

    r = lax.broadcasted_iota(jnp.int32, (t, t), 0)
    c = lax.broadcasted_iota(jnp.int32, (t, t), 1)
    idx = lax.broadcasted_iota(jnp.int32, (t, 1), 0).astype(F32)
    consts = []
    for j in range(2):
        for reverse in (False, True):
            lg = -jnp.exp(jnp.full((1, 1), ld_ref[1 if reverse else 0, pair * 2 + j], F32))
            rel = (c - r) if reverse else (r - c)
            dmask = jnp.where(rel >= 0, jnp.exp(lg * jnp.maximum(rel, 0).astype(F32)), 0.0)
            pos = (t - 1.0 - idx) if reverse else idx
            consts.append((dmask, jnp.exp(lg * (pos + 1.0)), jnp.exp(lg * (t - 1.0 - pos)), jnp.exp(lg * t)))

    def run(ci_f, ci_b, q_s, k_s, v_s, of, ob):
        for reverse, ci, oout in ((False, ci_f, of), (True, ci_b, ob)):
            rows = pl.ds(pl.multiple_of(ci * t, t), t)
            for j in range(2):
                chain = j * 2 + (1 if reverse else 0)
                dmask, q_decay, k_decay, c_decay = consts[chain]
                oout[j, rows, :] = _ret_chunk(q_s[j, rows, :], k_s[j, rows, :], v_s[j, rows, :],
                                              dmask, q_decay, k_decay, c_decay, s_scr.at[chain])

    _scan_loops(qc_ref.shape[0] // t, qx_ref.shape[0] // t,
                lambda f, b: run(f, b, qsc, ksc, vsc, ofc, obc),
                lambda f, b: run(f, b, qsx, ksx, vsx, ofx, obx), unroll)

    def finish(of, ob, g_ref, y_ref):
        on = _pair_finish(of, ob, val_dim, mask_invalid=False)
        y_ref[...] = (on * _silu(g_ref[...].astype(F32))).astype(BF16)

    finish(ofx, obx, gx_ref, yx_ref)
    if write_ctx:
        finish(ofc, obc, gc_ref, yc_ref)
    else:
        yc_ref[...] = jnp.zeros_like(yc_ref)


def _retention(px, pc, seq, seq_c, col0, log_decay, cosf, sgnf, key_dim, val_dim, write_ctx):
    batch = px.shape[0] // seq
    ktot, vtot = N_HEADS * key_dim, N_HEADS * val_dim
    assert ktot == PAIR_W
    npair = vtot // PAIR_W
    qb = col0 // PAIR_W
    vb = (col0 + 2 * ktot) // PAIR_W

    def pspecs(s):
        return [pl.BlockSpec((s, PAIR_W), lambda b, h: (b, qb)),
                pl.BlockSpec((s, PAIR_W), lambda b, h: (b, qb + 1)),
                pl.BlockSpec((s, PAIR_W), lambda b, h: (b, vb + h)),
                pl.BlockSpec((s, PAIR_W), lambda b, h: (b, vb + npair + h))]

    scr = lambda s, dt: pltpu.VMEM((2, s, WIN), dt)
    kern = functools.partial(_ret_kernel, key_dim=key_dim, val_dim=val_dim, write_ctx=write_ctx, unroll=2)
    return pl.pallas_call(
        kern,
        out_shape=(jax.ShapeDtypeStruct((batch * seq, vtot), BF16),
                   jax.ShapeDtypeStruct((batch * seq_c, vtot), BF16)),
        grid=(batch, npair),
        in_specs=([pl.BlockSpec(memory_space=pltpu.SMEM)] + pspecs(seq) + pspecs(seq_c) + [
            _resident((seq, PAIR_W), lambda b, h: (0, 0)),
            _resident((seq, PAIR_W), lambda b, h: (0, 0)),
        ]),
        out_specs=(pl.BlockSpec((seq, PAIR_W), lambda b, h: (b, h)),
                   pl.BlockSpec((seq_c, PAIR_W), lambda b, h: (b, h))),
        scratch_shapes=[scr(seq, BF16), scr(seq, BF16), scr(seq, BF16),
                        scr(seq_c, BF16), scr(seq_c, BF16), scr(seq_c, BF16),
                        scr(seq, F32), scr(seq, F32), scr(seq_c, F32), scr(seq_c, F32),
                        pltpu.VMEM((4, WIN, WIN), F32)],
        compiler_params=_cparams(("arbitrary", "arbitrary")),
        name="retention",
    )(log_decay.astype(F32), px, px, px, px, pc, pc, pc, pc, cosf, sgnf)


@functools.lru_cache(maxsize=None)
def _rope_tables(seq, key_dim):
    rows = seq // GRID_W
    r = np.repeat(np.arange(rows, dtype=np.float32), GRID_W)
    col = np.tile(np.arange(GRID_W, dtype=np.float32), rows)
    nf = key_dim // 4
    inv = (np.float32(ROPE_BASE) ** (-np.arange(nf, dtype=np.float32) / np.float32(nf))).astype(np.float32)
    ang = np.concatenate([r[:, None] * inv, col[:, None] * inv], axis=-1).astype(np.float32)
    cos, sin = np.cos(ang).astype(np.float32), np.sin(ang).astype(np.float32)
    return np.tile(np.concatenate([cos, cos], axis=1), (1, N_HEADS)), np.tile(np.concatenate([-sin, sin], axis=1), (1, N_HEADS))


def kernel(x, c, ctx, c_ctx, norm1_g, norm2_g, w_mod, b_mod, w_in, hy_conv_w, hy_f_w1, hy_f_b1, hy_f_w2, hy_f_b2, hy_f_w3, hy_f_freq, hy_bias, ml_conv_w, ml_gate_b, ml_norm_g, rt_log_decay, w_out, w_ff1, w_ff2, final_g):
    batch, seq, d = x.shape
    seq_c = ctx.shape[1]
    depth = w_mod.shape[0]
    hy_w = d // 4
    ml_w = 3 * d // 8
    rt_w = 3 * d // 8
    ml_dh = ml_w // N_HEADS
    rt_dv = rt_w // N_HEADS
    rt_dk = rt_dv // 2
    hy_cols, ml_cols = 3 * hy_w, 4 * ml_w
    n_in = w_in.shape[2]
    gate0 = hy_cols + ml_cols
    rt0 = gate0 + N_GATES
    n_p = -(-n_in // 1024) * 1024

    s_in = jnp.concatenate([c, c_ctx[None], jnp.zeros((MOD_ROWS - batch - 1, d), F32)], axis=0)
    mod = _modulation(s_in, w_mod, b_mod)

    grp_x = lambda tm: (lambda i: (i * tm) // seq)
    grp_c = lambda tm: (lambda i: batch + 0 * i)
    tm_x, tm_c = _pick_tile(seq, 1024), _pick_tile(seq_c, 1024)
    tm_ox, tm_oc = _pick_tile(seq, 512), _pick_tile(seq_c, 512)

    cm_x, sm_x, ct_x, st_x = _dft_tables(seq)
    cm_c, sm_c, ct_c, st_c = _dft_tables(seq_c)
    cosf, sgnf = _rope_tables(seq, rt_dk)
    tc = _pick_tile(hy_w, MXU_W)
    tc_spec = _pick_tile(hy_w, LANE)

    xs = x.reshape(batch * seq, d)
    hc = ctx.reshape(batch * seq_c, d)
    for l in range(depth):
        need_ctx = l < depth - 1
        mod3 = mod[l].reshape(MOD_ROWS * N_MOD, 1, d)
        wl = w_in[l]
        w_p = jnp.concatenate([wl[:, :gate0], wl[:, rt0:], wl[:, gate0:rt0],
                               jnp.zeros((d, n_p - n_in), F32)], axis=1).astype(BF16)
        w_g = jnp.pad(wl[:, gate0:rt0], ((0, 0), (0, LANE - N_GATES))).astype(BF16)
        ml_col0, rt_col0 = hy_cols, gate0
        wo = w_out[l].astype(BF16)
        w1 = w_ff1[l].astype(BF16)
        w2 = w_ff2[l].astype(BF16).reshape(-1, d // MXU_W, MXU_W).transpose(1, 0, 2)

        px, gx = _in_proj(xs, norm1_g[l], mod3, grp_x(tm_x), w_p, w_g, tm_x)
        pc, gc = _in_proj(hc, norm1_g[l], mod3, grp_c(tm_c), w_p, w_g, tm_c)

        filt = (hy_f_w1[l], hy_f_b1[l], hy_f_w2[l], hy_f_b2[l], hy_f_w3[l], hy_f_freq[l])
        hf, hb = _hyena_filter(seq, *filt)
        hre, him = _hyena_spectrum(hf, hb, cm_x, sm_x, ct_x, st_x, tc_spec)
        y_hy = _hyena_conv(px, seq, hy_conv_w[l], hre, him, hy_bias[l], cm_x, sm_x, tc)
        y_ml, yc_ml = _mlstm(px, pc, gx, gc, seq, seq_c, ml_col0, ml_conv_w[l], ml_gate_b[l], ml_norm_g[l],
                             ml_dh, need_ctx)
        y_rt, yc_rt = _retention(px, pc, seq, seq_c, rt_col0, rt_log_decay[l], cosf, sgnf,
                                 rt_dk, rt_dv, need_ctx)
        xs, h2 = _out_proj(xs, y_hy, y_ml, y_rt, wo, norm2_g[l], mod3, grp_x(tm_ox), tm_ox, alias=l > 0)
        if need_ctx:
            hfc, hbc = _hyena_filter(seq_c, *filt)
            hre_c, him_c = _hyena_spectrum(hfc, hbc, cm_c, sm_c, ct_c, st_c, tc_spec)
            yc_hy = _hyena_conv(pc, seq_c, hy_conv_w[l], hre_c, him_c, hy_bias[l], cm_c, sm_c, tc)
            hc, h2c = _out_proj(hc, yc_hy, yc_ml, yc_rt, wo, norm2_g[l], mod3, grp_c(tm_oc), tm_oc, alias=l > 0)
            hc = _ffn(hc, h2c, mod3, grp_c(tm_c), w1, w2, tm_c, 1024)
        xs = _ffn(xs, h2, mod3, grp_x(tm_x), w1, w2, tm_x, 1024)
    return _final_norm(xs, final_g, tm_ox).reshape(batch, seq, d)
```

```python
import functools
import math

import jax
import jax.numpy as jnp
import numpy as np
from jax import lax
from jax.experimental import pallas as pl
from jax.experimental.pallas import tpu as pltpu

F32 = jnp.float32
BF16 = jnp.bfloat16

GRID_W = 64
HY_EMB = 33
HY_FF = 64
HY_DECAY_TARGET = 1e-2
HY_SHORT_PCT = 0.3
HY_LONG_PCT = 1.5
N_HEADS = 4
N_GATES = 4 * N_HEADS
CHUNK = 128
ROPE_BASE = 10000.0
N_MOD = 6
EPS = 1e-6

LANE = 128
MXU_W = 256
WIN = 2 * LANE
PAIR_W = 3 * LANE
MOD_ROWS = 16
VMEM_LIMIT = 56 * 1024 * 1024


def _cparams(sem):
    return pltpu.CompilerParams(dimension_semantics=sem, vmem_limit_bytes=VMEM_LIMIT)


def _sigmoid(x):
    return 0.5 * jnp.tanh(0.5 * x) + 0.5


def _silu(x):
    return x * _sigmoid(x)


def _log_sigmoid(x):
    return jnp.minimum(x, 0.0) - jnp.log(1.0 + jnp.exp(-jnp.abs(x)))


def _dot(a, b):
    return jnp.dot(a, b, preferred_element_type=F32)


def _dot_nt(a, b):
    return lax.dot_general(a, b, (((1,), (1,)), ((), ())), preferred_element_type=F32)


def _dot_tn(a, b):
    return lax.dot_general(a, b, (((0,), (0,)), ((), ())), preferred_element_type=F32)


def _split_bf16(a):
    hi = a.astype(BF16)
    lo = (a - hi.astype(F32)).astype(BF16)
    return hi, lo


def _dot_split(a, b_f32):
    hi, lo = _split_bf16(b_f32)
    return _dot(a, hi) + _dot(a, lo)


def _resident(shape, index_map):
    return pl.BlockSpec(shape, index_map, pipeline_mode=pl.Buffered(1))


def _pick_tile(n, target):
    t = min(n, target)
    while n % t:
        t //= 2
    return t


def _norm_mod(x, g, shift, scale):
    ms = jnp.mean(x * x, axis=-1, keepdims=True)
    y = x * lax.rsqrt(ms + EPS) * g
    return y * (1.0 + scale) + shift


def _mod_kernel(s_ref, w_ref, b_ref, o_ref):
    s = _silu(s_ref[...]).astype(BF16)
    w = w_ref[0].astype(BF16)
    o_ref[0] = _dot(s, w) + b_ref[0]


def _modulation(s_in, w_mod, b_mod):
    depth, d, n = w_mod.shape
    tn = _pick_tile(n, 1024)
    return pl.pallas_call(
        _mod_kernel,
        out_shape=jax.ShapeDtypeStruct((depth, MOD_ROWS, n), F32),
        grid=(depth, n // tn),
        in_specs=[
            pl.BlockSpec((MOD_ROWS, d), lambda l, j: (0, 0)),
            pl.BlockSpec((1, d, tn), lambda l, j: (l, 0, j)),
            pl.BlockSpec((1, 1, tn), lambda l, j: (l, 0, j)),
        ],
        out_specs=pl.BlockSpec((1, MOD_ROWS, tn), lambda l, j: (l, 0, j)),
        compiler_params=_cparams(("arbitrary", "arbitrary")),
        name="modulation",
    )(s_in, w_mod, b_mod.reshape(depth, 1, n))


def _inproj_kernel(x_ref, g_ref, shift_ref, scale_ref, w_ref, wg_ref, p_ref, gate_ref, h_scr):
    @pl.when(pl.program_id(1) == 0)
    def _():
        h = _norm_mod(x_ref[...], g_ref[...], shift_ref[0], scale_ref[0]).astype(BF16)
        h_scr[...] = h
        gate_ref[...] = _dot(h, wg_ref[...])

    p_ref[...] = _dot(h_scr[...], w_ref[...]).astype(BF16)


def _in_proj(x2, norm_g, mod3, group_of_tile, w_p, w_g, tm):
    m, d = x2.shape
    n = w_p.shape[1]
    tn = _pick_tile(n, 1024)
    return pl.pallas_call(
        _inproj_kernel,
        out_shape=(jax.ShapeDtypeStruct((m, n), BF16), jax.ShapeDtypeStruct((m, LANE), F32)),
        grid=(m // tm, n // tn),
        in_specs=[
            pl.BlockSpec((tm, d), lambda i, j: (i, 0)),
            pl.BlockSpec((1, d), lambda i, j: (0, 0)),
            pl.BlockSpec((1, 1, d), lambda i, j: (group_of_tile(i) * N_MOD + 0, 0, 0)),
            pl.BlockSpec((1, 1, d), lambda i, j: (group_of_tile(i) * N_MOD + 1, 0, 0)),
            pl.BlockSpec((d, tn), lambda i, j: (0, j)),
            pl.BlockSpec((d, LANE), lambda i, j: (0, 0)),
        ],
        out_specs=(pl.BlockSpec((tm, tn), lambda i, j: (i, j)),
                   pl.BlockSpec((tm, LANE), lambda i, j: (i, 0))),
        scratch_shapes=[pltpu.VMEM((tm, d), BF16)],
        compiler_params=_cparams(("arbitrary", "arbitrary")),
        name="in_proj",
    )(x2, norm_g.reshape(1, d), mod3, mod3, w_p, w_g)


def _outproj_kernel(x_ref, yh_ref, ym_ref, yr_ref, w_ref, gate_ref,
                    g2_ref, shift_ref, scale_ref, o_ref, h2_ref):
    kh, km = yh_ref.shape[1], ym_ref.shape[1]
    acc = _dot(yh_ref[...], w_ref[0:kh, :])
    acc = acc + _dot(ym_ref[...], w_ref[kh:kh + km, :])
    acc = acc + _dot(yr_ref[...], w_ref[kh + km:, :])
    xn = x_ref[...] + gate_ref[0] * acc
    o_ref[...] = xn
    h2_ref[...] = _norm_mod(xn, g2_ref[...], shift_ref[0], scale_ref[0]).astype(BF16)


def _out_proj(x2, y_hy, y_ml, y_rt, wo, norm2_g, mod3, group_of_tile, tm, alias):
    m, d = x2.shape
    kh, km, kr = y_hy.shape[1], y_ml.shape[1], y_rt.shape[1]
    mrow = lambda k: pl.BlockSpec((1, 1, d), lambda i: (group_of_tile(i) * N_MOD + k, 0, 0))
    return pl.pallas_call(
        _outproj_kernel,
        out_shape=(jax.ShapeDtypeStruct((m, d), F32), jax.ShapeDtypeStruct((m, d), BF16)),
        grid=(m // tm,),
        in_specs=[
            pl.BlockSpec((tm, d), lambda i: (i, 0)),
            pl.BlockSpec((tm, kh), lambda i: (i, 0)),
            pl.BlockSpec((tm, km), lambda i: (i, 0)),
            pl.BlockSpec((tm, kr), lambda i: (i, 0)),
            _resident((kh + km + kr, d), lambda i: (0, 0)),
            mrow(2),
            pl.BlockSpec((1, d), lambda i: (0, 0)),
            mrow(3), mrow(4),
        ],
        out_specs=(pl.BlockSpec((tm, d), lambda i: (i, 0)), pl.BlockSpec((tm, d), lambda i: (i, 0))),
        input_output_aliases=({0: 0} if alias else {}),
        compiler_params=_cparams(("arbitrary",)),
        name="out_proj",
    )(x2, y_hy, y_ml, y_rt, wo, mod3, norm2_g.reshape(1, d), mod3, mod3)


def _ffn_kernel(h_ref, x_ref, gate_ref, w1_ref, w2_ref, o_ref, hid_scr, *, n1):
    j = pl.program_id(1)
    tf = w1_ref.shape[1]

    @pl.when(j < n1)
    def _():
        a = jnp.maximum(_dot(h_ref[...], w1_ref[...]), 0.0)
        hid_scr[jnp.minimum(j, n1 - 1)] = (a * a).astype(BF16)

    @pl.when(j >= n1)
    def _():
        acc = _dot(hid_scr[0], w2_ref[0:tf, :])
        for c in range(1, n1):
            acc = acc + _dot(hid_scr[c], w2_ref[c * tf:(c + 1) * tf, :])
        o_ref[...] = x_ref[...] + gate_ref[0] * acc


def _ffn(x2, h2, mod3, group_of_tile, w1, w2, tm, tf):
    m, d = x2.shape
    f = w1.shape[1]
    n1 = f // tf
    tn = MXU_W
    n2 = d // tn
    col = lambda j: jnp.maximum(j - n1, 0)
    return pl.pallas_call(
        functools.partial(_ffn_kernel, n1=n1),
        out_shape=jax.ShapeDtypeStruct((m, d), F32),
        grid=(m // tm, n1 + n2),
        in_specs=[
            pl.BlockSpec((tm, d), lambda i, j: (i, 0)),
            pl.BlockSpec((tm, tn), lambda i, j: (i, col(j))),
            pl.BlockSpec((1, 1, tn), lambda i, j: (group_of_tile(i) * N_MOD + 5, 0, col(j))),
            pl.BlockSpec((d, tf), lambda i, j: (0, jnp.minimum(j, n1 - 1))),
            pl.BlockSpec((f, tn), lambda i, j: (0, col(j))),
        ],
        out_specs=pl.BlockSpec((tm, tn), lambda i, j: (i, col(j))),
        scratch_shapes=[pltpu.VMEM((n1, tm, tf), BF16)],
        compiler_params=_cparams(("arbitrary", "arbitrary")),
        name="ffn",
    )(h2, x2, mod3, w1, w2)


def _final_norm_kernel(x_ref, g_ref, o_ref):
    x = x_ref[...]
    ms = jnp.mean(x * x, axis=-1, keepdims=True)
    o_ref[...] = x * lax.rsqrt(ms + EPS) * g_ref[...]


def _final_norm(x2, g, tm):
    m, d = x2.shape
    return pl.pallas_call(
        _final_norm_kernel,
        out_shape=jax.ShapeDtypeStruct((m, d), F32),
        grid=(m // tm,),
        in_specs=[pl.BlockSpec((tm, d), lambda i: (i, 0)), pl.BlockSpec((1, d), lambda i: (0, 0))],
        out_specs=pl.BlockSpec((tm, d), lambda i: (i, 0)),
        compiler_params=_cparams(("arbitrary",)),
        name="final_norm",
    )(x2, g.reshape(1, d))


def _hy_filter_kernel(z_ref, t_ref, w1_ref, b1_ref, w2_ref, b2_ref, w3f_ref, w3b_ref, fr_ref, dl_ref,
                      hf_ref, hb_ref):
    hp = lax.Precision.HIGHEST
    dot = lambda a, b: jnp.dot(a, b, precision=hp, preferred_element_type=F32)
    fr = fr_ref[...]
    hdn = jnp.sin(fr * (dot(z_ref[...], w1_ref[...]) + b1_ref[...]))
    hdn = jnp.sin(fr * (dot(hdn, w2_ref[...]) + b2_ref[...]))
    win = jnp.exp(-t_ref[...] * dl_ref[...])
    hf = dot(hdn, w3f_ref[...]) * win
    hb = dot(hdn, w3b_ref[...]) * win
    row = lax.broadcasted_iota(jnp.int32, hb.shape, 0)
    hb = jnp.where(row == 0, 0.0, hb)
    inv = 1.0 / (jnp.sum(jnp.abs(hf), axis=0, keepdims=True) + jnp.sum(jnp.abs(hb), axis=0, keepdims=True))
    hf_ref[...] = hf * inv
    hb_ref[...] = hb * inv


@functools.lru_cache(maxsize=None)
def _filter_tables(seq, hy_w):
    t = np.linspace(0.0, 1.0, seq, dtype=np.float32)[:, None]
    w = (np.float32(2.0 * math.pi / seq) * np.arange(seq, dtype=np.float32))[:, None]
    nb = (HY_EMB - 1) // 2
    bands = np.linspace(1e-4, nb - 1, nb, dtype=np.float32)[None, :]
    z = np.concatenate([t, np.cos(bands * w), -np.sin(bands * w)], axis=-1).astype(np.float32)
    z = np.pad(z, ((0, 0), (0, LANE - HY_EMB)))
    deltas = np.abs(np.linspace(math.log(HY_DECAY_TARGET) / HY_LONG_PCT,
                                math.log(HY_DECAY_TARGET) / HY_SHORT_PCT, hy_w, dtype=np.float32))[None, :]
    return z, t, deltas


def _hyena_filter(seq, w1, b1, w2, b2, w3, freq):
    hy_w = w3.shape[1] // 2
    z, t, deltas = _filter_tables(seq, hy_w)
    pad_ff = LANE - HY_FF
    w1p = jnp.pad(w1, ((0, LANE - HY_EMB), (0, pad_ff)))
    w2p = jnp.pad(w2, ((0, pad_ff), (0, pad_ff)))
    w3p = jnp.pad(w3, ((0, pad_ff), (0, 0)))
    padv = lambda v: jnp.pad(v, (0, pad_ff)).reshape(1, LANE)
    out = jax.ShapeDtypeStruct((seq, hy_w), F32)
    return pl.pallas_call(
        _hy_filter_kernel,
        out_shape=(out, out),
        compiler_params=pltpu.CompilerParams(vmem_limit_bytes=VMEM_LIMIT),
        name="hyena_filter",
    )(z, t, w1p, padv(b1), w2p, padv(b2), w3p[:, :hy_w], w3p[:, hy_w:], padv(freq), deltas)


@functools.lru_cache(maxsize=None)
def _dft_tables(seq):
    k = np.arange(seq, dtype=np.int64)
    mm = ((2 * k[:, None] + 1) * (2 * k[None, :] + 1)) % (8 * seq)
    ang = mm.astype(np.float64) * (2.0 * math.pi / (8 * seq))
    theta = (k.astype(np.float64) + 0.5) * (math.pi / (2 * seq))
    return (np.cos(ang).astype(np.float32), np.sin(ang).astype(np.float32),
            np.cos(theta).astype(np.float32)[:, None], np.sin(theta).astype(np.float32)[:, None])


def _hy_spec_kernel(hf_ref, hb_ref, c_ref, s_ref, ct_ref, st_ref, hre_ref, him_ref):
    cm, sm = c_ref[...], s_ref[...]
    seq = cm.shape[0]
    hf, hb = hf_ref[...], hb_ref[...]
    a, b = _dot_split(cm, hf), _dot_split(sm, hf)
    a2, b2 = _dot_split(cm, hb), _dot_split(sm, hb)
    ct, st = ct_ref[...], st_ref[...]
    scale = 1.0 / seq
    hre_ref[...] = ((a + a2) * ct + (b + b2) * st) * scale
    him_ref[...] = ((a - a2) * st - (b - b2) * ct) * scale


def _hyena_spectrum(hf, hb, cm, sm, ct, st, tc):
    seq, c = hf.shape
    out = jax.ShapeDtypeStruct((seq, c), F32)
    col = pl.BlockSpec((seq, tc), lambda j: (0, j))
    return pl.pallas_call(
        _hy_spec_kernel,
        out_shape=(out, out),
        grid=(c // tc,),
        in_specs=[col, col,
                  _resident((seq, seq), lambda j: (0, 0)), _resident((seq, seq), lambda j: (0, 0)),
                  pl.BlockSpec((seq, 1), lambda j: (0, 0)), pl.BlockSpec((seq, 1), lambda j: (0, 0))],
        out_specs=(col, col),
        compiler_params=_cparams(("arbitrary",)),
        name="hyena_spectrum",
    )(hf, hb, cm, sm, ct, st)


def _short_conv(p, w):
    n = p.shape[0]
    pz = jnp.concatenate([p, jnp.zeros((8, p.shape[1]), p.dtype)], axis=0)
    prev = pltpu.roll(pz, 1, axis=0)[:n]
    nxt = pltpu.roll(pz, n + 7, axis=0)[:n]
    return prev * w[0:1] + p * w[1:2] + nxt * w[2:3]


def _hy_conv_kernel(p0_ref, p1_ref, pv_ref, w0_ref, w1_ref, wv_ref, hre_ref, him_ref, bias_ref,
                    c_ref, s_ref, o_ref):
    x1 = _short_conv(p1_ref[...].astype(F32), w1_ref[...])
    v = _short_conv(pv_ref[...].astype(F32), wv_ref[...])
    u = v * x1
    ub = u.astype(BF16)
    cm, sm = c_ref[...], s_ref[...]
    zc = _dot(cm, ub)
    zs = _dot(sm, ub)
    hre, him = hre_ref[...], him_ref[...]
    yre = (hre * zc + him * zs).astype(BF16)
    yim = (him * zc - hre * zs).astype(BF16)
    y = _dot(cm, yre) - _dot(sm, yim)
    x0 = _short_conv(p0_ref[...].astype(F32), w0_ref[...])
    o_ref[...] = ((y + u * bias_ref[...]) * x0).astype(BF16)


def _hyena_conv(p, seq, conv_w, hre, him, bias, cm, sm, tc):
    m = p.shape[0]
    hy_w = hre.shape[1]
    nb = hy_w // tc
    pspec = lambda off: pl.BlockSpec((seq, tc), lambda j, b, off=off: (b, off * nb + j))
    wspec = lambda off: pl.BlockSpec((3, tc), lambda j, b, off=off: (0, off * nb + j))
    col = pl.BlockSpec((seq, tc), lambda j, b: (0, j))
    return pl.pallas_call(
        _hy_conv_kernel,
        out_shape=jax.ShapeDtypeStruct((m, hy_w), BF16),
        grid=(nb, m // seq),
        in_specs=[pspec(0), pspec(1), pspec(2), wspec(0), wspec(1), wspec(2), col, col,
                  pl.BlockSpec((1, tc), lambda j, b: (0, j)),
                  _resident((seq, seq), lambda j, b: (0, 0)), _resident((seq, seq), lambda j, b: (0, 0))],
        out_specs=pl.BlockSpec((seq, tc), lambda j, b: (b, j)),
        compiler_params=_cparams(("arbitrary", "arbitrary")),
        name="hyena_conv",
    )(p, p, p, conv_w, conv_w, conv_w, hre, him, bias.reshape(1, hy_w), cm, sm)


def _tri(n, upper):
    r = lax.broadcasted_iota(jnp.int32, (n, n), 0)
    c = lax.broadcasted_iota(jnp.int32, (n, n), 1)
    return (r <= c) if upper else (r >= c)


def _head_windows(head_dim):
    assert 2 * head_dim == PAIR_W
    lane = lax.broadcasted_iota(jnp.int32, (1, WIN), 1)
    return ((slice(0, WIN), lane < head_dim, head_dim),
            (slice(LANE, LANE + WIN), lane >= WIN - head_dim, 0))


def _pair_finish(acc_f, acc_b, head_dim, mask_invalid):
    outs = []
    for j, (_, valid, _) in enumerate(_head_windows(head_dim)):
        h = acc_f[j] + acc_b[j]
        if mask_invalid:
            h = jnp.where(valid, h, 0.0)
        ms = jnp.sum(h * h, axis=-1, keepdims=True) * (1.0 / head_dim)
        outs.append(h * lax.rsqrt(ms + EPS))
    h0, h1 = outs
    return jnp.concatenate([h0[:, :LANE], h0[:, LANE:] + h1[:, :LANE], h1[:, LANE:]], axis=1)


def _scan_loops(nc_c, nc_x, run_c, run_x, unroll):
    def loop(nc, run):
        u = unroll if nc % unroll == 0 else 1

        def body(i, carry):
            for k in range(u):
                run(i * u + k, nc - 1 - (i * u + k))
            return carry

        lax.fori_loop(0, nc // u, body, 0)

    loop(nc_c, run_c)
    loop(nc_x, run_x)


@functools.lru_cache(maxsize=None)
def _gate_select_matrix():
    e = np.zeros((2 * LANE, 2 * LANE), np.float32)
    for pair in range(N_HEADS // 2):
        for j in range(2):
            for direction in range(2):
                icol = 2 * N_HEADS * direction + 2 * pair + j
                out = LANE * pair + 2 * j + direction
                e[icol, out] = 1.0
                e[LANE + icol + N_HEADS, out] = -1.0
    return e.astype(BF16)


def _ml_gate_kernel(gc_ref, gr_ref, gbc_ref, gbr_ref, esel_ref, rows_ref, cc_ref):
    nc, _, tt = gr_ref.shape
    tri_lo = _tri(tt, upper=False).astype(BF16)
    tri_up = _tri(tt, upper=True).astype(BF16)
    lane = lax.broadcasted_iota(jnp.int32, (1, LANE), 1)
    row = lax.broadcasted_iota(jnp.int32, (N_GATES, 1), 0)
    esel = esel_ref[...]
    for ci in range(nc):
        rs = slice(ci * tt, (ci + 1) * tt)
        gc = gc_ref[rs, :] + gbc_ref[...]
        gr = gr_ref[ci] + gbr_ref[...]
        ls = _log_sigmoid(gc)
        cum_c = jnp.where(lane >= 2 * N_HEADS, _dot_split(tri_up, ls), _dot_split(tri_lo, ls))
        hi, lo = _split_bf16(jnp.concatenate([gc, cum_c], axis=1))
        cc_ref[rs, :] = _dot(hi, esel) + _dot(lo, esel)
        hi, lo = _split_bf16(_log_sigmoid(gr))
        prefix = _dot(hi, tri_up) + _dot(lo, tri_up)
        suffix = _dot(hi, tri_lo) + _dot(lo, tri_lo)
        is_forget = (row % (2 * N_HEADS)) >= N_HEADS
        rows_ref[ci] = jnp.where(is_forget, jnp.where(row >= 2 * N_HEADS, suffix, prefix), gr)


def _mlstm_gates(g, seq, tt, gate_b):
    batch = g.shape[0] // seq
    nc = seq // tt
    grow = g[:, :N_GATES].reshape(-1, tt, N_GATES).transpose(0, 2, 1)
    gb = gate_b.astype(F32)
    return pl.pallas_call(
        _ml_gate_kernel,
        out_shape=(jax.ShapeDtypeStruct((batch * nc, N_GATES, tt), F32),
                   jax.ShapeDtypeStruct((batch * seq, 2 * LANE), F32)),
        grid=(batch,),
        in_specs=[pl.BlockSpec((seq, LANE), lambda b: (b, 0)),
                  pl.BlockSpec((nc, N_GATES, tt), lambda b: (b, 0, 0)),
                  pl.BlockSpec((1, LANE), lambda b: (0, 0)),
                  pl.BlockSpec((N_GATES, 1), lambda b: (0, 0)),
                  pl.BlockSpec((2 * LANE, 2 * LANE), lambda b: (0, 0))],
        out_specs=(pl.BlockSpec((nc, N_GATES, tt), lambda b: (b, 0, 0)),
                   pl.BlockSpec((seq, 2 * LANE), lambda b: (b, 0))),
        compiler_params=_cparams(("arbitrary",)),
        name="mlstm_gates",
    )(g, grow, jnp.pad(gb, (0, LANE - N_GATES)).reshape(1, LANE), gb.reshape(-1, 1), _gate_select_matrix())


def _mlstm_chunk(q, k, vt, ones_row, b_row, i_row, c_col, ct_ref, m_ref, reverse):
    tt = q.shape[0]
    mask = _tri(tt, upper=not reverse)
    end = 0 if reverse else tt - 1
    b_end = b_row[:, end:end + 1]
    dm = jnp.where(mask, c_col + b_row, -jnp.inf)
    dmax = jnp.max(dm, axis=0, keepdims=True)
    pt = (_dot_nt(k, q) * jnp.exp(dm - dmax)).astype(BF16)
    pv = _dot(vt, pt)
    g = b_end - b_row + i_row
    gmax = jnp.max(g, axis=1, keepdims=True)

    m_prev = m_ref[...]
    ct_prev = ct_ref[...]
    m_inter = b_row + m_prev
    m_t = jnp.maximum(m_inter, dmax)
    num = jnp.exp(dmax - m_t) * pv + jnp.exp(m_inter - m_t) * _dot_nt(ct_prev.astype(BF16), q)
    den = num[ones_row:ones_row + 1, :]
    ht = num * (1.0 / jnp.maximum(jnp.abs(den), jnp.exp(-m_t)))
    m_new = jnp.maximum(b_end + m_prev, gmax)
    vw = (vt.astype(F32) * jnp.exp(g - m_new)).astype(BF16)
    ct_ref[...] = jnp.exp(b_end + m_prev - m_new) * ct_prev + _dot(vw, k)
    m_ref[...] = m_new
    return ht


def _mlstm_kernel(qx_ref, kx_ref, vx_ref, ox_ref, rwx_ref, ccx_ref,
                  qc_ref, kc_ref, vc_ref, oc_ref, rwc_ref, ccc_ref,
                  wq_ref, wk_ref, ng_ref,
                  yx_ref, yc_ref,
                  qsx, ksx, vtx, qsc, ksc, vtc, hfx, hbx, hfc, hbc, ct_scr, m_scr,
                  *, head_dim, write_ctx, unroll):
    pair = pl.program_id(1)
    tt = rwx_ref.shape[2]
    kscale = head_dim ** -0.5
    wins = _head_windows(head_dim)
    lane = lax.broadcasted_iota(jnp.int32, (1, WIN), 1)

    def prep(q_ref, k_ref, v_ref, qs, ks, vts):
        qa = _silu(_short_conv(q_ref[...].astype(F32), wq_ref[...]))
        ka = _silu(_short_conv(k_ref[...].astype(F32), wk_ref[...])) * kscale
        va = v_ref[...].astype(F32)
        for j, (win, valid, free) in enumerate(wins):
            qs[j] = jnp.where(valid, qa[:, win], 0.0).astype(BF16)
            ks[j] = jnp.where(valid, ka[:, win], 0.0).astype(BF16)
            vm = jnp.where(lane == free, 1.0, jnp.where(valid, va[:, win], 0.0))
            for ci in range(vts.shape[1]):
                vts[j, ci] = vm[ci * tt:(ci + 1) * tt, :].T.astype(BF16)

    prep(qx_ref, kx_ref, vx_ref, qsx, ksx, vtx)
    prep(qc_ref, kc_ref, vc_ref, qsc, ksc, vtc)
    ct_scr[...] = jnp.zeros_like(ct_scr)
    m_scr[...] = jnp.zeros_like(m_scr)

    sel_r = lax.broadcasted_iota(jnp.int32, (N_GATES, 1), 0)
    pick_r = lambda a, c: jnp.sum(jnp.where(sel_r == c, a, 0.0), axis=0, keepdims=True)

    def run(ci_f, ci_b, q_s, k_s, v_t, rows_ref, cc_ref, hf, hb):
        for reverse, ci, hout in ((False, ci_f, hf), (True, ci_b, hb)):
            rows = pl.ds(pl.multiple_of(ci * tt, tt), tt)
            gates = rows_ref[ci]
            cc = cc_ref[rows, :]
            for j, (_, _, free) in enumerate(wins):
                irow = (2 * N_HEADS if reverse else 0) + pair * 2 + j
                lane_c = 2 * j + (1 if reverse else 0)
                chain = j * 2 + (1 if reverse else 0)
                ht = _mlstm_chunk(q_s[j, rows, :], k_s[j, rows, :], v_t[j, ci], free,
                                  pick_r(gates, irow + N_HEADS), pick_r(gates, irow),
                                  cc[:, lane_c:lane_c + 1], ct_scr.at[chain], m_scr.at[chain], reverse)
                hout[j, rows, :] = ht.T

    _scan_loops(qc_ref.shape[0] // tt, qx_ref.shape[0] // tt,
                lambda f, b: run(f, b, qsc, ksc, vtc, rwc_ref, ccc_ref, hfc, hbc),
                lambda f, b: run(f, b, qsx, ksx, vtx, rwx_ref, ccx_ref, hfx, hbx), unroll)

    def finish(hf, hb, o_ref, y_ref):
        hn = _pair_finish(hf, hb, head_dim, mask_invalid=True)
        y_ref[...] = (hn * ng_ref[...] * _sigmoid(o_ref[...].astype(F32))).astype(BF16)

    finish(hfx, hbx, ox_ref, yx_ref)
    if write_ctx:
        finish(hfc, hbc, oc_ref, yc_ref)
    else:
        yc_ref[...] = jnp.zeros_like(yc_ref)


def _mlstm(px, pc, gx, gc, seq, seq_c, col0, conv_w, gate_b, norm_g, head_dim, write_ctx):
    batch = px.shape[0] // seq
    total = N_HEADS * head_dim
    npair = total // PAIR_W
    cb = col0 // PAIR_W
    tt = MXU_W if (seq % MXU_W == 0 and seq_c % MXU_W == 0) else CHUNK
    rows_x, cc_x = _mlstm_gates(gx, seq, tt, gate_b)
    rows_c, cc_c = _mlstm_gates(gc, seq_c, tt, gate_b)

    def pspecs(s):
        return [pl.BlockSpec((s, PAIR_W), lambda b, h, k=k: (b, cb + k * npair + h)) for k in range(4)]

    def gspecs(s):
        return [pl.BlockSpec((s // tt, N_GATES, tt), lambda b, h: (b, 0, 0)),
                pl.BlockSpec((s, LANE), lambda b, h: (b, h))]

    scr = lambda s, dt: pltpu.VMEM((2, s, WIN), dt)
    scr_t = lambda s: pltpu.VMEM((2, s // tt, WIN, tt), BF16)
    kern = functools.partial(_mlstm_kernel, head_dim=head_dim, write_ctx=write_ctx, unroll=2)
    return pl.pallas_call(
        kern,
        out_shape=(jax.ShapeDtypeStruct((batch * seq, total), BF16),
                   jax.ShapeDtypeStruct((batch * seq_c, total), BF16)),
        grid=(batch, npair),
        in_specs=(pspecs(seq) + gspecs(seq) + pspecs(seq_c) + gspecs(seq_c) + [
            pl.BlockSpec((3, PAIR_W), lambda b, h: (0, h)),
            pl.BlockSpec((3, PAIR_W), lambda b, h: (0, npair + h)),
            pl.BlockSpec((1, PAIR_W), lambda b, h: (0, h)),
        ]),
        out_specs=(pl.BlockSpec((seq, PAIR_W), lambda b, h: (b, h)),
                   pl.BlockSpec((seq_c, PAIR_W), lambda b, h: (b, h))),
        scratch_shapes=[scr(seq, BF16), scr(seq, BF16), scr_t(seq),
                        scr(seq_c, BF16), scr(seq_c, BF16), scr_t(seq_c),
                        scr(seq, F32), scr(seq, F32), scr(seq_c, F32), scr(seq_c, F32),
                        pltpu.VMEM((4, WIN, WIN), F32),
                        pltpu.VMEM((4, 1, 1), F32)],
        compiler_params=_cparams(("arbitrary", "arbitrary")),
        name="mlstm",
    )(px, px, px, px, rows_x, cc_x, pc, pc, pc, pc, rows_c, cc_c,
      conv_w, conv_w, norm_g.reshape(1, total))


def _ret_chunk(q, k, v, dmask, q_decay, k_decay, c_decay, s_ref):
    pv = _dot((_dot_nt(q, k) * dmask).astype(BF16), v)
    kd = (k.astype(F32) * k_decay).astype(BF16)
    s_prev = s_ref[...]
    s_ref[...] = c_decay * s_prev + _dot_tn(kd, v)
    return pv + q_decay * _dot(q, s_prev.astype(BF16))


def _ret_kernel(ld_ref, qx_ref, kx_ref, vx_ref, gx_ref, qc_ref, kc_ref, vc_ref, gc_ref, cos_ref, sgn_ref,
                yx_ref, yc_ref,
                qsx, ksx, vsx, qsc, ksc, vsc, ofx, obx, ofc, obc, s_scr,
                *, key_dim, val_dim, write_ctx, unroll):
    pair = pl.program_id(1)
    t = CHUNK
    qscale = key_dim ** -0.5
    half = key_dim // 2
    vwins = _head_windows(val_dim)
    lane = lax.broadcasted_iota(jnp.int32, (1, WIN), 1)
    klane = lax.broadcasted_iota(jnp.int32, (1, qx_ref.shape[1]), 1)
    first_half = (klane % key_dim) < half

    def rope(a):
        width = a.shape[1]
        partner = jnp.where(first_half, pltpu.roll(a, width - half, axis=1), pltpu.roll(a, half, axis=1))
        return a * cos_ref[...] + partner * sgn_ref[...]

    def prep(qa, ka, v_ref, qs, ks, vs):
        qw = jnp.where(pair == 0, qa[:, :WIN], qa[:, LANE:LANE + WIN])
        kw = jnp.where(pair == 0, ka[:, :WIN], ka[:, LANE:LANE + WIN])
        va = v_ref[...].astype(F32)
        for j, (win, valid, _) in enumerate(vwins):
            start = (2 * pair + j) * key_dim - LANE * pair
            kvalid = jnp.logical_and(lane >= start, lane < start + key_dim)
            qs[j] = jnp.where(kvalid, qw, 0.0).astype(BF16)
            ks[j] = jnp.where(kvalid, kw, 0.0).astype(BF16)
            vs[j] = jnp.where(valid, va[:, win], 0.0).astype(BF16)

    prep(rope(qx_ref[...].astype(F32)) * qscale, rope(kx_ref[...].astype(F32)), vx_ref, qsx, ksx, vsx)
    prep(qc_ref[...].astype(F32) * qscale, kc_ref[...].astype(F32), vc_ref, qsc, ksc, vsc)
    s_scr[...] = jnp.zeros_like(s_scr)

    r = lax.broadcasted_iota(jnp.int32, (t, t), 0)
    c = lax.broadcasted_iota(jnp.int32, (t, t), 1)
    idx = lax.broadcasted_iota(jnp.int32, (t, 1), 0).astype(F32)
    consts = []
    for j in range(2):
        for reverse in (False, True):
            lg = -jnp.exp(jnp.full((1, 1), ld_ref[1 if reverse else 0, pair * 2 + j], F32))
            rel = (c - r) if reverse else (r - c)
            dmask = jnp.where(rel >= 0, jnp.exp(lg * jnp.maximum(rel, 0).astype(F32)), 0.0)
            pos = (t - 1.0 - idx) if reverse else idx
            consts.append((dmask, jnp.exp(lg * (pos + 1.0)), jnp.exp(lg * (t - 1.0 - pos)), jnp.exp(lg * t)))

    def run(ci_f, ci_b, q_s, k_s, v_s, of, ob):
        for reverse, ci, oout in ((False, ci_f, of), (True, ci_b, ob)):
            rows = pl.ds(pl.multiple_of(ci * t, t), t)
            for j in range(2):
                chain = j * 2 + (1 if reverse else 0)
                dmask, q_decay, k_decay, c_decay = consts[chain]
                oout[j, rows, :] = _ret_chunk(q_s[j, rows, :], k_s[j, rows, :], v_s[j, rows, :],
                                              dmask, q_decay, k_decay, c_decay, s_scr.at[chain])

    _scan_loops(qc_ref.shape[0] // t, qx_ref.shape[0] // t,
                lambda f, b: run(f, b, qsc, ksc, vsc, ofc, obc),
                lambda f, b: run(f, b, qsx, ksx, vsx, ofx, obx), unroll)

    def finish(of, ob, g_ref, y_ref):
        on = _pair_finish(of, ob, val_dim, mask_invalid=False)
        y_ref[...] = (on * _silu(g_ref[...].astype(F32))).astype(BF16)

    finish(ofx, obx, gx_ref, yx_ref)
    if write_ctx:
        finish(ofc, obc, gc_ref, yc_ref)
    else:
        yc_ref[...] = jnp.zeros_like(yc_ref)


def _retention(px, pc, seq, seq_c, col0, log_decay, cosf, sgnf, key_dim, val_dim, write_ctx):
    batch = px.shape[0] // seq
    ktot, vtot = N_HEADS * key_dim, N_HEADS * val_dim
    assert ktot == PAIR_W
    npair = vtot // PAIR_W
    qb = col0 // PAIR_W
    vb = (col0 + 2 * ktot) // PAIR_W

    def pspecs(s):
        return [pl.BlockSpec((s, PAIR_W), lambda b, h: (b, qb)),
                pl.BlockSpec((s, PAIR_W), lambda b, h: (b, qb + 1)),
                pl.BlockSpec((s, PAIR_W), lambda b, h: (b, vb + h)),
                pl.BlockSpec((s, PAIR_W), lambda b, h: (b, vb + npair + h))]

    scr = lambda s, dt: pltpu.VMEM((2, s, WIN), dt)
    kern = functools.partial(_ret_kernel, key_dim=key_dim, val_dim=val_dim, write_ctx=write_ctx, unroll=2)
    return pl.pallas_call(
        kern,
        out_shape=(jax.ShapeDtypeStruct((batch * seq, vtot), BF16),
                   jax.ShapeDtypeStruct((batch * seq_c, vtot), BF16)),
        grid=(batch, npair),
        in_specs=([pl.BlockSpec(memory_space=pltpu.SMEM)] + pspecs(seq) + pspecs(seq_c) + [
            _resident((seq, PAIR_W), lambda b, h: (0, 0)),
            _resident((seq, PAIR_W), lambda b, h: (0, 0)),
        ]),
        out_specs=(pl.BlockSpec((seq, PAIR_W), lambda b, h: (b, h)),
                   pl.BlockSpec((seq_c, PAIR_W), lambda b, h: (b, h))),
        scratch_shapes=[scr(seq, BF16), scr(seq, BF16), scr(seq, BF16),
                        scr(seq_c, BF16), scr(seq_c, BF16), scr(seq_c, BF16),
                        scr(seq, F32), scr(seq, F32), scr(seq_c, F32), scr(seq_c, F32),
                        pltpu.VMEM((4, WIN, WIN), F32)],
        compiler_params=_cparams(("arbitrary", "arbitrary")),
        name="retention",
    )(log_decay.astype(F32), px, px, px, px, pc, pc, pc, pc, cosf, sgnf)


@functools.lru_cache(maxsize=None)
def _rope_tables(seq, key_dim):
    rows = seq // GRID_W
    r = np.repeat(np.arange(rows, dtype=np.float32), GRID_W)
    col = np.tile(np.arange(GRID_W, dtype=np.float32), rows)
    nf = key_dim // 4
    inv = (np.float32(ROPE_BASE) ** (-np.arange(nf, dtype=np.float32) / np.float32(nf))).astype(np.float32)
    ang = np.concatenate([r[:, None] * inv, col[:, None] * inv], axis=-1).astype(np.float32)
    cos, sin = np.cos(ang).astype(np.float32), np.sin(ang).astype(np.float32)
    return np.tile(np.concatenate([cos, cos], axis=1), (1, N_HEADS)), np.tile(np.concatenate([-sin, sin], axis=1), (1, N_HEADS))


def kernel(x, c, ctx, c_ctx, norm1_g, norm2_g, w_mod, b_mod, w_in, hy_conv_w, hy_f_w1, hy_f_b1, hy_f_w2, hy_f_b2, hy_f_w3, hy_f_freq, hy_bias, ml_conv_w, ml_gate_b, ml_norm_g, rt_log_decay, w_out, w_ff1, w_ff2, final_g):
    batch, seq, d = x.shape
    seq_c = ctx.shape[1]
    depth = w_mod.shape[0]
    hy_w = d // 4
    ml_w = 3 * d // 8
    rt_w = 3 * d // 8
    ml_dh = ml_w // N_HEADS
    rt_dv = rt_w // N_HEADS
    rt_dk = rt_dv // 2
    hy_cols, ml_cols = 3 * hy_w, 4 * ml_w
    n_in = w_in.shape[2]
    gate0 = hy_cols + ml_cols
    rt0 = gate0 + N_GATES
    n_p = -(-n_in // 1024) * 1024

    s_in = jnp.concatenate([c, c_ctx[None], jnp.zeros((MOD_ROWS - batch - 1, d), F32)], axis=0)
    mod = _modulation(s_in, w_mod, b_mod)

    grp_x = lambda tm: (lambda i: (i * tm) // seq)
    grp_c = lambda tm: (lambda i: batch + 0 * i)
    tm_x, tm_c = _pick_tile(seq, 1024), _pick_tile(seq_c, 1024)
    tm_ox, tm_oc = _pick_tile(seq, 512), _pick_tile(seq_c, 512)

    def dft(n):
        cm, sm, ct, st = _dft_tables(n)
        return jnp.asarray(cm).astype(BF16), jnp.asarray(sm).astype(BF16), ct, st

    cm_x, sm_x, ct_x, st_x = dft(seq)
    cm_c, sm_c, ct_c, st_c = dft(seq_c)
    cosf, sgnf = _rope_tables(seq, rt_dk)
    tc = _pick_tile(hy_w, MXU_W)
    tc_spec = _pick_tile(hy_w, LANE)

    xs = x.reshape(batch * seq, d)
    hc = ctx.reshape(batch * seq_c, d)
    for l in range(depth):
        need_ctx = l < depth - 1
        mod3 = mod[l].reshape(MOD_ROWS * N_MOD, 1, d)
        wl = w_in[l]
        w_p = jnp.concatenate([wl[:, :gate0], wl[:, rt0:], wl[:, gate0:rt0],
                               jnp.zeros((d, n_p - n_in), F32)], axis=1).astype(BF16)
        w_g = jnp.pad(wl[:, gate0:rt0], ((0, 0), (0, LANE - N_GATES))).astype(BF16)
        ml_col0, rt_col0 = hy_cols, gate0
        wo = w_out[l].astype(BF16)
        w1 = w_ff1[l].astype(BF16)
        w2 = w_ff2[l].astype(BF16)

        px, gx = _in_proj(xs, norm1_g[l], mod3, grp_x(tm_x), w_p, w_g, tm_x)
        pc, gc = _in_proj(hc, norm1_g[l], mod3, grp_c(tm_c), w_p, w_g, tm_c)

        filt = (hy_f_w1[l], hy_f_b1[l], hy_f_w2[l], hy_f_b2[l], hy_f_w3[l], hy_f_freq[l])
        hf, hb = _hyena_filter(seq, *filt)
        hre, him = _hyena_spectrum(hf, hb, cm_x, sm_x, ct_x, st_x, tc_spec)
        y_hy = _hyena_conv(px, seq, hy_conv_w[l], hre, him, hy_bias[l], cm_x, sm_x, tc)
        y_ml, yc_ml = _mlstm(px, pc, gx, gc, seq, seq_c, ml_col0, ml_conv_w[l], ml_gate_b[l], ml_norm_g[l],
                             ml_dh, need_ctx)
        y_rt, yc_rt = _retention(px, pc, seq, seq_c, rt_col0, rt_log_decay[l], cosf, sgnf,
                                 rt_dk, rt_dv, need_ctx)
        xs, h2 = _out_proj(xs, y_hy, y_ml, y_rt, wo, norm2_g[l], mod3, grp_x(tm_ox), tm_ox, alias=l > 0)
        if need_ctx:
            hfc, hbc = _hyena_filter(seq_c, *filt)
            hre_c, him_c = _hyena_spectrum(hfc, hbc, cm_c, sm_c, ct_c, st_c, tc_spec)
            yc_hy = _hyena_conv(pc, seq_c, hy_conv_w[l], hre_c, him_c, hy_bias[l], cm_c, sm_c, tc)
            hc, h2c = _out_proj(hc, yc_hy, yc_ml, yc_rt, wo, norm2_g[l], mod3, grp_c(tm_oc), tm_oc, alias=l > 0)
            hc = _ffn(hc, h2c, mod3, grp_c(tm_c), w1, w2, tm_c, 1024)
        xs = _ffn(xs, h2, mod3, grp_x(tm_x), w1, w2, tm_x, 1024)
    return _final_norm(xs, final_g, tm_ox).reshape(batch, seq, d)
```

```python
import functools
import math

import jax
import jax.numpy as jnp
import numpy as np
from jax import lax
from jax.experimental import pallas as pl
from jax.experimental.pallas import tpu as pltpu

F32 = jnp.float32
BF16 = jnp.bfloat16

GRID_W = 64
HY_EMB = 33
HY_FF = 64
HY_DECAY_TARGET = 1e-2
HY_SHORT_PCT = 0.3
HY_LONG_PCT = 1.5
N_HEADS = 4
N_GATES = 4 * N_HEADS
CHUNK = 128
ROPE_BASE = 10000.0
N_MOD = 6
EPS = 1e-6

LANE = 128
MXU_W = 256
WIN = 2 * LANE
PAIR_W = 3 * LANE
MOD_ROWS = 16
VMEM_LIMIT = 56 * 1024 * 1024


def _cparams(sem):
    return pltpu.CompilerParams(dimension_semantics=sem, vmem_limit_bytes=VMEM_LIMIT)


def _sigmoid(x):
    return 0.5 * jnp.tanh(0.5 * x) + 0.5


def _silu(x):
    return x * _sigmoid(x)


def _log_sigmoid(x):
    return jnp.minimum(x, 0.0) - jnp.log(1.0 + jnp.exp(-jnp.abs(x)))


def _dot(a, b):
    return jnp.dot(a, b, preferred_element_type=F32)


def _dot_nt(a, b):
    return lax.dot_general(a, b, (((1,), (1,)), ((), ())), preferred_element_type=F32)


def _dot_tn(a, b):
    return lax.dot_general(a, b, (((0,), (0,)), ((), ())), preferred_element_type=F32)


def _split_bf16(a):
    hi = a.astype(BF16)
    lo = (a - hi.astype(F32)).astype(BF16)
    return hi, lo


def _dot_split(a, b_f32):
    hi, lo = _split_bf16(b_f32)
    return _dot(a, hi) + _dot(a, lo)


def _resident(shape, index_map):
    return pl.BlockSpec(shape, index_map, pipeline_mode=pl.Buffered(1))


def _pick_tile(n, target):
    t = min(n, target)
    while n % t:
        t //= 2
    return t


def _norm_mod(x, g, shift, scale):
    ms = jnp.mean(x * x, axis=-1, keepdims=True)
    y = x * lax.rsqrt(ms + EPS) * g
    return y * (1.0 + scale) + shift


def _mod_kernel(s_ref, w_ref, b_ref, o_ref):
    s = _silu(s_ref[...]).astype(BF16)
    w = w_ref[0].astype(BF16)
    o_ref[0] = _dot(s, w) + b_ref[0]


def _modulation(s_in, w_mod, b_mod):
    depth, d, n = w_mod.shape
    tn = _pick_tile(n, 1024)
    return pl.pallas_call(
        _mod_kernel,
        out_shape=jax.ShapeDtypeStruct((depth, MOD_ROWS, n), F32),
        grid=(depth, n // tn),
        in_specs=[
            pl.BlockSpec((MOD_ROWS, d), lambda l, j: (0, 0)),
            pl.BlockSpec((1, d, tn), lambda l, j: (l, 0, j)),
            pl.BlockSpec((1, 1, tn), lambda l, j: (l, 0, j)),
        ],
        out_specs=pl.BlockSpec((1, MOD_ROWS, tn), lambda l, j: (l, 0, j)),
        compiler_params=_cparams(("arbitrary", "arbitrary")),
        name="modulation",
    )(s_in, w_mod, b_mod.reshape(depth, 1, n))


def _inproj_kernel(x_ref, g_ref, shift_ref, scale_ref, w_ref, wg_ref, p_ref, gate_ref, h_scr):
    @pl.when(pl.program_id(1) == 0)
    def _():
        h = _norm_mod(x_ref[...], g_ref[...], shift_ref[0], scale_ref[0]).astype(BF16)
        h_scr[...] = h
        gate_ref[...] = _dot(h, wg_ref[...])

    p_ref[...] = _dot(h_scr[...], w_ref[...]).astype(BF16)


def _in_proj(x2, norm_g, mod3, group_of_tile, w_p, w_g, layer, tm):
    m, d = x2.shape
    n = w_p.shape[2]
    tn = _pick_tile(n, 1024)
    return pl.pallas_call(
        _inproj_kernel,
        out_shape=(jax.ShapeDtypeStruct((m, n), BF16), jax.ShapeDtypeStruct((m, LANE), F32)),
        grid=(m // tm, n // tn),
        in_specs=[
            pl.BlockSpec((tm, d), lambda i, j: (i, 0)),
            pl.BlockSpec((1, d), lambda i, j: (0, 0)),
            pl.BlockSpec((1, 1, d), lambda i, j: (group_of_tile(i) * N_MOD + 0, 0, 0)),
            pl.BlockSpec((1, 1, d), lambda i, j: (group_of_tile(i) * N_MOD + 1, 0, 0)),
            pl.BlockSpec((None, d, tn), lambda i, j: (layer, 0, j)),
            pl.BlockSpec((None, d, LANE), lambda i, j: (layer, 0, 0)),
        ],
        out_specs=(pl.BlockSpec((tm, tn), lambda i, j: (i, j)),
                   pl.BlockSpec((tm, LANE), lambda i, j: (i, 0))),
        scratch_shapes=[pltpu.VMEM((tm, d), BF16)],
        compiler_params=_cparams(("arbitrary", "arbitrary")),
        name="in_proj",
    )(x2, norm_g.reshape(1, d), mod3, mod3, w_p, w_g)


def _outproj_kernel(x_ref, yh_ref, ym_ref, yr_ref, w_ref, gate_ref,
                    g2_ref, shift_ref, scale_ref, o_ref, h2_ref):
    kh, km = yh_ref.shape[1], ym_ref.shape[1]
    acc = _dot(yh_ref[...], w_ref[0:kh, :])
    acc = acc + _dot(ym_ref[...], w_ref[kh:kh + km, :])
    acc = acc + _dot(yr_ref[...], w_ref[kh + km:, :])
    xn = x_ref[...] + gate_ref[0] * acc
    o_ref[...] = xn
    h2_ref[...] = _norm_mod(xn, g2_ref[...], shift_ref[0], scale_ref[0]).astype(BF16)


def _out_proj(x2, y_hy, y_ml, y_rt, wo, layer, norm2_g, mod3, group_of_tile, tm, alias):
    m, d = x2.shape
    kh, km, kr = y_hy.shape[1], y_ml.shape[1], y_rt.shape[1]
    mrow = lambda k: pl.BlockSpec((1, 1, d), lambda i: (group_of_tile(i) * N_MOD + k, 0, 0))
    return pl.pallas_call(
        _outproj_kernel,
        out_shape=(jax.ShapeDtypeStruct((m, d), F32), jax.ShapeDtypeStruct((m, d), BF16)),
        grid=(m // tm,),
        in_specs=[
            pl.BlockSpec((tm, d), lambda i: (i, 0)),
            pl.BlockSpec((tm, kh), lambda i: (i, 0)),
            pl.BlockSpec((tm, km), lambda i: (i, 0)),
            pl.BlockSpec((tm, kr), lambda i: (i, 0)),
            _resident((None, kh + km + kr, d), lambda i: (layer, 0, 0)),
            mrow(2),
            pl.BlockSpec((1, d), lambda i: (0, 0)),
            mrow(3), mrow(4),
        ],
        out_specs=(pl.BlockSpec((tm, d), lambda i: (i, 0)), pl.BlockSpec((tm, d), lambda i: (i, 0))),
        input_output_aliases=({0: 0} if alias else {}),
        compiler_params=_cparams(("arbitrary",)),
        name="out_proj",
    )(x2, y_hy, y_ml, y_rt, wo, mod3, norm2_g.reshape(1, d), mod3, mod3)


def _ffn_kernel(h_ref, x_ref, gate_ref, w1_ref, w2_ref, o_ref, hid_scr, *, n1):
    j = pl.program_id(1)
    tf = w1_ref.shape[1]

    @pl.when(j < n1)
    def _():
        a = jnp.maximum(_dot(h_ref[...], w1_ref[...]), 0.0)
        hid_scr[jnp.minimum(j, n1 - 1)] = (a * a).astype(BF16)

    @pl.when(j >= n1)
    def _():
        acc = _dot(hid_scr[0], w2_ref[0:tf, :])
        for c in range(1, n1):
            acc = acc + _dot(hid_scr[c], w2_ref[c * tf:(c + 1) * tf, :])
        o_ref[...] = x_ref[...] + gate_ref[0] * acc


def _ffn(x2, h2, mod3, group_of_tile, w1, w2, layer, tm, tf):
    m, d = x2.shape
    f = w1.shape[2]
    n1 = f // tf
    tn = MXU_W
    n2 = d // tn
    col = lambda j: jnp.maximum(j - n1, 0)
    return pl.pallas_call(
        functools.partial(_ffn_kernel, n1=n1),
        out_shape=jax.ShapeDtypeStruct((m, d), F32),
        grid=(m // tm, n1 + n2),
        in_specs=[
            pl.BlockSpec((tm, d), lambda i, j: (i, 0)),
            pl.BlockSpec((tm, tn), lambda i, j: (i, col(j))),
            pl.BlockSpec((1, 1, tn), lambda i, j: (group_of_tile(i) * N_MOD + 5, 0, col(j))),
            pl.BlockSpec((None, d, tf), lambda i, j: (layer, 0, jnp.minimum(j, n1 - 1))),
            pl.BlockSpec((None, f, tn), lambda i, j: (layer, 0, col(j))),
        ],
        out_specs=pl.BlockSpec((tm, tn), lambda i, j: (i, col(j))),
        scratch_shapes=[pltpu.VMEM((n1, tm, tf), BF16)],
        compiler_params=_cparams(("arbitrary", "arbitrary")),
        name="ffn",
    )(h2, x2, mod3, w1, w2)


def _final_norm_kernel(x_ref, g_ref, o_ref):
    x = x_ref[...]
    ms = jnp.mean(x * x, axis=-1, keepdims=True)
    o_ref[...] = x * lax.rsqrt(ms + EPS) * g_ref[...]


def _final_norm(x2, g, tm):
    m, d = x2.shape
    return pl.pallas_call(
        _final_norm_kernel,
        out_shape=jax.ShapeDtypeStruct((m, d), F32),
        grid=(m // tm,),
        in_specs=[pl.BlockSpec((tm, d), lambda i: (i, 0)), pl.BlockSpec((1, d), lambda i: (0, 0))],
        out_specs=pl.BlockSpec((tm, d), lambda i: (i, 0)),
        compiler_params=_cparams(("arbitrary",)),
        name="final_norm",
    )(x2, g.reshape(1, d))


def _hy_filter_kernel(z_ref, t_ref, w1_ref, b1_ref, w2_ref, b2_ref, w3f_ref, w3b_ref, fr_ref, dl_ref,
                      hf_ref, hb_ref):
    hp = lax.Precision.HIGHEST
    dot = lambda a, b: jnp.dot(a, b, precision=hp, preferred_element_type=F32)
    fr = fr_ref[...]
    hdn = jnp.sin(fr * (dot(z_ref[...], w1_ref[...]) + b1_ref[...]))
    hdn = jnp.sin(fr * (dot(hdn, w2_ref[...]) + b2_ref[...]))
    win = jnp.exp(-t_ref[...] * dl_ref[...])
    hf = dot(hdn, w3f_ref[...]) * win
    hb = dot(hdn, w3b_ref[...]) * win
    row = lax.broadcasted_iota(jnp.int32, hb.shape, 0)
    hb = jnp.where(row == 0, 0.0, hb)
    inv = 1.0 / (jnp.sum(jnp.abs(hf), axis=0, keepdims=True) + jnp.sum(jnp.abs(hb), axis=0, keepdims=True))
    hf_ref[...] = hf * inv
    hb_ref[...] = hb * inv


@functools.lru_cache(maxsize=None)
def _filter_tables(seq, hy_w):
    t = np.linspace(0.0, 1.0, seq, dtype=np.float32)[:, None]
    w = (np.float32(2.0 * math.pi / seq) * np.arange(seq, dtype=np.float32))[:, None]
    nb = (HY_EMB - 1) // 2
    bands = np.linspace(1e-4, nb - 1, nb, dtype=np.float32)[None, :]
    z = np.concatenate([t, np.cos(bands * w), -np.sin(bands * w)], axis=-1).astype(np.float32)
    z = np.pad(z, ((0, 0), (0, LANE - HY_EMB)))
    deltas = np.abs(np.linspace(math.log(HY_DECAY_TARGET) / HY_LONG_PCT,
                                math.log(HY_DECAY_TARGET) / HY_SHORT_PCT, hy_w, dtype=np.float32))[None, :]
    return z, t, deltas


def _hyena_filter(seq, w1, b1, w2, b2, w3, freq):
    hy_w = w3.shape[1] // 2
    z, t, deltas = _filter_tables(seq, hy_w)
    pad_ff = LANE - HY_FF
    w1p = jnp.pad(w1, ((0, LANE - HY_EMB), (0, pad_ff)))
    w2p = jnp.pad(w2, ((0, pad_ff), (0, pad_ff)))
    w3p = jnp.pad(w3, ((0, pad_ff), (0, 0)))
    padv = lambda v: jnp.pad(v, (0, pad_ff)).reshape(1, LANE)
    out = jax.ShapeDtypeStruct((seq, hy_w), F32)
    return pl.pallas_call(
        _hy_filter_kernel,
        out_shape=(out, out),
        compiler_params=pltpu.CompilerParams(vmem_limit_bytes=VMEM_LIMIT),
        name="hyena_filter",
    )(z, t, w1p, padv(b1), w2p, padv(b2), w3p[:, :hy_w], w3p[:, hy_w:], padv(freq), deltas)


@functools.lru_cache(maxsize=None)
def _dft_tables(seq):
    k = np.arange(seq, dtype=np.int64)
    mm = ((2 * k[:, None] + 1) * (2 * k[None, :] + 1)) % (8 * seq)
    ang = mm.astype(np.float64) * (2.0 * math.pi / (8 * seq))
    theta = (k.astype(np.float64) + 0.5) * (math.pi / (2 * seq))
    return (np.cos(ang).astype(np.float32), np.sin(ang).astype(np.float32),
            np.cos(theta).astype(np.float32)[:, None], np.sin(theta).astype(np.float32)[:, None])


def _hy_spec_kernel(hf_ref, hb_ref, c_ref, s_ref, ct_ref, st_ref, hre_ref, him_ref):
    cm, sm = c_ref[...], s_ref[...]
    seq = cm.shape[0]
    tc = hf_ref.shape[1]
    taps = jnp.concatenate([hf_ref[...], hb_ref[...]], axis=1)
    ca, sa = _dot_split(cm, taps), _dot_split(sm, taps)
    a, a2 = ca[:, :tc], ca[:, tc:]
    b, b2 = sa[:, :tc], sa[:, tc:]
    ct, st = ct_ref[...], st_ref[...]
    scale = 1.0 / seq
    hre_ref[...] = ((a + a2) * ct + (b + b2) * st) * scale
    him_ref[...] = ((a - a2) * st - (b - b2) * ct) * scale


def _hyena_spectrum(hf, hb, cm, sm, ct, st, tc):
    seq, c = hf.shape
    out = jax.ShapeDtypeStruct((seq, c), F32)
    col = pl.BlockSpec((seq, tc), lambda j: (0, j))
    return pl.pallas_call(
        _hy_spec_kernel,
        out_shape=(out, out),
        grid=(c // tc,),
        in_specs=[col, col,
                  _resident((seq, seq), lambda j: (0, 0)), _resident((seq, seq), lambda j: (0, 0)),
                  pl.BlockSpec((seq, 1), lambda j: (0, 0)), pl.BlockSpec((seq, 1), lambda j: (0, 0))],
        out_specs=(col, col),
        compiler_params=_cparams(("arbitrary",)),
        name="hyena_spectrum",
    )(hf, hb, cm, sm, ct, st)


def _short_conv(p, w):
    n = p.shape[0]
    pz = jnp.concatenate([p, jnp.zeros((8, p.shape[1]), p.dtype)], axis=0)
    prev = pltpu.roll(pz, 1, axis=0)[:n]
    nxt = pltpu.roll(pz, n + 7, axis=0)[:n]
    return prev * w[0:1] + p * w[1:2] + nxt * w[2:3]


def _hy_conv_kernel(p0_ref, p1_ref, pv_ref, w0_ref, w1_ref, wv_ref, hre_ref, him_ref, bias_ref,
                    c_ref, s_ref, o_ref):
    x1 = _short_conv(p1_ref[...].astype(F32), w1_ref[...])
    v = _short_conv(pv_ref[...].astype(F32), wv_ref[...])
    u = v * x1
    ub = u.astype(BF16)
    cm, sm = c_ref[...], s_ref[...]
    zc = _dot(cm, ub)
    zs = _dot(sm, ub)
    hre, him = hre_ref[...], him_ref[...]
    yre = (hre * zc + him * zs).astype(BF16)
    yim = (him * zc - hre * zs).astype(BF16)
    y = _dot(cm, yre) - _dot(sm, yim)
    x0 = _short_conv(p0_ref[...].astype(F32), w0_ref[...])
    o_ref[...] = ((y + u * bias_ref[...]) * x0).astype(BF16)


def _hyena_conv(p, seq, conv_w, hre, him, bias, cm, sm, tc):
    m = p.shape[0]
    hy_w = hre.shape[1]
    nb = hy_w // tc
    pspec = lambda off: pl.BlockSpec((seq, tc), lambda j, b, off=off: (b, off * nb + j))
    wspec = lambda off: pl.BlockSpec((3, tc), lambda j, b, off=off: (0, off * nb + j))
    col = pl.BlockSpec((seq, tc), lambda j, b: (0, j))
    return pl.pallas_call(
        _hy_conv_kernel,
        out_shape=jax.ShapeDtypeStruct((m, hy_w), BF16),
        grid=(nb, m // seq),
        in_specs=[pspec(0), pspec(1), pspec(2), wspec(0), wspec(1), wspec(2), col, col,
                  pl.BlockSpec((1, tc), lambda j, b: (0, j)),
                  _resident((seq, seq), lambda j, b: (0, 0)), _resident((seq, seq), lambda j, b: (0, 0))],
        out_specs=pl.BlockSpec((seq, tc), lambda j, b: (b, j)),
        compiler_params=_cparams(("arbitrary", "arbitrary")),
        name="hyena_conv",
    )(p, p, p, conv_w, conv_w, conv_w, hre, him, bias.reshape(1, hy_w), cm, sm)


def _tri(n, upper):
    r = lax.broadcasted_iota(jnp.int32, (n, n), 0)
    c = lax.broadcasted_iota(jnp.int32, (n, n), 1)
    return (r <= c) if upper else (r >= c)


def _head_windows(head_dim):
    assert 2 * head_dim == PAIR_W
    lane = lax.broadcasted_iota(jnp.int32, (1, WIN), 1)
    return ((slice(0, WIN), lane < head_dim, head_dim),
            (slice(LANE, LANE + WIN), lane >= WIN - head_dim, 0))


def _pair_finish(acc_f, acc_b, head_dim, mask_invalid):
    outs = []
    for j, (_, valid, _) in enumerate(_head_windows(head_dim)):
        h = acc_f[j] + acc_b[j]
        if mask_invalid:
            h = jnp.where(valid, h, 0.0)
        ms = jnp.sum(h * h, axis=-1, keepdims=True) * (1.0 / head_dim)
        outs.append(h * lax.rsqrt(ms + EPS))
    h0, h1 = outs
    return jnp.concatenate([h0[:, :LANE], h0[:, LANE:] + h1[:, :LANE], h1[:, LANE:]], axis=1)


def _scan_loops(nc_c, nc_x, run_c, run_x, unroll):
    def loop(nc, run):
        u = unroll if nc % unroll == 0 else 1

        def body(i, carry):
            for k in range(u):
                run(i * u + k, nc - 1 - (i * u + k))
            return carry

        lax.fori_loop(0, nc // u, body, 0)

    loop(nc_c, run_c)
    loop(nc_x, run_x)


@functools.lru_cache(maxsize=None)
def _gate_select_matrix():
    e = np.zeros((2 * LANE, 2 * LANE), np.float32)
    for pair in range(N_HEADS // 2):
        for j in range(2):
            for direction in range(2):
                icol = 2 * N_HEADS * direction + 2 * pair + j
                out = LANE * pair + 2 * j + direction
                e[icol, out] = 1.0
                e[LANE + icol + N_HEADS, out] = -1.0
    return e.astype(BF16)


def _ml_gate_kernel(gc_ref, gr_ref, gbc_ref, gbr_ref, esel_ref, rows_ref, cc_ref):
    nc, _, tt = gr_ref.shape
    tri_lo = _tri(tt, upper=False).astype(BF16)
    tri_up = _tri(tt, upper=True).astype(BF16)
    lane = lax.broadcasted_iota(jnp.int32, (1, LANE), 1)
    row = lax.broadcasted_iota(jnp.int32, (N_GATES, 1), 0)
    esel = esel_ref[...]
    for ci in range(nc):
        rs = slice(ci * tt, (ci + 1) * tt)
        gc = gc_ref[rs, :] + gbc_ref[...]
        gr = gr_ref[ci] + gbr_ref[...]
        ls = _log_sigmoid(gc)
        cum_c = jnp.where(lane >= 2 * N_HEADS, _dot_split(tri_up, ls), _dot_split(tri_lo, ls))
        hi, lo = _split_bf16(jnp.concatenate([gc, cum_c], axis=1))
        cc_ref[rs, :] = _dot(hi, esel) + _dot(lo, esel)
        hi, lo = _split_bf16(_log_sigmoid(gr))
        prefix = _dot(hi, tri_up) + _dot(lo, tri_up)
        suffix = _dot(hi, tri_lo) + _dot(lo, tri_lo)
        is_forget = (row % (2 * N_HEADS)) >= N_HEADS
        rows_ref[ci] = jnp.where(is_forget, jnp.where(row >= 2 * N_HEADS, suffix, prefix), gr)


def _mlstm_gates(g, seq, tt, gate_b):
    batch = g.shape[0] // seq
    nc = seq // tt
    grow = g[:, :N_GATES].reshape(-1, tt, N_GATES).transpose(0, 2, 1)
    gb = gate_b.astype(F32)
    return pl.pallas_call(
        _ml_gate_kernel,
        out_shape=(jax.ShapeDtypeStruct((batch * nc, N_GATES, tt), F32),
                   jax.ShapeDtypeStruct((batch * seq, 2 * LANE), F32)),
        grid=(batch,),
        in_specs=[pl.BlockSpec((seq, LANE), lambda b: (b, 0)),
                  pl.BlockSpec((nc, N_GATES, tt), lambda b: (b, 0, 0)),
                  pl.BlockSpec((1, LANE), lambda b: (0, 0)),
                  pl.BlockSpec((N_GATES, 1), lambda b: (0, 0)),
                  pl.BlockSpec((2 * LANE, 2 * LANE), lambda b: (0, 0))],
        out_specs=(pl.BlockSpec((nc, N_GATES, tt), lambda b: (b, 0, 0)),
                   pl.BlockSpec((seq, 2 * LANE), lambda b: (b, 0))),
        compiler_params=_cparams(("arbitrary",)),
        name="mlstm_gates",
    )(g, grow, jnp.pad(gb, (0, LANE - N_GATES)).reshape(1, LANE), gb.reshape(-1, 1), _gate_select_matrix())


def _mlstm_chunk(q, k, vt, ones_row, b_row, i_row, c_col, ct_ref, m_ref, reverse):
    tt = q.shape[0]
    mask = _tri(tt, upper=not reverse)
    end = 0 if reverse else tt - 1
    b_end = b_row[:, end:end + 1]
    dm = jnp.where(mask, c_col + b_row, -jnp.inf)
    dmax = jnp.max(dm, axis=0, keepdims=True)
    pt = (_dot_nt(k, q) * jnp.exp(dm - dmax)).astype(BF16)
    pv = _dot(vt, pt)
    g = b_end - b_row + i_row
    gmax = jnp.max(g, axis=1, keepdims=True)

    m_prev = m_ref[...]
    ct_prev = ct_ref[...]
    m_inter = b_row + m_prev
    m_t = jnp.maximum(m_inter, dmax)
    num = jnp.exp(dmax - m_t) * pv + jnp.exp(m_inter - m_t) * _dot_nt(ct_prev.astype(BF16), q)
    den = num[ones_row:ones_row + 1, :]
    ht = num * (1.0 / jnp.maximum(jnp.abs(den), jnp.exp(-m_t)))
    m_new = jnp.maximum(b_end + m_prev, gmax)
    vw = (vt.astype(F32) * jnp.exp(g - m_new)).astype(BF16)
    ct_ref[...] = jnp.exp(b_end + m_prev - m_new) * ct_prev + _dot(vw, k)
    m_ref[...] = m_new
    return ht


def _mlstm_kernel(qx_ref, kx_ref, vx_ref, ox_ref, rwx_ref, ccx_ref,
                  qc_ref, kc_ref, vc_ref, oc_ref, rwc_ref, ccc_ref,
                  wq_ref, wk_ref, ng_ref,
                  yx_ref, yc_ref,
                  qsx, ksx, vtx, qsc, ksc, vtc, hfx, hbx, hfc, hbc, ct_scr, m_scr,
                  *, head_dim, write_ctx, unroll):
    pair = pl.program_id(1)
    tt = rwx_ref.shape[2]
    kscale = head_dim ** -0.5
    wins = _head_windows(head_dim)
    lane = lax.broadcasted_iota(jnp.int32, (1, WIN), 1)

    def prep(q_ref, k_ref, v_ref, qs, ks, vts):
        qa = _silu(_short_conv(q_ref[...].astype(F32), wq_ref[...]))
        ka = _silu(_short_conv(k_ref[...].astype(F32), wk_ref[...])) * kscale
        va = v_ref[...].astype(F32)
        for j, (win, valid, free) in enumerate(wins):
            qs[j] = jnp.where(valid, qa[:, win], 0.0).astype(BF16)
            ks[j] = jnp.where(valid, ka[:, win], 0.0).astype(BF16)
            vm = jnp.where(lane == free, 1.0, jnp.where(valid, va[:, win], 0.0))
            for ci in range(vts.shape[1]):
                vts[j, ci] = vm[ci * tt:(ci + 1) * tt, :].T.astype(BF16)

    prep(qx_ref, kx_ref, vx_ref, qsx, ksx, vtx)
    prep(qc_ref, kc_ref, vc_ref, qsc, ksc, vtc)
    ct_scr[...] = jnp.zeros_like(ct_scr)
    m_scr[...] = jnp.zeros_like(m_scr)

    sel_r = lax.broadcasted_iota(jnp.int32, (N_GATES, 1), 0)
    pick_r = lambda a, c: jnp.sum(jnp.where(sel_r == c, a, 0.0), axis=0, keepdims=True)

    def run(ci_f, ci_b, q_s, k_s, v_t, rows_ref, cc_ref, hf, hb):
        for reverse, ci, hout in ((False, ci_f, hf), (True, ci_b, hb)):
            rows = pl.ds(pl.multiple_of(ci * tt, tt), tt)
            gates = rows_ref[ci]
            cc = cc_ref[rows, :]
            for j, (_, _, free) in enumerate(wins):
                irow = (2 * N_HEADS if reverse else 0) + pair * 2 + j
                lane_c = 2 * j + (1 if reverse else 0)
                chain = j * 2 + (1 if reverse else 0)
                ht = _mlstm_chunk(q_s[j, rows, :], k_s[j, rows, :], v_t[j, ci], free,
                                  pick_r(gates, irow + N_HEADS), pick_r(gates, irow),
                                  cc[:, lane_c:lane_c + 1], ct_scr.at[chain], m_scr.at[chain], reverse)
                hout[j, rows, :] = ht.T

    _scan_loops(qc_ref.shape[0] // tt, qx_ref.shape[0] // tt,
                lambda f, b: run(f, b, qsc, ksc, vtc, rwc_ref, ccc_ref, hfc, hbc),
                lambda f, b: run(f, b, qsx, ksx, vtx, rwx_ref, ccx_ref, hfx, hbx), unroll)

    def finish(hf, hb, o_ref, y_ref):
        hn = _pair_finish(hf, hb, head_dim, mask_invalid=True)
        y_ref[...] = (hn * ng_ref[...] * _sigmoid(o_ref[...].astype(F32))).astype(BF16)

    finish(hfx, hbx, ox_ref, yx_ref)
    if write_ctx:
        finish(hfc, hbc, oc_ref, yc_ref)
    else:
        yc_ref[...] = jnp.zeros_like(yc_ref)


def _mlstm(px, pc, gx, gc, seq, seq_c, col0, conv_w, gate_b, norm_g, head_dim, write_ctx):
    batch = px.shape[0] // seq
    total = N_HEADS * head_dim
    npair = total // PAIR_W
    cb = col0 // PAIR_W
    tt = MXU_W if (seq % MXU_W == 0 and seq_c % MXU_W == 0) else CHUNK
    rows_x, cc_x = _mlstm_gates(gx, seq, tt, gate_b)
    rows_c, cc_c = _mlstm_gates(gc, seq_c, tt, gate_b)

    def pspecs(s):
        return [pl.BlockSpec((s, PAIR_W), lambda b, h, k=k: (b, cb + k * npair + h)) for k in range(4)]

    def gspecs(s):
        return [pl.BlockSpec((s // tt, N_GATES, tt), lambda b, h: (b, 0, 0)),
                pl.BlockSpec((s, LANE), lambda b, h: (b, h))]

    scr = lambda s, dt: pltpu.VMEM((2, s, WIN), dt)
    scr_t = lambda s: pltpu.VMEM((2, s // tt, WIN, tt), BF16)
    kern = functools.partial(_mlstm_kernel, head_dim=head_dim, write_ctx=write_ctx, unroll=2)
    return pl.pallas_call(
        kern,
        out_shape=(jax.ShapeDtypeStruct((batch * seq, total), BF16),
                   jax.ShapeDtypeStruct((batch * seq_c, total), BF16)),
        grid=(batch, npair),
        in_specs=(pspecs(seq) + gspecs(seq) + pspecs(seq_c) + gspecs(seq_c) + [
            pl.BlockSpec((3, PAIR_W), lambda b, h: (0, h)),
            pl.BlockSpec((3, PAIR_W), lambda b, h: (0, npair + h)),
            pl.BlockSpec((1, PAIR_W), lambda b, h: (0, h)),
        ]),
        out_specs=(pl.BlockSpec((seq, PAIR_W), lambda b, h: (b, h)),
                   pl.BlockSpec((seq_c, PAIR_W), lambda b, h: (b, h))),
        scratch_shapes=[scr(seq, BF16), scr(seq, BF16), scr_t(seq),
                        scr(seq_c, BF16), scr(seq_c, BF16), scr_t(seq_c),
                        scr(seq, F32), scr(seq, F32), scr(seq_c, F32), scr(seq_c, F32),
                        pltpu.VMEM((4, WIN, WIN), F32),
                        pltpu.VMEM((4, 1, 1), F32)],
        compiler_params=_cparams(("arbitrary", "arbitrary")),
        name="mlstm",
    )(px, px, px, px, rows_x, cc_x, pc, pc, pc, pc, rows_c, cc_c,
      conv_w, conv_w, norm_g.reshape(1, total))


def _ret_chunk(q, k, v, dmask, q_decay, k_decay, c_decay, s_ref):
    pv = _dot((_dot_nt(q, k) * dmask).astype(BF16), v)
    kd = (k.astype(F32) * k_decay).astype(BF16)
    s_prev = s_ref[...]
    s_ref[...] = c_decay * s_prev + _dot_tn(kd, v)
    return pv + q_decay * _dot(q, s_prev.astype(BF16))


def _ret_kernel(ld_ref, qx_ref, kx_ref, vx_ref, gx_ref, qc_ref, kc_ref, vc_ref, gc_ref, cos_ref, sgn_ref,
                yx_ref, yc_ref,
                qsx, ksx, vsx, qsc, ksc, vsc, ofx, obx, ofc, obc, s_scr,
                *, key_dim, val_dim, write_ctx, unroll, t):
    pair = pl.program_id(1)
    qscale = key_dim ** -0.5
    half = key_dim // 2
    vwins = _head_windows(val_dim)
    lane = lax.broadcasted_iota(jnp.int32, (1, WIN), 1)
    klane = lax.broadcasted_iota(jnp.int32, (1, qx_ref.shape[1]), 1)
    first_half = (klane % key_dim) < half

    def rope(a):
        width = a.shape[1]
        partner = jnp.where(first_half, pltpu.roll(a, width - half, axis=1), pltpu.roll(a, half, axis=1))
        return a * cos_ref[...] + partner * sgn_ref[...]

    def prep(qa, ka, v_ref, qs, ks, vs):
        qw = jnp.where(pair == 0, qa[:, :WIN], qa[:, LANE:LANE + WIN])
        kw = jnp.where(pair == 0, ka[:, :WIN], ka[:, LANE:LANE + WIN])
        va = v_ref[...].astype(F32)
        for j, (win, valid, _) in enumerate(vwins):
            start = (2 * pair + j) * key_dim - LANE * pair
            kvalid = jnp.logical_and(lane >= start, lane < start + key_dim)
            qs[j] = jnp.where(kvalid, qw, 0.0).astype(BF16)
            ks[j] = jnp.where(kvalid, kw, 0.0).astype(BF16)
            vs[j] = jnp.where(valid, va[:, win], 0.0).astype(BF16)

    prep(rope(qx_ref[...].astype(F32)) * qscale, rope(kx_ref[...].astype(F32)), vx_ref, qsx, ksx, vsx)
    prep(qc_ref[...].astype(F32) * qscale, kc_ref[...].astype(F32), vc_ref, qsc, ksc, vsc)
    s_scr[...] = jnp.zeros_like(s_scr)

    r = lax.broadcasted_iota(jnp.int32, (t, t), 0)
    c = lax.broadcasted_iota(jnp.int32, (t, t), 1)
    idx = lax.broadcasted_iota(jnp.int32, (t, 1), 0).astype(F32)
    consts = []
    for j in range(2):
        for reverse in (False, True):
            lg = -jnp.exp(jnp.full((1, 1), ld_ref[1 if reverse else 0, pair * 2 + j], F32))
            rel = (c - r) if reverse else (r - c)
            dmask = jnp.where(rel >= 0, jnp.exp(lg * jnp.maximum(rel, 0).astype(F32)), 0.0)
            pos = (t - 1.0 - idx) if reverse else idx
            consts.append((dmask, jnp.exp(lg * (pos + 1.0)), jnp.exp(lg * (t - 1.0 - pos)), jnp.exp(lg * t)))

    def run(ci_f, ci_b, q_s, k_s, v_s, of, ob):
        for reverse, ci, oout in ((False, ci_f, of), (True, ci_b, ob)):
            rows = pl.ds(pl.multiple_of(ci * t, t), t)
            for j in range(2):
                chain = j * 2 + (1 if reverse else 0)
                dmask, q_decay, k_decay, c_decay = consts[chain]
                oout[j, rows, :] = _ret_chunk(q_s[j, rows, :], k_s[j, rows, :], v_s[j, rows, :],
                                              dmask, q_decay, k_decay, c_decay, s_scr.at[chain])

    _scan_loops(qc_ref.shape[0] // t, qx_ref.shape[0] // t,
                lambda f, b: run(f, b, qsc, ksc, vsc, ofc, obc),
                lambda f, b: run(f, b, qsx, ksx, vsx, ofx, obx), unroll)

    def finish(of, ob, g_ref, y_ref):
        on = _pair_finish(of, ob, val_dim, mask_invalid=False)
        y_ref[...] = (on * _silu(g_ref[...].astype(F32))).astype(BF16)

    finish(ofx, obx, gx_ref, yx_ref)
    if write_ctx:
        finish(ofc, obc, gc_ref, yc_ref)
    else:
        yc_ref[...] = jnp.zeros_like(yc_ref)


def _retention(px, pc, seq, seq_c, col0, log_decay, cosf, sgnf, key_dim, val_dim, write_ctx):
    batch = px.shape[0] // seq
    ktot, vtot = N_HEADS * key_dim, N_HEADS * val_dim
    assert ktot == PAIR_W
    npair = vtot // PAIR_W
    qb = col0 // PAIR_W
    vb = (col0 + 2 * ktot) // PAIR_W

    def pspecs(s):
        return [pl.BlockSpec((s, PAIR_W), lambda b, h: (b, qb)),
                pl.BlockSpec((s, PAIR_W), lambda b, h: (b, qb + 1)),
                pl.BlockSpec((s, PAIR_W), lambda b, h: (b, vb + h)),
                pl.BlockSpec((s, PAIR_W), lambda b, h: (b, vb + npair + h))]

    scr = lambda s, dt: pltpu.VMEM((2, s, WIN), dt)
    t = MXU_W if (seq % MXU_W == 0 and seq_c % MXU_W == 0) else CHUNK
    kern = functools.partial(_ret_kernel, key_dim=key_dim, val_dim=val_dim, write_ctx=write_ctx, unroll=2, t=t)
    return pl.pallas_call(
        kern,
        out_shape=(jax.ShapeDtypeStruct((batch * seq, vtot), BF16),
                   jax.ShapeDtypeStruct((batch * seq_c, vtot), BF16)),
        grid=(batch, npair),
        in_specs=([pl.BlockSpec(memory_space=pltpu.SMEM)] + pspecs(seq) + pspecs(seq_c) + [
            _resident((seq, PAIR_W), lambda b, h: (0, 0)),
            _resident((seq, PAIR_W), lambda b, h: (0, 0)),
        ]),
        out_specs=(pl.BlockSpec((seq, PAIR_W), lambda b, h: (b, h)),
                   pl.BlockSpec((seq_c, PAIR_W), lambda b, h: (b, h))),
        scratch_shapes=[scr(seq, BF16), scr(seq, BF16), scr(seq, BF16),
                        scr(seq_c, BF16), scr(seq_c, BF16), scr(seq_c, BF16),
                        scr(seq, F32), scr(seq, F32), scr(seq_c, F32), scr(seq_c, F32),
                        pltpu.VMEM((4, WIN, WIN), F32)],
        compiler_params=_cparams(("arbitrary", "arbitrary")),
        name="retention",
    )(log_decay.astype(F32), px, px, px, px, pc, pc, pc, pc, cosf, sgnf)


@functools.lru_cache(maxsize=None)
def _rope_tables(seq, key_dim):
    rows = seq // GRID_W
    r = np.repeat(np.arange(rows, dtype=np.float32), GRID_W)
    col = np.tile(np.arange(GRID_W, dtype=np.float32), rows)
    nf = key_dim // 4
    inv = (np.float32(ROPE_BASE) ** (-np.arange(nf, dtype=np.float32) / np.float32(nf))).astype(np.float32)
    ang = np.concatenate([r[:, None] * inv, col[:, None] * inv], axis=-1).astype(np.float32)
    cos, sin = np.cos(ang).astype(np.float32), np.sin(ang).astype(np.float32)
    return np.tile(np.concatenate([cos, cos], axis=1), (1, N_HEADS)), np.tile(np.concatenate([-sin, sin], axis=1), (1, N_HEADS))


def kernel(x, c, ctx, c_ctx, norm1_g, norm2_g, w_mod, b_mod, w_in, hy_conv_w, hy_f_w1, hy_f_b1, hy_f_w2, hy_f_b2, hy_f_w3, hy_f_freq, hy_bias, ml_conv_w, ml_gate_b, ml_norm_g, rt_log_decay, w_out, w_ff1, w_ff2, final_g):
    batch, seq, d = x.shape
    seq_c = ctx.shape[1]
    depth = w_mod.shape[0]
    hy_w = d // 4
    ml_w = 3 * d // 8
    rt_w = 3 * d // 8
    ml_dh = ml_w // N_HEADS
    rt_dv = rt_w // N_HEADS
    rt_dk = rt_dv // 2
    hy_cols, ml_cols = 3 * hy_w, 4 * ml_w
    n_in = w_in.shape[2]
    gate0 = hy_cols + ml_cols
    rt0 = gate0 + N_GATES
    n_p = -(-n_in // 1024) * 1024

    s_in = jnp.concatenate([c, c_ctx[None], jnp.zeros((MOD_ROWS - batch - 1, d), F32)], axis=0)
    mod = _modulation(s_in, w_mod, b_mod)

    grp_x = lambda tm: (lambda i: (i * tm) // seq)
    grp_c = lambda tm: (lambda i: batch + 0 * i)
    tm_x, tm_c = _pick_tile(seq, 1024), _pick_tile(seq_c, 1024)
    tm_ox, tm_oc = _pick_tile(seq, 512), _pick_tile(seq_c, 512)

    def dft(n):
        cm, sm, ct, st = _dft_tables(n)
        return jnp.asarray(cm).astype(BF16), jnp.asarray(sm).astype(BF16), ct, st

    cm_x, sm_x, ct_x, st_x = dft(seq)
    cm_c, sm_c, ct_c, st_c = dft(seq_c)
    cosf, sgnf = _rope_tables(seq, rt_dk)
    tc = _pick_tile(hy_w, MXU_W)
    tc_spec = _pick_tile(hy_w, LANE)

    xs = x.reshape(batch * seq, d)
    hc = ctx.reshape(batch * seq_c, d)
    w_p = jnp.concatenate([w_in[:, :, :gate0], w_in[:, :, rt0:], w_in[:, :, gate0:rt0],
                           jnp.zeros((depth, d, n_p - n_in), F32)], axis=2).astype(BF16)
    w_g = jnp.pad(w_in[:, :, gate0:rt0], ((0, 0), (0, 0), (0, LANE - N_GATES))).astype(BF16)
    ml_col0, rt_col0 = hy_cols, gate0
    wo = w_out.astype(BF16)
    w1 = w_ff1.astype(BF16)
    w2 = w_ff2.astype(BF16)

    for l in range(depth):
        need_ctx = l < depth - 1
        mod3 = mod[l].reshape(MOD_ROWS * N_MOD, 1, d)

        px, gx = _in_proj(xs, norm1_g[l], mod3, grp_x(tm_x), w_p, w_g, l, tm_x)
        pc, gc = _in_proj(hc, norm1_g[l], mod3, grp_c(tm_c), w_p, w_g, l, tm_c)

        filt = (hy_f_w1[l], hy_f_b1[l], hy_f_w2[l], hy_f_b2[l], hy_f_w3[l], hy_f_freq[l])
        hf, hb = _hyena_filter(seq, *filt)
        hre, him = _hyena_spectrum(hf, hb, cm_x, sm_x, ct_x, st_x, tc_spec)
        y_hy = _hyena_conv(px, seq, hy_conv_w[l], hre, him, hy_bias[l], cm_x, sm_x, tc)
        y_ml, yc_ml = _mlstm(px, pc, gx, gc, seq, seq_c, ml_col0, ml_conv_w[l], ml_gate_b[l], ml_norm_g[l],
                             ml_dh, need_ctx)
        y_rt, yc_rt = _retention(px, pc, seq, seq_c, rt_col0, rt_log_decay[l], cosf, sgnf,
                                 rt_dk, rt_dv, need_ctx)
        xs, h2 = _out_proj(xs, y_hy, y_ml, y_rt, wo, l, norm2_g[l], mod3, grp_x(tm_ox), tm_ox, alias=l > 0)
        if need_ctx:
            hfc, hbc = _hyena_filter(seq_c, *filt)
            hre_c, him_c = _hyena_spectrum(hfc, hbc, cm_c, sm_c, ct_c, st_c, tc_spec)
            yc_hy = _hyena_conv(pc, seq_c, hy_conv_w[l], hre_c, him_c, hy_bias[l], cm_c, sm_c, tc)
            hc, h2c = _out_proj(hc, yc_hy, yc_ml, yc_rt, wo, l, norm2_g[l], mod3, grp_c(tm_oc), tm_oc, alias=l > 0)
            hc = _ffn(hc, h2c, mod3, grp_c(tm_c), w1, w2, l, tm_c, 1024)
        xs = _ffn(xs, h2, mod3, grp_x(tm_x), w1, w2, l, tm_x, 1024)
    return _final_norm(xs, final_g, tm_ox).reshape(batch, seq, d)
```

```python
import functools
import math

import jax
import jax.numpy as jnp
import numpy as np
from jax import lax
from jax.experimental import pallas as pl
from jax.experimental.pallas import tpu as pltpu

F32 = jnp.float32
BF16 = jnp.bfloat16

GRID_W = 64
HY_EMB = 33
HY_FF = 64
HY_DECAY_TARGET = 1e-2
HY_SHORT_PCT = 0.3
HY_LONG_PCT = 1.5
N_HEADS = 4
N_GATES = 4 * N_HEADS
CHUNK = 128
ROPE_BASE = 10000.0
N_MOD = 6
EPS = 1e-6

LANE = 128
MXU_W = 256
WIN = 2 * LANE
PAIR_W = 3 * LANE
MOD_ROWS = 16
VMEM_LIMIT = 56 * 1024 * 1024


def _cparams(sem):
    return pltpu.CompilerParams(dimension_semantics=sem, vmem_limit_bytes=VMEM_LIMIT)


def _sigmoid(x):
    return 0.5 * jnp.tanh(0.5 * x) + 0.5


def _silu(x):
    return x * _sigmoid(x)


def _log_sigmoid(x):
    return jnp.minimum(x, 0.0) - jnp.log(1.0 + jnp.exp(-jnp.abs(x)))


def _dot(a, b):
    return jnp.dot(a, b, preferred_element_type=F32)


def _dot_nt(a, b):
    return lax.dot_general(a, b, (((1,), (1,)), ((), ())), preferred_element_type=F32)


def _dot_tn(a, b):
    return lax.dot_general(a, b, (((0,), (0,)), ((), ())), preferred_element_type=F32)


def _split_bf16(a):
    hi = a.astype(BF16)
    lo = (a - hi.astype(F32)).astype(BF16)
    return hi, lo


def _dot_split(a, b_f32):
    hi, lo = _split_bf16(b_f32)
    return _dot(a, hi) + _dot(a, lo)


def _resident(shape, index_map):
    return pl.BlockSpec(shape, index_map, pipeline_mode=pl.Buffered(1))


def _pick_tile(n, target):
    t = min(n, target)
    while n % t:
        t //= 2
    return t


def _norm_mod(x, g, shift, scale):
    ms = jnp.mean(x * x, axis=-1, keepdims=True)
    y = x * lax.rsqrt(ms + EPS) * g
    return y * (1.0 + scale) + shift


def _mod_kernel(s_ref, w_ref, b_ref, o_ref):
    s = _silu(s_ref[...]).astype(BF16)
    w = w_ref[0].astype(BF16)
    o_ref[0] = _dot(s, w) + b_ref[0]


def _modulation(s_in, w_mod, b_mod):
    depth, d, n = w_mod.shape
    tn = _pick_tile(n, 1024)
    return pl.pallas_call(
        _mod_kernel,
        out_shape=jax.ShapeDtypeStruct((depth, MOD_ROWS, n), F32),
        grid=(depth, n // tn),
        in_specs=[
            pl.BlockSpec((MOD_ROWS, d), lambda l, j: (0, 0)),
            pl.BlockSpec((1, d, tn), lambda l, j: (l, 0, j)),
            pl.BlockSpec((1, 1, tn), lambda l, j: (l, 0, j)),
        ],
        out_specs=pl.BlockSpec((1, MOD_ROWS, tn), lambda l, j: (l, 0, j)),
        compiler_params=_cparams(("arbitrary", "arbitrary")),
        name="modulation",
    )(s_in, w_mod, b_mod.reshape(depth, 1, n))


def _wproj_kernel(a_ref, b_ref, o_ref, *, split, n_in, shift):
    c = pl.program_id(1)
    wb = a_ref.shape[1]
    a = a_ref[...]
    lane = lax.broadcasted_iota(jnp.int32, (1, wb), 1)
    nxt = jnp.concatenate([b_ref[...]] * (wb // LANE), axis=1)
    moved = jnp.where(lane < wb - shift, pltpu.roll(a, wb - shift, axis=1), pltpu.roll(nxt, wb - shift, axis=1))
    moved = jnp.where(lane < n_in - shift - c * wb, moved, 0.0)
    o_ref[...] = jnp.where(c < split // wb, a, moved).astype(BF16)


def _proj_weights(w_in, split, shift, n_p):
    depth, d, n_in = w_in.shape
    wb = 4 * LANE
    assert split % wb == 0 and n_p % wb == 0 and shift < LANE
    last_a = -(-n_in // wb) - 1
    last_b = -(-n_in // LANE) - 1
    return pl.pallas_call(
        functools.partial(_wproj_kernel, split=split, n_in=n_in, shift=shift),
        out_shape=jax.ShapeDtypeStruct((depth, d, n_p), BF16),
        grid=(depth, n_p // wb),
        in_specs=[pl.BlockSpec((None, d, wb), lambda l, c: (l, 0, jnp.minimum(c, last_a))),
                  pl.BlockSpec((None, d, LANE), lambda l, c: (l, 0, jnp.minimum((c + 1) * (wb // LANE), last_b)))],
        out_specs=pl.BlockSpec((None, d, wb), lambda l, c: (l, 0, c)),
        compiler_params=_cparams(("arbitrary", "arbitrary")),
        name="proj_weights",
    )(w_in, w_in)


def _inproj_kernel(x_ref, g_ref, shift_ref, scale_ref, w_ref, wg_ref, p_ref, gate_ref, h_scr):
    @pl.when(pl.program_id(1) == 0)
    def _():
        h = _norm_mod(x_ref[...], g_ref[...], shift_ref[0], scale_ref[0]).astype(BF16)
        h_scr[...] = h
        gate_ref[...] = _dot(h, wg_ref[...])

    p_ref[...] = _dot(h_scr[...], w_ref[...]).astype(BF16)


def _in_proj(x2, norm_g, mod3, group_of_tile, w_p, w_g, layer, tm):
    m, d = x2.shape
    n = w_p.shape[2]
    tn = _pick_tile(n, 1024)
    return pl.pallas_call(
        _inproj_kernel,
        out_shape=(jax.ShapeDtypeStruct((m, n), BF16), jax.ShapeDtypeStruct((m, LANE), F32)),
        grid=(m // tm, n // tn),
        in_specs=[
            pl.BlockSpec((tm, d), lambda i, j: (i, 0)),
            pl.BlockSpec((1, d), lambda i, j: (0, 0)),
            pl.BlockSpec((1, 1, d), lambda i, j: (group_of_tile(i) * N_MOD + 0, 0, 0)),
            pl.BlockSpec((1, 1, d), lambda i, j: (group_of_tile(i) * N_MOD + 1, 0, 0)),
            pl.BlockSpec((None, d, tn), lambda i, j: (layer, 0, j)),
            pl.BlockSpec((None, d, LANE), lambda i, j: (layer, 0, 0)),
        ],
        out_specs=(pl.BlockSpec((tm, tn), lambda i, j: (i, j)),
                   pl.BlockSpec((tm, LANE), lambda i, j: (i, 0))),
        scratch_shapes=[pltpu.VMEM((tm, d), BF16)],
        compiler_params=_cparams(("arbitrary", "arbitrary")),
        name="in_proj",
    )(x2, norm_g.reshape(1, d), mod3, mod3, w_p, w_g)


def _outproj_kernel(x_ref, yh_ref, ym_ref, yr_ref, w_ref, gate_ref,
                    g2_ref, shift_ref, scale_ref, o_ref, h2_ref):
    kh, km = yh_ref.shape[1], ym_ref.shape[1]
    acc = _dot(yh_ref[...], w_ref[0:kh, :])
    acc = acc + _dot(ym_ref[...], w_ref[kh:kh + km, :])
    acc = acc + _dot(yr_ref[...], w_ref[kh + km:, :])
    xn = x_ref[...] + gate_ref[0] * acc
    o_ref[...] = xn
    h2_ref[...] = _norm_mod(xn, g2_ref[...], shift_ref[0], scale_ref[0]).astype(BF16)


def _out_proj(x2, y_hy, y_ml, y_rt, wo, layer, norm2_g, mod3, group_of_tile, tm, alias):
    m, d = x2.shape
    kh, km, kr = y_hy.shape[1], y_ml.shape[1], y_rt.shape[1]
    mrow = lambda k: pl.BlockSpec((1, 1, d), lambda i: (group_of_tile(i) * N_MOD + k, 0, 0))
    return pl.pallas_call(
        _outproj_kernel,
        out_shape=(jax.ShapeDtypeStruct((m, d), F32), jax.ShapeDtypeStruct((m, d), BF16)),
        grid=(m // tm,),
        in_specs=[
            pl.BlockSpec((tm, d), lambda i: (i, 0)),
            pl.BlockSpec((tm, kh), lambda i: (i, 0)),
            pl.BlockSpec((tm, km), lambda i: (i, 0)),
            pl.BlockSpec((tm, kr), lambda i: (i, 0)),
            _resident((None, kh + km + kr, d), lambda i: (layer, 0, 0)),
            mrow(2),
            pl.BlockSpec((1, d), lambda i: (0, 0)),
            mrow(3), mrow(4),
        ],
        out_specs=(pl.BlockSpec((tm, d), lambda i: (i, 0)), pl.BlockSpec((tm, d), lambda i: (i, 0))),
        input_output_aliases=({0: 0} if alias else {}),
        compiler_params=_cparams(("arbitrary",)),
        name="out_proj",
    )(x2, y_hy, y_ml, y_rt, wo, mod3, norm2_g.reshape(1, d), mod3, mod3)


def _ffn_kernel(h_ref, x_ref, gate_ref, w1_ref, w2_ref, o_ref, hid_scr, *, n1):
    j = pl.program_id(1)
    tf = w1_ref.shape[1]

    @pl.when(j < n1)
    def _():
        a = jnp.maximum(_dot(h_ref[...], w1_ref[...]), 0.0)
        hid_scr[jnp.minimum(j, n1 - 1)] = (a * a).astype(BF16)

    @pl.when(j >= n1)
    def _():
        acc = _dot(hid_scr[0], w2_ref[0:tf, :])
        for c in range(1, n1):
            acc = acc + _dot(hid_scr[c], w2_ref[c * tf:(c + 1) * tf, :])
        o_ref[...] = x_ref[...] + gate_ref[0] * acc


def _ffn(x2, h2, mod3, group_of_tile, w1, w2, layer, tm, tf):
    m, d = x2.shape
    f = w1.shape[2]
    n1 = f // tf
    tn = MXU_W
    n2 = d // tn
    col = lambda j: jnp.maximum(j - n1, 0)
    return pl.pallas_call(
        functools.partial(_ffn_kernel, n1=n1),
        out_shape=jax.ShapeDtypeStruct((m, d), F32),
        grid=(m // tm, n1 + n2),
        in_specs=[
            pl.BlockSpec((tm, d), lambda i, j: (i, 0)),
            pl.BlockSpec((tm, tn), lambda i, j: (i, col(j))),
            pl.BlockSpec((1, 1, tn), lambda i, j: (group_of_tile(i) * N_MOD + 5, 0, col(j))),
            pl.BlockSpec((None, d, tf), lambda i, j: (layer, 0, jnp.minimum(j, n1 - 1))),
            pl.BlockSpec((None, f, tn), lambda i, j: (layer, 0, col(j))),
        ],
        out_specs=pl.BlockSpec((tm, tn), lambda i, j: (i, col(j))),
        scratch_shapes=[pltpu.VMEM((n1, tm, tf), BF16)],
        compiler_params=_cparams(("arbitrary", "arbitrary")),
        name="ffn",
    )(h2, x2, mod3, w1, w2)


def _final_norm_kernel(x_ref, g_ref, o_ref):
    x = x_ref[...]
    ms = jnp.mean(x * x, axis=-1, keepdims=True)
    o_ref[...] = x * lax.rsqrt(ms + EPS) * g_ref[...]


def _final_norm(x2, g, tm):
    m, d = x2.shape
    return pl.pallas_call(
        _final_norm_kernel,
        out_shape=jax.ShapeDtypeStruct((m, d), F32),
        grid=(m // tm,),
        in_specs=[pl.BlockSpec((tm, d), lambda i: (i, 0)), pl.BlockSpec((1, d), lambda i: (0, 0))],
        out_specs=pl.BlockSpec((tm, d), lambda i: (i, 0)),
        compiler_params=_cparams(("arbitrary",)),
        name="final_norm",
    )(x2, g.reshape(1, d))


def _hy_filter_kernel(z_ref, t_ref, w1_ref, b1_ref, w2_ref, b2_ref, w3f_ref, w3b_ref, fr_ref, dl_ref,
                      hf_ref, hb_ref):
    hp = lax.Precision.HIGHEST
    dot = lambda a, b: jnp.dot(a, b, precision=hp, preferred_element_type=F32)
    fr = fr_ref[...]
    hdn = jnp.sin(fr * (dot(z_ref[...], w1_ref[...]) + b1_ref[...]))
    hdn = jnp.sin(fr * (dot(hdn, w2_ref[...]) + b2_ref[...]))
    win = jnp.exp(-t_ref[...] * dl_ref[...])
    hf = dot(hdn, w3f_ref[...]) * win
    hb = dot(hdn, w3b_ref[...]) * win
    row = lax.broadcasted_iota(jnp.int32, hb.shape, 0)
    hb = jnp.where(row == 0, 0.0, hb)
    inv = 1.0 / (jnp.sum(jnp.abs(hf), axis=0, keepdims=True) + jnp.sum(jnp.abs(hb), axis=0, keepdims=True))
    hf_ref[...] = hf * inv
    hb_ref[...] = hb * inv


@functools.lru_cache(maxsize=None)
def _filter_tables(seq, hy_w):
    t = np.linspace(0.0, 1.0, seq, dtype=np.float32)[:, None]
    w = (np.float32(2.0 * math.pi / seq) * np.arange(seq, dtype=np.float32))[:, None]
    nb = (HY_EMB - 1) // 2
    bands = np.linspace(1e-4, nb - 1, nb, dtype=np.float32)[None, :]
    z = np.concatenate([t, np.cos(bands * w), -np.sin(bands * w)], axis=-1).astype(np.float32)
    z = np.pad(z, ((0, 0), (0, LANE - HY_EMB)))
    deltas = np.abs(np.linspace(math.log(HY_DECAY_TARGET) / HY_LONG_PCT,
                                math.log(HY_DECAY_TARGET) / HY_SHORT_PCT, hy_w, dtype=np.float32))[None, :]
    return z, t, deltas


def _hyena_filter(seq, w1, b1, w2, b2, w3, freq):
    hy_w = w3.shape[1] // 2
    z, t, deltas = _filter_tables(seq, hy_w)
    pad_ff = LANE - HY_FF
    w1p = jnp.pad(w1, ((0, LANE - HY_EMB), (0, pad_ff)))
    w2p = jnp.pad(w2, ((0, pad_ff), (0, pad_ff)))
    w3p = jnp.pad(w3, ((0, pad_ff), (0, 0)))
    padv = lambda v: jnp.pad(v, (0, pad_ff)).reshape(1, LANE)
    out = jax.ShapeDtypeStruct((seq, hy_w), F32)
    return pl.pallas_call(
        _hy_filter_kernel,
        out_shape=(out, out),
        compiler_params=pltpu.CompilerParams(vmem_limit_bytes=VMEM_LIMIT),
        name="hyena_filter",
    )(z, t, w1p, padv(b1), w2p, padv(b2), w3p[:, :hy_w], w3p[:, hy_w:], padv(freq), deltas)


@functools.lru_cache(maxsize=None)
def _dft_tables(seq):
    k = np.arange(seq, dtype=np.int64)
    mm = ((2 * k[:, None] + 1) * (2 * k[None, :] + 1)) % (8 * seq)
    ang = mm.astype(np.float64) * (2.0 * math.pi / (8 * seq))
    theta = (k.astype(np.float64) + 0.5) * (math.pi / (2 * seq))
    return (np.cos(ang).astype(np.float32), np.sin(ang).astype(np.float32),
            np.cos(theta).astype(np.float32)[:, None], np.sin(theta).astype(np.float32)[:, None])


def _hy_spec_kernel(hf_ref, hb_ref, c_ref, s_ref, ct_ref, st_ref, hre_ref, him_ref):
    cm, sm = c_ref[...], s_ref[...]
    seq = cm.shape[0]
    tc = hf_ref.shape[1]
    taps = jnp.concatenate([hf_ref[...], hb_ref[...]], axis=1)
    ca, sa = _dot_split(cm, taps), _dot_split(sm, taps)
    a, a2 = ca[:, :tc], ca[:, tc:]
    b, b2 = sa[:, :tc], sa[:, tc:]
    ct, st = ct_ref[...], st_ref[...]
    scale = 1.0 / seq
    hre_ref[...] = ((a + a2) * ct + (b + b2) * st) * scale
    him_ref[...] = ((a - a2) * st - (b - b2) * ct) * scale


def _hyena_spectrum(hf, hb, cm, sm, ct, st, tc):
    seq, c = hf.shape
    out = jax.ShapeDtypeStruct((seq, c), F32)
    col = pl.BlockSpec((seq, tc), lambda j: (0, j))
    return pl.pallas_call(
        _hy_spec_kernel,
        out_shape=(out, out),
        grid=(c // tc,),
        in_specs=[col, col,
                  _resident((seq, seq), lambda j: (0, 0)), _resident((seq, seq), lambda j: (0, 0)),
                  pl.BlockSpec((seq, 1), lambda j: (0, 0)), pl.BlockSpec((seq, 1), lambda j: (0, 0))],
        out_specs=(col, col),
        compiler_params=_cparams(("arbitrary",)),
        name="hyena_spectrum",
    )(hf, hb, cm, sm, ct, st)


def _short_conv(p, w):
    n = p.shape[0]
    pz = jnp.concatenate([p, jnp.zeros((8, p.shape[1]), p.dtype)], axis=0)
    prev = pltpu.roll(pz, 1, axis=0)[:n]
    nxt = pltpu.roll(pz, n + 7, axis=0)[:n]
    return prev * w[0:1] + p * w[1:2] + nxt * w[2:3]


def _hy_conv_kernel(p0_ref, p1_ref, pv_ref, w0_ref, w1_ref, wv_ref, hre_ref, him_ref, bias_ref,
                    c_ref, s_ref, o_ref):
    x1 = _short_conv(p1_ref[...].astype(F32), w1_ref[...])
    v = _short_conv(pv_ref[...].astype(F32), wv_ref[...])
    u = v * x1
    ub = u.astype(BF16)
    cm, sm = c_ref[...], s_ref[...]
    zc = _dot(cm, ub)
    zs = _dot(sm, ub)
    hre, him = hre_ref[...], him_ref[...]
    yre = (hre * zc + him * zs).astype(BF16)
    yim = (him * zc - hre * zs).astype(BF16)
    y = _dot(cm, yre) - _dot(sm, yim)
    x0 = _short_conv(p0_ref[...].astype(F32), w0_ref[...])
    o_ref[...] = ((y + u * bias_ref[...]) * x0).astype(BF16)


def _hyena_conv(p, seq, conv_w, hre, him, bias, cm, sm, tc):
    m = p.shape[0]
    hy_w = hre.shape[1]
    nb = hy_w // tc
    pspec = lambda off: pl.BlockSpec((seq, tc), lambda j, b, off=off: (b, off * nb + j))
    wspec = lambda off: pl.BlockSpec((3, tc), lambda j, b, off=off: (0, off * nb + j))
    col = pl.BlockSpec((seq, tc), lambda j, b: (0, j))
    return pl.pallas_call(
        _hy_conv_kernel,
        out_shape=jax.ShapeDtypeStruct((m, hy_w), BF16),
        grid=(nb, m // seq),
        in_specs=[pspec(0), pspec(1), pspec(2), wspec(0), wspec(1), wspec(2), col, col,
                  pl.BlockSpec((1, tc), lambda j, b: (0, j)),
                  _resident((seq, seq), lambda j, b: (0, 0)), _resident((seq, seq), lambda j, b: (0, 0))],
        out_specs=pl.BlockSpec((seq, tc), lambda j, b: (b, j)),
        compiler_params=_cparams(("arbitrary", "arbitrary")),
        name="hyena_conv",
    )(p, p, p, conv_w, conv_w, conv_w, hre, him, bias.reshape(1, hy_w), cm, sm)


def _tri(n, upper):
    r = lax.broadcasted_iota(jnp.int32, (n, n), 0)
    c = lax.broadcasted_iota(jnp.int32, (n, n), 1)
    return (r <= c) if upper else (r >= c)


def _head_windows(head_dim):
    assert 2 * head_dim == PAIR_W
    lane = lax.broadcasted_iota(jnp.int32, (1, WIN), 1)
    return ((slice(0, WIN), lane < head_dim, head_dim),
            (slice(LANE, LANE + WIN), lane >= WIN - head_dim, 0))


def _pair_finish(acc_f, acc_b, rows, head_dim, mask_invalid):
    outs = []
    for j, (_, valid, _) in enumerate(_head_windows(head_dim)):
        h = acc_f[j, rows, :] + acc_b[j, rows, :]
        if mask_invalid:
            h = jnp.where(valid, h, 0.0)
        ms = jnp.sum(h * h, axis=-1, keepdims=True) * (1.0 / head_dim)
        outs.append(h * lax.rsqrt(ms + EPS))
    h0, h1 = outs
    return jnp.concatenate([h0[:, :LANE], h0[:, LANE:] + h1[:, :LANE], h1[:, LANE:]], axis=1)


def _row_blocks(n, blk, body):
    def step(i, carry):
        body(pl.ds(pl.multiple_of(i * blk, blk), blk))
        return carry

    lax.fori_loop(0, n // blk, step, 0)


BF16_ROWS = 16


def _conv_window(w, cw):
    n = w.shape[0]
    u = pltpu.roll(w, 1, axis=0) * cw[0:1] + w * cw[1:2] + pltpu.roll(w, n - 1, axis=0) * cw[2:3]
    return u[BF16_ROWS:n - BF16_ROWS]


def _fill_padded(pad_ref, src_ref):
    n = src_ref.shape[0]
    zero = jnp.zeros((BF16_ROWS, src_ref.shape[1]), src_ref.dtype)
    pad_ref[0:BF16_ROWS, :] = zero
    pad_ref[BF16_ROWS + n:2 * BF16_ROWS + n, :] = zero
    pad_ref[BF16_ROWS:BF16_ROWS + n, :] = src_ref[...]


def _scan_loops(nc_c, nc_x, run_c, run_x, unroll):
    def loop(nc, run):
        u = unroll if nc % unroll == 0 else 1

        def body(i, carry):
            for k in range(u):
                run(i * u + k, nc - 1 - (i * u + k))
            return carry

        lax.fori_loop(0, nc // u, body, 0)

    loop(nc_c, run_c)
    loop(nc_x, run_x)


@functools.lru_cache(maxsize=None)
def _gate_select_matrix():
    e = np.zeros((2 * LANE, 2 * LANE), np.float32)
    for pair in range(N_HEADS // 2):
        for j in range(2):
            for direction in range(2):
                icol = 2 * N_HEADS * direction + 2 * pair + j
                out = LANE * pair + 2 * j + direction
                e[icol, out] = 1.0
                e[LANE + icol + N_HEADS, out] = -1.0
    return e.astype(BF16)


def _ml_gate_kernel(gc_ref, gr_ref, gbc_ref, gbr_ref, esel_ref, rows_ref, cc_ref):
    nc, _, tt = gr_ref.shape
    tri_lo = _tri(tt, upper=False).astype(BF16)
    tri_up = _tri(tt, upper=True).astype(BF16)
    lane = lax.broadcasted_iota(jnp.int32, (1, LANE), 1)
    row = lax.broadcasted_iota(jnp.int32, (N_GATES, 1), 0)
    esel = esel_ref[...]
    for ci in range(nc):
        rs = slice(ci * tt, (ci + 1) * tt)
        gc = gc_ref[rs, :] + gbc_ref[...]
        gr = gr_ref[ci] + gbr_ref[...]
        ls = _log_sigmoid(gc)
        cum_c = jnp.where(lane >= 2 * N_HEADS, _dot_split(tri_up, ls), _dot_split(tri_lo, ls))
        hi, lo = _split_bf16(jnp.concatenate([gc, cum_c], axis=1))
        cc_ref[rs, :] = _dot(hi, esel) + _dot(lo, esel)
        hi, lo = _split_bf16(_log_sigmoid(gr))
        prefix = _dot(hi, tri_up) + _dot(lo, tri_up)
        suffix = _dot(hi, tri_lo) + _dot(lo, tri_lo)
        is_forget = (row % (2 * N_HEADS)) >= N_HEADS
        rows_ref[ci] = jnp.where(is_forget, jnp.where(row >= 2 * N_HEADS, suffix, prefix), gr)


def _mlstm_gates(g, seq, tt, gate_b):
    batch = g.shape[0] // seq
    nc = seq // tt
    grow = g[:, :N_GATES].reshape(-1, tt, N_GATES).transpose(0, 2, 1)
    gb = gate_b.astype(F32)
    return pl.pallas_call(
        _ml_gate_kernel,
        out_shape=(jax.ShapeDtypeStruct((batch * nc, N_GATES, tt), F32),
                   jax.ShapeDtypeStruct((batch * seq, 2 * LANE), F32)),
        grid=(batch,),
        in_specs=[pl.BlockSpec((seq, LANE), lambda b: (b, 0)),
                  pl.BlockSpec((nc, N_GATES, tt), lambda b: (b, 0, 0)),
                  pl.BlockSpec((1, LANE), lambda b: (0, 0)),
                  pl.BlockSpec((N_GATES, 1), lambda b: (0, 0)),
                  pl.BlockSpec((2 * LANE, 2 * LANE), lambda b: (0, 0))],
        out_specs=(pl.BlockSpec((nc, N_GATES, tt), lambda b: (b, 0, 0)),
                   pl.BlockSpec((seq, 2 * LANE), lambda b: (b, 0))),
        compiler_params=_cparams(("arbitrary",)),
        name="mlstm_gates",
    )(g, grow, jnp.pad(gb, (0, LANE - N_GATES)).reshape(1, LANE), gb.reshape(-1, 1), _gate_select_matrix())


def _mlstm_chunk(q, k, vt, ones_row, b_row, i_row, c_col, ct_ref, m_ref, reverse):
    tt = q.shape[0]
    mask = _tri(tt, upper=not reverse)
    end = 0 if reverse else tt - 1
    b_end = b_row[:, end:end + 1]
    dm = jnp.where(mask, c_col + b_row, -jnp.inf)
    dmax = jnp.max(dm, axis=0, keepdims=True)
    pt = (_dot_nt(k, q) * jnp.exp(dm - dmax)).astype(BF16)
    pv = _dot(vt, pt)
    g = b_end - b_row + i_row
    gmax = jnp.max(g, axis=1, keepdims=True)

    m_prev = m_ref[...]
    ct_prev = ct_ref[...]
    m_inter = b_row + m_prev
    m_t = jnp.maximum(m_inter, dmax)
    num = jnp.exp(dmax - m_t) * pv + jnp.exp(m_inter - m_t) * _dot_nt(ct_prev.astype(BF16), q)
    den = num[ones_row:ones_row + 1, :]
    ht = num * (1.0 / jnp.maximum(jnp.abs(den), jnp.exp(-m_t)))
    m_new = jnp.maximum(b_end + m_prev, gmax)
    vw = (vt.astype(F32) * jnp.exp(g - m_new)).astype(BF16)
    ct_ref[...] = jnp.exp(b_end + m_prev - m_new) * ct_prev + _dot(vw, k)
    m_ref[...] = m_new
    return ht


def _mlstm_kernel(qx_ref, kx_ref, vx_ref, ox_ref, rwx_ref, ccx_ref,
                  qc_ref, kc_ref, vc_ref, oc_ref, rwc_ref, ccc_ref,
                  wq_ref, wk_ref, ng_ref,
                  yx_ref, yc_ref,
                  qsx, ksx, vtx, qsc, ksc, vtc, hfx, hbx, hfc, hbc, qpx, kpx, qpc, kpc, ct_scr, m_scr,
                  *, head_dim, write_ctx, unroll):
    pair = pl.program_id(1)
    tt = rwx_ref.shape[2]
    kscale = head_dim ** -0.5
    wins = _head_windows(head_dim)
    pad_rows = WIN - head_dim
    ones_block = (lax.broadcasted_iota(jnp.int32, (pad_rows, tt), 0) == 0).astype(BF16)

    def prep(q_ref, k_ref, v_ref, qpad, kpad, qs, ks, vts):
        _fill_padded(qpad, q_ref)
        _fill_padded(kpad, k_ref)

        def block(i, carry):
            r0 = pl.multiple_of(i * tt, tt)
            rows, halo = pl.ds(r0, tt), pl.ds(r0, tt + 2 * BF16_ROWS)
            qs[rows, :] = _silu(_conv_window(qpad[halo, :].astype(F32), wq_ref[...])).astype(BF16)
            ka = _silu(_conv_window(kpad[halo, :].astype(F32), wk_ref[...])) * kscale
            for j, (win, valid, _) in enumerate(wins):
                ks[j, rows, :] = jnp.where(valid, ka[:, win], 0.0).astype(BF16)
                vt = v_ref[rows, win].T
                vts[j, i] = (jnp.concatenate([vt[:head_dim], ones_block], axis=0) if j == 0 else
                             jnp.concatenate([ones_block, vt[pad_rows:]], axis=0))
            return carry

        lax.fori_loop(0, q_ref.shape[0] // tt, block, 0)

    prep(qx_ref, kx_ref, vx_ref, qpx, kpx, qsx, ksx, vtx)
    prep(qc_ref, kc_ref, vc_ref, qpc, kpc, qsc, ksc, vtc)
    ct_scr[...] = jnp.zeros_like(ct_scr)
    m_scr[...] = jnp.zeros_like(m_scr)

    sel_r = lax.broadcasted_iota(jnp.int32, (N_GATES, 1), 0)
    pick_r = lambda a, c: jnp.sum(jnp.where(sel_r == c, a, 0.0), axis=0, keepdims=True)

    def run(ci_f, ci_b, q_s, k_s, v_t, rows_ref, cc_ref, hf, hb):
        for reverse, ci, hout in ((False, ci_f, hf), (True, ci_b, hb)):
            rows = pl.ds(pl.multiple_of(ci * tt, tt), tt)
            gates = rows_ref[ci]
            cc = cc_ref[rows, :]
            for j, (win, _, free) in enumerate(wins):
                irow = (2 * N_HEADS if reverse else 0) + pair * 2 + j
                lane_c = 2 * j + (1 if reverse else 0)
                chain = j * 2 + (1 if reverse else 0)
                ht = _mlstm_chunk(q_s[rows, win], k_s[j, rows, :], v_t[j, ci], free,
                                  pick_r(gates, irow + N_HEADS), pick_r(gates, irow),
                                  cc[:, lane_c:lane_c + 1], ct_scr.at[chain], m_scr.at[chain], reverse)
                hout[j, rows, :] = ht.T

    _scan_loops(qc_ref.shape[0] // tt, qx_ref.shape[0] // tt,
                lambda f, b: run(f, b, qsc, ksc, vtc, rwc_ref, ccc_ref, hfc, hbc),
                lambda f, b: run(f, b, qsx, ksx, vtx, rwx_ref, ccx_ref, hfx, hbx), unroll)

    def finish(hf, hb, o_ref, y_ref):
        def block(rows):
            hn = _pair_finish(hf, hb, rows, head_dim, mask_invalid=True)
            y_ref[rows, :] = (hn * ng_ref[...] * _sigmoid(o_ref[rows, :].astype(F32))).astype(BF16)

        _row_blocks(o_ref.shape[0], tt, block)

    finish(hfx, hbx, ox_ref, yx_ref)
    if write_ctx:
        finish(hfc, hbc, oc_ref, yc_ref)
    else:
        yc_ref[...] = jnp.zeros_like(yc_ref)


def _mlstm(px, pc, gx, gc, seq, seq_c, col0, conv_w, gate_b, norm_g, head_dim, write_ctx):
    batch = px.shape[0] // seq
    total = N_HEADS * head_dim
    npair = total // PAIR_W
    cb = col0 // PAIR_W
    tt = MXU_W if (seq % MXU_W == 0 and seq_c % MXU_W == 0) else CHUNK
    rows_x, cc_x = _mlstm_gates(gx, seq, tt, gate_b)
    rows_c, cc_c = _mlstm_gates(gc, seq_c, tt, gate_b)

    def pspecs(s):
        return [pl.BlockSpec((s, PAIR_W), lambda b, h, k=k: (b, cb + k * npair + h)) for k in range(4)]

    def gspecs(s):
        return [pl.BlockSpec((s // tt, N_GATES, tt), lambda b, h: (b, 0, 0)),
                pl.BlockSpec((s, LANE), lambda b, h: (b, h))]

    scr = lambda s, dt: pltpu.VMEM((2, s, WIN), dt)
    scr_t = lambda s: pltpu.VMEM((2, s // tt, WIN, tt), BF16)
    scr_q = lambda s: pltpu.VMEM((s, PAIR_W), BF16)
    scr_pad = lambda s: pltpu.VMEM((s + 2 * BF16_ROWS, PAIR_W), BF16)
    kern = functools.partial(_mlstm_kernel, head_dim=head_dim, write_ctx=write_ctx, unroll=2)
    return pl.pallas_call(
        kern,
        out_shape=(jax.ShapeDtypeStruct((batch * seq, total), BF16),
                   jax.ShapeDtypeStruct((batch * seq_c, total), BF16)),
        grid=(batch, npair),
        in_specs=(pspecs(seq) + gspecs(seq) + pspecs(seq_c) + gspecs(seq_c) + [
            pl.BlockSpec((3, PAIR_W), lambda b, h: (0, h)),
            pl.BlockSpec((3, PAIR_W), lambda b, h: (0, npair + h)),
            pl.BlockSpec((1, PAIR_W), lambda b, h: (0, h)),
        ]),
        out_specs=(pl.BlockSpec((seq, PAIR_W), lambda b, h: (b, h)),
                   pl.BlockSpec((seq_c, PAIR_W), lambda b, h: (b, h))),
        scratch_shapes=[scr_q(seq), scr(seq, BF16), scr_t(seq),
                        scr_q(seq_c), scr(seq_c, BF16), scr_t(seq_c),
                        scr(seq, F32), scr(seq, F32), scr(seq_c, F32), scr(seq_c, F32),
                        scr_pad(seq), scr_pad(seq), scr_pad(seq_c), scr_pad(seq_c),
                        pltpu.VMEM((4, WIN, WIN), F32),
                        pltpu.VMEM((4, 1, 1), F32)],
        compiler_params=_cparams(("arbitrary", "arbitrary")),
        name="mlstm",
    )(px, px, px, px, rows_x, cc_x, pc, pc, pc, pc, rows_c, cc_c,
      conv_w, conv_w, norm_g.reshape(1, total))


def _ret_chunk(q, k, v, dmask, q_decay, k_decay, c_decay, s_ref):
    pv = _dot((_dot_nt(q, k) * dmask).astype(BF16), v)
    kd = (k.astype(F32) * k_decay).astype(BF16)
    s_prev = s_ref[...]
    s_ref[...] = c_decay * s_prev + _dot_tn(kd, v)
    return pv + q_decay * _dot(q, s_prev.astype(BF16))


def _ret_kernel(ld_ref, qx_ref, kx_ref, vx_ref, gx_ref, qc_ref, kc_ref, vc_ref, gc_ref, cos_ref, sgn_ref,
                yx_ref, yc_ref,
                qsx, ksx, vsx, qsc, ksc, vsc, ofx, obx, ofc, obc, s_scr,
                *, key_dim, val_dim, write_ctx, unroll, t):
    pair = pl.program_id(1)
    qscale = key_dim ** -0.5
    half = key_dim // 2
    vwins = _head_windows(val_dim)
    lane = lax.broadcasted_iota(jnp.int32, (1, WIN), 1)
    klane = lax.broadcasted_iota(jnp.int32, (1, qx_ref.shape[1]), 1)
    first_half = (klane % key_dim) < half

    def rope(a, rows):
        width = a.shape[1]
        partner = jnp.where(first_half, pltpu.roll(a, width - half, axis=1), pltpu.roll(a, half, axis=1))
        return a * cos_ref[rows, :] + partner * sgn_ref[rows, :]

    def prep(q_ref, k_ref, v_ref, qs, ks, vs, rotary):
        def block(rows):
            qa, ka = q_ref[rows, :].astype(F32), k_ref[rows, :].astype(F32)
            if rotary:
                qa, ka = rope(qa, rows), rope(ka, rows)
            qs[rows, :] = (jnp.where(pair == 0, qa[:, :WIN], qa[:, LANE:LANE + WIN]) * qscale).astype(BF16)
            kw = jnp.where(pair == 0, ka[:, :WIN], ka[:, LANE:LANE + WIN])
            for j, (win, valid, _) in enumerate(vwins):
                start = (2 * pair + j) * key_dim - LANE * pair
                kvalid = jnp.logical_and(lane >= start, lane < start + key_dim)
                ks[j, rows, :] = jnp.where(kvalid, kw, 0.0).astype(BF16)
                vs[j, rows, :] = jnp.where(valid, v_ref[rows, win].astype(F32), 0.0).astype(BF16)

        _row_blocks(q_ref.shape[0], t, block)

    prep(qx_ref, kx_ref, vx_ref, qsx, ksx, vsx, True)
    prep(qc_ref, kc_ref, vc_ref, qsc, ksc, vsc, False)
    s_scr[...] = jnp.zeros_like(s_scr)

    r = lax.broadcasted_iota(jnp.int32, (t, t), 0)
    c = lax.broadcasted_iota(jnp.int32, (t, t), 1)
    idx = lax.broadcasted_iota(jnp.int32, (t, 1), 0).astype(F32)
    consts = []
    for j in range(2):
        for reverse in (False, True):
            lg = -jnp.exp(jnp.full((1, 1), ld_ref[1 if reverse else 0, pair * 2 + j], F32))
            rel = (c - r) if reverse else (r - c)
            dmask = jnp.where(rel >= 0, jnp.exp(lg * jnp.maximum(rel, 0).astype(F32)), 0.0)
            pos = (t - 1.0 - idx) if reverse else idx
            consts.append((dmask, jnp.exp(lg * (pos + 1.0)), jnp.exp(lg * (t - 1.0 - pos)), jnp.exp(lg * t)))

    def run(ci_f, ci_b, q_s, k_s, v_s, of, ob):
        for reverse, ci, oout in ((False, ci_f, of), (True, ci_b, ob)):
            rows = pl.ds(pl.multiple_of(ci * t, t), t)
            for j in range(2):
                chain = j * 2 + (1 if reverse else 0)
                dmask, q_decay, k_decay, c_decay = consts[chain]
                oout[j, rows, :] = _ret_chunk(q_s[rows, :], k_s[j, rows, :], v_s[j, rows, :],
                                              dmask, q_decay, k_decay, c_decay, s_scr.at[chain])

    _scan_loops(qc_ref.shape[0] // t, qx_ref.shape[0] // t,
                lambda f, b: run(f, b, qsc, ksc, vsc, ofc, obc),
                lambda f, b: run(f, b, qsx, ksx, vsx, ofx, obx), unroll)

    def finish(of, ob, g_ref, y_ref):
        def block(rows):
            on = _pair_finish(of, ob, rows, val_dim, mask_invalid=False)
            y_ref[rows, :] = (on * _silu(g_ref[rows, :].astype(F32))).astype(BF16)

        _row_blocks(g_ref.shape[0], t, block)

    finish(ofx, obx, gx_ref, yx_ref)
    if write_ctx:
        finish(ofc, obc, gc_ref, yc_ref)
    else:
        yc_ref[...] = jnp.zeros_like(yc_ref)


def _retention(px, pc, seq, seq_c, col0, log_decay, cosf, sgnf, key_dim, val_dim, write_ctx):
    batch = px.shape[0] // seq
    ktot, vtot = N_HEADS * key_dim, N_HEADS * val_dim
    assert ktot == PAIR_W
    npair = vtot // PAIR_W
    qb = col0 // PAIR_W
    vb = (col0 + 2 * ktot) // PAIR_W

    def pspecs(s):
        return [pl.BlockSpec((s, PAIR_W), lambda b, h: (b, qb)),
                pl.BlockSpec((s, PAIR_W), lambda b, h: (b, qb + 1)),
                pl.BlockSpec((s, PAIR_W), lambda b, h: (b, vb + h)),
                pl.BlockSpec((s, PAIR_W), lambda b, h: (b, vb + npair + h))]

    scr = lambda s, dt: pltpu.VMEM((2, s, WIN), dt)
    t = MXU_W if (seq % MXU_W == 0 and seq_c % MXU_W == 0) else CHUNK
    kern = functools.partial(_ret_kernel, key_dim=key_dim, val_dim=val_dim, write_ctx=write_ctx, unroll=2, t=t)
    return pl.pallas_call(
        kern,
        out_shape=(jax.ShapeDtypeStruct((batch * seq, vtot), BF16),
                   jax.ShapeDtypeStruct((batch * seq_c, vtot), BF16)),
        grid=(batch, npair),
        in_specs=([pl.BlockSpec(memory_space=pltpu.SMEM)] + pspecs(seq) + pspecs(seq_c) + [
            _resident((seq, PAIR_W), lambda b, h: (0, 0)),
            _resident((seq, PAIR_W), lambda b, h: (0, 0)),
        ]),
        out_specs=(pl.BlockSpec((seq, PAIR_W), lambda b, h: (b, h)),
                   pl.BlockSpec((seq_c, PAIR_W), lambda b, h: (b, h))),
        scratch_shapes=[pltpu.VMEM((seq, WIN), BF16), scr(seq, BF16), scr(seq, BF16),
                        pltpu.VMEM((seq_c, WIN), BF16), scr(seq_c, BF16), scr(seq_c, BF16),
                        scr(seq, F32), scr(seq, F32), scr(seq_c, F32), scr(seq_c, F32),
                        pltpu.VMEM((4, WIN, WIN), F32)],
        compiler_params=_cparams(("arbitrary", "arbitrary")),
        name="retention",
    )(log_decay.astype(F32), px, px, px, px, pc, pc, pc, pc, cosf, sgnf)


@functools.lru_cache(maxsize=None)
def _rope_tables(seq, key_dim):
    rows = seq // GRID_W
    r = np.repeat(np.arange(rows, dtype=np.float32), GRID_W)
    col = np.tile(np.arange(GRID_W, dtype=np.float32), rows)
    nf = key_dim // 4
    inv = (np.float32(ROPE_BASE) ** (-np.arange(nf, dtype=np.float32) / np.float32(nf))).astype(np.float32)
    ang = np.concatenate([r[:, None] * inv, col[:, None] * inv], axis=-1).astype(np.float32)
    cos, sin = np.cos(ang).astype(np.float32), np.sin(ang).astype(np.float32)
    return np.tile(np.concatenate([cos, cos], axis=1), (1, N_HEADS)), np.tile(np.concatenate([-sin, sin], axis=1), (1, N_HEADS))


def kernel(x, c, ctx, c_ctx, norm1_g, norm2_g, w_mod, b_mod, w_in, hy_conv_w, hy_f_w1, hy_f_b1, hy_f_w2, hy_f_b2, hy_f_w3, hy_f_freq, hy_bias, ml_conv_w, ml_gate_b, ml_norm_g, rt_log_decay, w_out, w_ff1, w_ff2, final_g):
    batch, seq, d = x.shape
    seq_c = ctx.shape[1]
    depth = w_mod.shape[0]
    hy_w = d // 4
    ml_w = 3 * d // 8
    rt_w = 3 * d // 8
    ml_dh = ml_w // N_HEADS
    rt_dv = rt_w // N_HEADS
    rt_dk = rt_dv // 2
    hy_cols, ml_cols = 3 * hy_w, 4 * ml_w
    n_in = w_in.shape[2]
    gate0 = hy_cols + ml_cols
    rt0 = gate0 + N_GATES
    n_p = -(-n_in // 1024) * 1024

    s_in = jnp.concatenate([c, c_ctx[None], jnp.zeros((MOD_ROWS - batch - 1, d), F32)], axis=0)
    mod = _modulation(s_in, w_mod, b_mod)

    grp_x = lambda tm: (lambda i: (i * tm) // seq)
    grp_c = lambda tm: (lambda i: batch + 0 * i)
    tm_x, tm_c = _pick_tile(seq, 1024), _pick_tile(batch * seq_c, 1024)
    tm_ox, tm_oc = _pick_tile(seq, 512), _pick_tile(batch * seq_c, 512)

    def dft(n):
        cm, sm, ct, st = _dft_tables(n)
        return jnp.asarray(cm).astype(BF16), jnp.asarray(sm).astype(BF16), ct, st

    cm_x, sm_x, ct_x, st_x = dft(seq)
    cm_c, sm_c, ct_c, st_c = dft(seq_c)
    cosf, sgnf = _rope_tables(seq, rt_dk)
    tc = _pick_tile(hy_w, MXU_W)
    tc_spec = _pick_tile(hy_w, LANE)

    xs = x.reshape(batch * seq, d)
    hc = ctx.reshape(batch * seq_c, d)
    w_p = _proj_weights(w_in, gate0, N_GATES, n_p)
    w_g = jnp.pad(w_in[:, :, gate0:rt0], ((0, 0), (0, 0), (0, LANE - N_GATES))).astype(BF16)
    ml_col0, rt_col0 = hy_cols, gate0
    wo = w_out.astype(BF16)
    w1 = w_ff1.astype(BF16)
    w2 = w_ff2.astype(BF16)

    for l in range(depth):
        need_ctx = l < depth - 1
        mod3 = mod[l].reshape(MOD_ROWS * N_MOD, 1, d)

        px, gx = _in_proj(xs, norm1_g[l], mod3, grp_x(tm_x), w_p, w_g, l, tm_x)
        pc, gc = _in_proj(hc, norm1_g[l], mod3, grp_c(tm_c), w_p, w_g, l, tm_c)

        filt = (hy_f_w1[l], hy_f_b1[l], hy_f_w2[l], hy_f_b2[l], hy_f_w3[l], hy_f_freq[l])
        hf, hb = _hyena_filter(seq, *filt)
        hre, him = _hyena_spectrum(hf, hb, cm_x, sm_x, ct_x, st_x, tc_spec)
        y_hy = _hyena_conv(px, seq, hy_conv_w[l], hre, him, hy_bias[l], cm_x, sm_x, tc)
        y_ml, yc_ml = _mlstm(px, pc, gx, gc, seq, seq_c, ml_col0, ml_conv_w[l], ml_gate_b[l], ml_norm_g[l],
                             ml_dh, need_ctx)
        y_rt, yc_rt = _retention(px, pc, seq, seq_c, rt_col0, rt_log_decay[l], cosf, sgnf,
                                 rt_dk, rt_dv, need_ctx)
        xs, h2 = _out_proj(xs, y_hy, y_ml, y_rt, wo, l, norm2_g[l], mod3, grp_x(tm_ox), tm_ox, alias=l > 0)
        if need_ctx:
            hfc, hbc = _hyena_filter(seq_c, *filt)
            hre_c, him_c = _hyena_spectrum(hfc, hbc, cm_c, sm_c, ct_c, st_c, tc_spec)
            yc_hy = _hyena_conv(pc, seq_c, hy_conv_w[l], hre_c, him_c, hy_bias[l], cm_c, sm_c, tc)
            hc, h2c = _out_proj(hc, yc_hy, yc_ml, yc_rt, wo, l, norm2_g[l], mod3, grp_c(tm_oc), tm_oc, alias=l > 0)
            hc = _ffn(hc, h2c, mod3, grp_c(tm_c), w1, w2, l, tm_c, 1024)
        xs = _ffn(xs, h2, mod3, grp_x(tm_x), w1, w2, l, tm_x, 1024)
    return _final_norm(xs, final_g, tm_ox).reshape(batch, seq, d)
```

```python
import functools
import math

import jax
import jax.numpy as jnp
import numpy as np
from jax import lax
from jax.experimental import pallas as pl
from jax.experimental.pallas import tpu as pltpu

F32 = jnp.float32
BF16 = jnp.bfloat16

GRID_W = 64
HY_EMB = 33
HY_FF = 64
HY_DECAY_TARGET = 1e-2
HY_SHORT_PCT = 0.3
HY_LONG_PCT = 1.5
N_HEADS = 4
N_GATES = 4 * N_HEADS
CHUNK = 128
ROPE_BASE = 10000.0
N_MOD = 6
EPS = 1e-6

LANE = 128
MXU_W = 256
WIN = 2 * LANE
PAIR_W = 3 * LANE
MOD_ROWS = 16
VMEM_LIMIT = 56 * 1024 * 1024


def _cparams(sem):
    return pltpu.CompilerParams(dimension_semantics=sem, vmem_limit_bytes=VMEM_LIMIT)


def _sigmoid(x):
    return 0.5 * jnp.tanh(0.5 * x) + 0.5


def _silu(x):
    return x * _sigmoid(x)


def _log_sigmoid(x):
    return jnp.minimum(x, 0.0) - jnp.log(1.0 + jnp.exp(-jnp.abs(x)))


def _dot(a, b):
    return jnp.dot(a, b, preferred_element_type=F32)


def _dot_nt(a, b):
    return lax.dot_general(a, b, (((1,), (1,)), ((), ())), preferred_element_type=F32)


def _dot_tn(a, b):
    return lax.dot_general(a, b, (((0,), (0,)), ((), ())), preferred_element_type=F32)


def _split_bf16(a):
    hi = a.astype(BF16)
    lo = (a - hi.astype(F32)).astype(BF16)
    return hi, lo


def _dot_split(a, b_f32):
    hi, lo = _split_bf16(b_f32)
    return _dot(a, hi) + _dot(a, lo)


def _resident(shape, index_map):
    return pl.BlockSpec(shape, index_map, pipeline_mode=pl.Buffered(1))


def _pick_tile(n, target):
    t = min(n, target)
    while n % t:
        t //= 2
    return t


def _norm_mod(x, g, shift, scale):
    ms = jnp.mean(x * x, axis=-1, keepdims=True)
    y = x * lax.rsqrt(ms + EPS) * g
    return y * (1.0 + scale) + shift


def _mod_kernel(s_ref, w_ref, b_ref, o_ref):
    s = _silu(s_ref[...]).astype(BF16)
    w = w_ref[0].astype(BF16)
    o_ref[0] = _dot(s, w) + b_ref[0]


def _modulation(s_in, w_mod, b_mod):
    depth, d, n = w_mod.shape
    tn = _pick_tile(n, 1024)
    return pl.pallas_call(
        _mod_kernel,
        out_shape=jax.ShapeDtypeStruct((depth, MOD_ROWS, n), F32),
        grid=(depth, n // tn),
        in_specs=[
            pl.BlockSpec((MOD_ROWS, d), lambda l, j: (0, 0)),
            pl.BlockSpec((1, d, tn), lambda l, j: (l, 0, j)),
            pl.BlockSpec((1, 1, tn), lambda l, j: (l, 0, j)),
        ],
        out_specs=pl.BlockSpec((1, MOD_ROWS, tn), lambda l, j: (l, 0, j)),
        compiler_params=_cparams(("arbitrary", "arbitrary")),
        name="modulation",
    )(s_in, w_mod, b_mod.reshape(depth, 1, n))


def _inproj_kernel(x_ref, g_ref, shift_ref, scale_ref, wa_ref, wb_ref, wg_ref, p_ref, gate_ref, h_scr, *, na):
    j = pl.program_id(1)

    @pl.when(j == 0)
    def _():
        h = _norm_mod(x_ref[...], g_ref[...], shift_ref[0], scale_ref[0]).astype(BF16)
        h_scr[...] = h
        gate_ref[...] = _dot(h, wg_ref[...])

    @pl.when(j < na)
    def _():
        p_ref[...] = _dot(h_scr[...], wa_ref[...]).astype(BF16)

    @pl.when(j >= na)
    def _():
        p_ref[...] = _dot(h_scr[...], wb_ref[...]).astype(BF16)


def _in_proj(x2, norm_g, mod3, group_of_tile, w_a, w_b, w_g, layer, tm):
    m, d = x2.shape
    n_a, n_b = w_a.shape[2], w_b.shape[2]
    tn = 3 * MXU_W
    assert n_a % tn == 0 and n_b % tn == 0
    na, nb = n_a // tn, n_b // tn
    return pl.pallas_call(
        functools.partial(_inproj_kernel, na=na),
        out_shape=(jax.ShapeDtypeStruct((m, n_a + n_b), BF16), jax.ShapeDtypeStruct((m, LANE), F32)),
        grid=(m // tm, na + nb),
        in_specs=[
            pl.BlockSpec((tm, d), lambda i, j: (i, 0)),
            pl.BlockSpec((1, d), lambda i, j: (0, 0)),
            pl.BlockSpec((1, 1, d), lambda i, j: (group_of_tile(i) * N_MOD + 0, 0, 0)),
            pl.BlockSpec((1, 1, d), lambda i, j: (group_of_tile(i) * N_MOD + 1, 0, 0)),
            pl.BlockSpec((None, d, tn), lambda i, j: (layer, 0, jnp.minimum(j, na - 1))),
            pl.BlockSpec((None, d, tn), lambda i, j: (layer, 0, jnp.maximum(j - na, 0))),
            pl.BlockSpec((None, d, LANE), lambda i, j: (layer, 0, 0)),
        ],
        out_specs=(pl.BlockSpec((tm, tn), lambda i, j: (i, j)),
                   pl.BlockSpec((tm, LANE), lambda i, j: (i, 0))),
        scratch_shapes=[pltpu.VMEM((tm, d), BF16)],
        compiler_params=_cparams(("arbitrary", "arbitrary")),
        name="in_proj",
    )(x2, norm_g.reshape(1, d), mod3, mod3, w_a, w_b, w_g)


def _outproj_kernel(x_ref, yh_ref, ym_ref, yr_ref, w_ref, gate_ref,
                    g2_ref, shift_ref, scale_ref, o_ref, h2_ref):
    kh, km = yh_ref.shape[1], ym_ref.shape[1]
    acc = _dot(yh_ref[...], w_ref[0:kh, :])
    acc = acc + _dot(ym_ref[...], w_ref[kh:kh + km, :])
    acc = acc + _dot(yr_ref[...], w_ref[kh + km:, :])
    xn = x_ref[...] + gate_ref[0] * acc
    o_ref[...] = xn
    h2_ref[...] = _norm_mod(xn, g2_ref[...], shift_ref[0], scale_ref[0]).astype(BF16)


def _out_proj(x2, y_hy, y_ml, y_rt, wo, layer, norm2_g, mod3, group_of_tile, tm, alias):
    m, d = x2.shape
    kh, km, kr = y_hy.shape[1], y_ml.shape[1], y_rt.shape[1]
    mrow = lambda k: pl.BlockSpec((1, 1, d), lambda i: (group_of_tile(i) * N_MOD + k, 0, 0))
    return pl.pallas_call(
        _outproj_kernel,
        out_shape=(jax.ShapeDtypeStruct((m, d), F32), jax.ShapeDtypeStruct((m, d), BF16)),
        grid=(m // tm,),
        in_specs=[
            pl.BlockSpec((tm, d), lambda i: (i, 0)),
            pl.BlockSpec((tm, kh), lambda i: (i, 0)),
            pl.BlockSpec((tm, km), lambda i: (i, 0)),
            pl.BlockSpec((tm, kr), lambda i: (i, 0)),
            _resident((None, kh + km + kr, d), lambda i: (layer, 0, 0)),
            mrow(2),
            pl.BlockSpec((1, d), lambda i: (0, 0)),
            mrow(3), mrow(4),
        ],
        out_specs=(pl.BlockSpec((tm, d), lambda i: (i, 0)), pl.BlockSpec((tm, d), lambda i: (i, 0))),
        input_output_aliases=({0: 0} if alias else {}),
        compiler_params=_cparams(("arbitrary",)),
        name="out_proj",
    )(x2, y_hy, y_ml, y_rt, wo, mod3, norm2_g.reshape(1, d), mod3, mod3)


def _ffn_kernel(h_ref, x_ref, gate_ref, w1_ref, w2_ref, o_ref, hid_scr, *, n1):
    j = pl.program_id(1)
    tf = w1_ref.shape[1]

    @pl.when(j < n1)
    def _():
        a = jnp.maximum(_dot(h_ref[...], w1_ref[...]), 0.0)
        hid_scr[jnp.minimum(j, n1 - 1)] = (a * a).astype(BF16)

    @pl.when(j >= n1)
    def _():
        acc = _dot(hid_scr[0], w2_ref[0:tf, :])
        for c in range(1, n1):
            acc = acc + _dot(hid_scr[c], w2_ref[c * tf:(c + 1) * tf, :])
        o_ref[...] = x_ref[...] + gate_ref[0] * acc


def _ffn(x2, h2, mod3, group_of_tile, w1, w2, layer, tm, tf):
    m, d = x2.shape
    f = w1.shape[2]
    n1 = f // tf
    tn = MXU_W
    n2 = d // tn
    col = lambda j: jnp.maximum(j - n1, 0)
    return pl.pallas_call(
        functools.partial(_ffn_kernel, n1=n1),
        out_shape=jax.ShapeDtypeStruct((m, d), F32),
        grid=(m // tm, n1 + n2),
        in_specs=[
            pl.BlockSpec((tm, d), lambda i, j: (i, 0)),
            pl.BlockSpec((tm, tn), lambda i, j: (i, col(j))),
            pl.BlockSpec((1, 1, tn), lambda i, j: (group_of_tile(i) * N_MOD + 5, 0, col(j))),
            pl.BlockSpec((None, d, tf), lambda i, j: (layer, 0, jnp.minimum(j, n1 - 1))),
            pl.BlockSpec((None, f, tn), lambda i, j: (layer, 0, col(j))),
        ],
        out_specs=pl.BlockSpec((tm, tn), lambda i, j: (i, col(j))),
        scratch_shapes=[pltpu.VMEM((n1, tm, tf), BF16)],
        compiler_params=_cparams(("arbitrary", "arbitrary")),
        name="ffn",
    )(h2, x2, mod3, w1, w2)


def _final_norm_kernel(x_ref, g_ref, o_ref):
    x = x_ref[...]
    ms = jnp.mean(x * x, axis=-1, keepdims=True)
    o_ref[...] = x * lax.rsqrt(ms + EPS) * g_ref[...]


def _final_norm(x2, g, tm):
    m, d = x2.shape
    return pl.pallas_call(
        _final_norm_kernel,
        out_shape=jax.ShapeDtypeStruct((m, d), F32),
        grid=(m // tm,),
        in_specs=[pl.BlockSpec((tm, d), lambda i: (i, 0)), pl.BlockSpec((1, d), lambda i: (0, 0))],
        out_specs=pl.BlockSpec((tm, d), lambda i: (i, 0)),
        compiler_params=_cparams(("arbitrary",)),
        name="final_norm",
    )(x2, g.reshape(1, d))


def _hy_filter_kernel(z_ref, t_ref, w1_ref, b1_ref, w2_ref, b2_ref, w3f_ref, w3b_ref, fr_ref, dl_ref,
                      hf_ref, hb_ref):
    hp = lax.Precision.HIGHEST
    dot = lambda a, b: jnp.dot(a, b, precision=hp, preferred_element_type=F32)
    fr = fr_ref[...]
    hdn = jnp.sin(fr * (dot(z_ref[...], w1_ref[...]) + b1_ref[...]))
    hdn = jnp.sin(fr * (dot(hdn, w2_ref[...]) + b2_ref[...]))
    win = jnp.exp(-t_ref[...] * dl_ref[...])
    hf = dot(hdn, w3f_ref[...]) * win
    hb = dot(hdn, w3b_ref[...]) * win
    row = lax.broadcasted_iota(jnp.int32, hb.shape, 0)
    hb = jnp.where(row == 0, 0.0, hb)
    inv = 1.0 / (jnp.sum(jnp.abs(hf), axis=0, keepdims=True) + jnp.sum(jnp.abs(hb), axis=0, keepdims=True))
    hf_ref[...] = hf * inv
    hb_ref[...] = hb * inv


@functools.lru_cache(maxsize=None)
def _filter_tables(seq, hy_w):
    t = np.linspace(0.0, 1.0, seq, dtype=np.float32)[:, None]
    w = (np.float32(2.0 * math.pi / seq) * np.arange(seq, dtype=np.float32))[:, None]
    nb = (HY_EMB - 1) // 2
    bands = np.linspace(1e-4, nb - 1, nb, dtype=np.float32)[None, :]
    z = np.concatenate([t, np.cos(bands * w), -np.sin(bands * w)], axis=-1).astype(np.float32)
    z = np.pad(z, ((0, 0), (0, LANE - HY_EMB)))
    deltas = np.abs(np.linspace(math.log(HY_DECAY_TARGET) / HY_LONG_PCT,
                                math.log(HY_DECAY_TARGET) / HY_SHORT_PCT, hy_w, dtype=np.float32))[None, :]
    return z, t, deltas


def _hyena_filter(seq, w1, b1, w2, b2, w3, freq):
    hy_w = w3.shape[1] // 2
    z, t, deltas = _filter_tables(seq, hy_w)
    pad_ff = LANE - HY_FF
    w1p = jnp.pad(w1, ((0, LANE - HY_EMB), (0, pad_ff)))
    w2p = jnp.pad(w2, ((0, pad_ff), (0, pad_ff)))
    w3p = jnp.pad(w3, ((0, pad_ff), (0, 0)))
    padv = lambda v: jnp.pad(v, (0, pad_ff)).reshape(1, LANE)
    out = jax.ShapeDtypeStruct((seq, hy_w), F32)
    return pl.pallas_call(
        _hy_filter_kernel,
        out_shape=(out, out),
        compiler_params=pltpu.CompilerParams(vmem_limit_bytes=VMEM_LIMIT),
        name="hyena_filter",
    )(z, t, w1p, padv(b1), w2p, padv(b2), w3p[:, :hy_w], w3p[:, hy_w:], padv(freq), deltas)


@functools.lru_cache(maxsize=None)
def _dft_tables(seq):
    k = np.arange(seq, dtype=np.int64)
    mm = ((2 * k[:, None] + 1) * (2 * k[None, :] + 1)) % (8 * seq)
    ang = mm.astype(np.float64) * (2.0 * math.pi / (8 * seq))
    theta = (k.astype(np.float64) + 0.5) * (math.pi / (2 * seq))
    return (np.cos(ang).astype(np.float32), np.sin(ang).astype(np.float32),
            np.cos(theta).astype(np.float32)[:, None], np.sin(theta).astype(np.float32)[:, None])


def _hy_spec_kernel(hf_ref, hb_ref, c_ref, s_ref, ct_ref, st_ref, hre_ref, him_ref):
    cm, sm = c_ref[...], s_ref[...]
    seq = cm.shape[0]
    tc = hf_ref.shape[1]
    taps = jnp.concatenate([hf_ref[...], hb_ref[...]], axis=1)
    ca, sa = _dot_split(cm, taps), _dot_split(sm, taps)
    a, a2 = ca[:, :tc], ca[:, tc:]
    b, b2 = sa[:, :tc], sa[:, tc:]
    ct, st = ct_ref[...], st_ref[...]
    scale = 1.0 / seq
    hre_ref[...] = ((a + a2) * ct + (b + b2) * st) * scale
    him_ref[...] = ((a - a2) * st - (b - b2) * ct) * scale


def _hyena_spectrum(hf, hb, cm, sm, ct, st, tc):
    seq, c = hf.shape
    out = jax.ShapeDtypeStruct((seq, c), F32)
    col = pl.BlockSpec((seq, tc), lambda j: (0, j))
    return pl.pallas_call(
        _hy_spec_kernel,
        out_shape=(out, out),
        grid=(c // tc,),
        in_specs=[col, col,
                  _resident((seq, seq), lambda j: (0, 0)), _resident((seq, seq), lambda j: (0, 0)),
                  pl.BlockSpec((seq, 1), lambda j: (0, 0)), pl.BlockSpec((seq, 1), lambda j: (0, 0))],
        out_specs=(col, col),
        compiler_params=_cparams(("arbitrary",)),
        name="hyena_spectrum",
    )(hf, hb, cm, sm, ct, st)


def _short_conv(p, w):
    n = p.shape[0]
    pz = jnp.concatenate([p, jnp.zeros((8, p.shape[1]), p.dtype)], axis=0)
    prev = pltpu.roll(pz, 1, axis=0)[:n]
    nxt = pltpu.roll(pz, n + 7, axis=0)[:n]
    return prev * w[0:1] + p * w[1:2] + nxt * w[2:3]


def _hy_conv_kernel(p0_ref, p1_ref, pv_ref, w0_ref, w1_ref, wv_ref, hre_ref, him_ref, bias_ref,
                    c_ref, s_ref, o_ref):
    x1 = _short_conv(p1_ref[...].astype(F32), w1_ref[...])
    v = _short_conv(pv_ref[...].astype(F32), wv_ref[...])
    u = v * x1
    ub = u.astype(BF16)
    cm, sm = c_ref[...], s_ref[...]
    zc = _dot(cm, ub)
    zs = _dot(sm, ub)
    hre, him = hre_ref[...], him_ref[...]
    yre = (hre * zc + him * zs).astype(BF16)
    yim = (him * zc - hre * zs).astype(BF16)
    y = _dot(cm, yre) - _dot(sm, yim)
    x0 = _short_conv(p0_ref[...].astype(F32), w0_ref[...])
    o_ref[...] = ((y + u * bias_ref[...]) * x0).astype(BF16)


def _hyena_conv(p, seq, conv_w, hre, him, bias, cm, sm, tc):
    m = p.shape[0]
    hy_w = hre.shape[1]
    nb = hy_w // tc
    pspec = lambda off: pl.BlockSpec((seq, tc), lambda j, b, off=off: (b, off * nb + j))
    wspec = lambda off: pl.BlockSpec((3, tc), lambda j, b, off=off: (0, off * nb + j))
    col = pl.BlockSpec((seq, tc), lambda j, b: (0, j))
    return pl.pallas_call(
        _hy_conv_kernel,
        out_shape=jax.ShapeDtypeStruct((m, hy_w), BF16),
        grid=(nb, m // seq),
        in_specs=[pspec(0), pspec(1), pspec(2), wspec(0), wspec(1), wspec(2), col, col,
                  pl.BlockSpec((1, tc), lambda j, b: (0, j)),
                  _resident((seq, seq), lambda j, b: (0, 0)), _resident((seq, seq), lambda j, b: (0, 0))],
        out_specs=pl.BlockSpec((seq, tc), lambda j, b: (b, j)),
        compiler_params=_cparams(("arbitrary", "arbitrary")),
        name="hyena_conv",
    )(p, p, p, conv_w, conv_w, conv_w, hre, him, bias.reshape(1, hy_w), cm, sm)


def _tri(n, upper):
    r = lax.broadcasted_iota(jnp.int32, (n, n), 0)
    c = lax.broadcasted_iota(jnp.int32, (n, n), 1)
    return (r <= c) if upper else (r >= c)


def _head_windows(head_dim):
    assert 2 * head_dim == PAIR_W
    lane = lax.broadcasted_iota(jnp.int32, (1, WIN), 1)
    return ((slice(0, WIN), lane < head_dim, head_dim),
            (slice(LANE, LANE + WIN), lane >= WIN - head_dim, 0))


def _pair_finish(acc_f, acc_b, rows, head_dim, mask_invalid):
    outs = []
    for j, (_, valid, _) in enumerate(_head_windows(head_dim)):
        h = acc_f[j, rows, :] + acc_b[j, rows, :]
        if mask_invalid:
            h = jnp.where(valid, h, 0.0)
        ms = jnp.sum(h * h, axis=-1, keepdims=True) * (1.0 / head_dim)
        outs.append(h * lax.rsqrt(ms + EPS))
    h0, h1 = outs
    return jnp.concatenate([h0[:, :LANE], h0[:, LANE:] + h1[:, :LANE], h1[:, LANE:]], axis=1)


def _row_blocks(n, blk, body):
    def step(i, carry):
        body(pl.ds(pl.multiple_of(i * blk, blk), blk))
        return carry

    lax.fori_loop(0, n // blk, step, 0)


BF16_ROWS = 16


def _conv_window(w, cw):
    n = w.shape[0]
    u = pltpu.roll(w, 1, axis=0) * cw[0:1] + w * cw[1:2] + pltpu.roll(w, n - 1, axis=0) * cw[2:3]
    return u[BF16_ROWS:n - BF16_ROWS]


def _fill_padded(pad_ref, src_ref):
    n = src_ref.shape[0]
    zero = jnp.zeros((BF16_ROWS, src_ref.shape[1]), src_ref.dtype)
    pad_ref[0:BF16_ROWS, :] = zero
    pad_ref[BF16_ROWS + n:2 * BF16_ROWS + n, :] = zero
    pad_ref[BF16_ROWS:BF16_ROWS + n, :] = src_ref[...]


def _scan_loops(nc_c, nc_x, run_c, run_x, unroll):
    def loop(nc, run):
        u = unroll if nc % unroll == 0 else 1

        def body(i, carry):
            for k in range(u):
                run(i * u + k, nc - 1 - (i * u + k))
            return carry

        lax.fori_loop(0, nc // u, body, 0)

    loop(nc_c, run_c)
    loop(nc_x, run_x)


@functools.lru_cache(maxsize=None)
def _gate_select_matrix():
    e = np.zeros((2 * LANE, 2 * LANE), np.float32)
    for pair in range(N_HEADS // 2):
        for j in range(2):
            for direction in range(2):
                icol = 2 * N_HEADS * direction + 2 * pair + j
                out = LANE * pair + 2 * j + direction
                e[icol, out] = 1.0
                e[LANE + icol + N_HEADS, out] = -1.0
    return e.astype(BF16)


def _ml_gate_kernel(gc_ref, gr_ref, gbc_ref, gbr_ref, esel_ref, rows_ref, cc_ref):
    nc, _, tt = gr_ref.shape
    tri_lo = _tri(tt, upper=False).astype(BF16)
    tri_up = _tri(tt, upper=True).astype(BF16)
    lane = lax.broadcasted_iota(jnp.int32, (1, LANE), 1)
    row = lax.broadcasted_iota(jnp.int32, (N_GATES, 1), 0)
    esel = esel_ref[...]
    for ci in range(nc):
        rs = slice(ci * tt, (ci + 1) * tt)
        gc = gc_ref[rs, :] + gbc_ref[...]
        gr = gr_ref[ci] + gbr_ref[...]
        ls = _log_sigmoid(gc)
        cum_c = jnp.where(lane >= 2 * N_HEADS, _dot_split(tri_up, ls), _dot_split(tri_lo, ls))
        hi, lo = _split_bf16(jnp.concatenate([gc, cum_c], axis=1))
        cc_ref[rs, :] = _dot(hi, esel) + _dot(lo, esel)
        hi, lo = _split_bf16(_log_sigmoid(gr))
        prefix = _dot(hi, tri_up) + _dot(lo, tri_up)
        suffix = _dot(hi, tri_lo) + _dot(lo, tri_lo)
        is_forget = (row % (2 * N_HEADS)) >= N_HEADS
        rows_ref[ci] = jnp.where(is_forget, jnp.where(row >= 2 * N_HEADS, suffix, prefix), gr)


def _mlstm_gates(g, seq, tt, gate_b):
    batch = g.shape[0] // seq
    nc = seq // tt
    grow = g[:, :N_GATES].reshape(-1, tt, N_GATES).transpose(0, 2, 1)
    gb = gate_b.astype(F32)
    return pl.pallas_call(
        _ml_gate_kernel,
        out_shape=(jax.ShapeDtypeStruct((batch * nc, N_GATES, tt), F32),
                   jax.ShapeDtypeStruct((batch * seq, 2 * LANE), F32)),
        grid=(batch,),
        in_specs=[pl.BlockSpec((seq, LANE), lambda b: (b, 0)),
                  pl.BlockSpec((nc, N_GATES, tt), lambda b: (b, 0, 0)),
                  pl.BlockSpec((1, LANE), lambda b: (0, 0)),
                  pl.BlockSpec((N_GATES, 1), lambda b: (0, 0)),
                  pl.BlockSpec((2 * LANE, 2 * LANE), lambda b: (0, 0))],
        out_specs=(pl.BlockSpec((nc, N_GATES, tt), lambda b: (b, 0, 0)),
                   pl.BlockSpec((seq, 2 * LANE), lambda b: (b, 0))),
        compiler_params=_cparams(("arbitrary",)),
        name="mlstm_gates",
    )(g, grow, jnp.pad(gb, (0, LANE - N_GATES)).reshape(1, LANE), gb.reshape(-1, 1), _gate_select_matrix())


def _mlstm_chunk(q, k, vt, ones_row, b_row, i_row, c_col, ct_ref, m_ref, reverse):
    tt = q.shape[0]
    mask = _tri(tt, upper=not reverse)
    end = 0 if reverse else tt - 1
    b_end = b_row[:, end:end + 1]
    dm = jnp.where(mask, c_col + b_row, -jnp.inf)
    dmax = jnp.max(dm, axis=0, keepdims=True)
    pt = (_dot_nt(k, q) * jnp.exp(dm - dmax)).astype(BF16)
    pv = _dot(vt, pt)
    g = b_end - b_row + i_row
    gmax = jnp.max(g, axis=1, keepdims=True)

    m_prev = m_ref[...]
    ct_prev = ct_ref[...]
    m_inter = b_row + m_prev
    m_t = jnp.maximum(m_inter, dmax)
    num = jnp.exp(dmax - m_t) * pv + jnp.exp(m_inter - m_t) * _dot_nt(ct_prev.astype(BF16), q)
    den = num[ones_row:ones_row + 1, :]
    ht = num * (1.0 / jnp.maximum(jnp.abs(den), jnp.exp(-m_t)))
    m_new = jnp.maximum(b_end + m_prev, gmax)
    vw = (vt.astype(F32) * jnp.exp(g - m_new)).astype(BF16)
    ct_ref[...] = jnp.exp(b_end + m_prev - m_new) * ct_prev + _dot(vw, k)
    m_ref[...] = m_new
    return ht


def _mlstm_kernel(qx_ref, kx_ref, vx_ref, ox_ref, rwx_ref, ccx_ref,
                  qc_ref, kc_ref, vc_ref, oc_ref, rwc_ref, ccc_ref,
                  wq_ref, wk_ref, ng_ref,
                  yx_ref, yc_ref,
                  qsx, ksx, vtx, qsc, ksc, vtc, hfx, hbx, hfc, hbc, qpx, kpx, qpc, kpc, ct_scr, m_scr,
                  *, head_dim, write_ctx, unroll):
    pair = pl.program_id(1)
    tt = rwx_ref.shape[2]
    kscale = head_dim ** -0.5
    wins = _head_windows(head_dim)
    pad_rows = WIN - head_dim
    ones_block = (lax.broadcasted_iota(jnp.int32, (pad_rows, tt), 0) == 0).astype(BF16)

    def prep(q_ref, k_ref, v_ref, qpad, kpad, qs, ks, vts):
        _fill_padded(qpad, q_ref)
        _fill_padded(kpad, k_ref)

        def block(i, carry):
            r0 = pl.multiple_of(i * tt, tt)
            rows, halo = pl.ds(r0, tt), pl.ds(r0, tt + 2 * BF16_ROWS)
            qs[rows, :] = _silu(_conv_window(qpad[halo, :].astype(F32), wq_ref[...])).astype(BF16)
            ka = _silu(_conv_window(kpad[halo, :].astype(F32), wk_ref[...])) * kscale
            for j, (win, valid, _) in enumerate(wins):
                ks[j, rows, :] = jnp.where(valid, ka[:, win], 0.0).astype(BF16)
                vt = v_ref[rows, win].T
                vts[j, i] = (jnp.concatenate([vt[:head_dim], ones_block], axis=0) if j == 0 else
                             jnp.concatenate([ones_block, vt[pad_rows:]], axis=0))
            return carry

        lax.fori_loop(0, q_ref.shape[0] // tt, block, 0)

    prep(qx_ref, kx_ref, vx_ref, qpx, kpx, qsx, ksx, vtx)
    prep(qc_ref, kc_ref, vc_ref, qpc, kpc, qsc, ksc, vtc)
    ct_scr[...] = jnp.zeros_like(ct_scr)
    m_scr[...] = jnp.zeros_like(m_scr)

    sel_r = lax.broadcasted_iota(jnp.int32, (N_GATES, 1), 0)
    pick_r = lambda a, c: jnp.sum(jnp.where(sel_r == c, a, 0.0), axis=0, keepdims=True)

    def run(ci_f, ci_b, q_s, k_s, v_t, rows_ref, cc_ref, hf, hb):
        for reverse, ci, hout in ((False, ci_f, hf), (True, ci_b, hb)):
            rows = pl.ds(pl.multiple_of(ci * tt, tt), tt)
            gates = rows_ref[ci]
            cc = cc_ref[rows, :]
            for j, (win, _, free) in enumerate(wins):
                irow = (2 * N_HEADS if reverse else 0) + pair * 2 + j
                lane_c = 2 * j + (1 if reverse else 0)
                chain = j * 2 + (1 if reverse else 0)
                ht = _mlstm_chunk(q_s[rows, win], k_s[j, rows, :], v_t[j, ci], free,
                                  pick_r(gates, irow + N_HEADS), pick_r(gates, irow),
                                  cc[:, lane_c:lane_c + 1], ct_scr.at[chain], m_scr.at[chain], reverse)
                hout[j, rows, :] = ht.T

    _scan_loops(qc_ref.shape[0] // tt, qx_ref.shape[0] // tt,
                lambda f, b: run(f, b, qsc, ksc, vtc, rwc_ref, ccc_ref, hfc, hbc),
                lambda f, b: run(f, b, qsx, ksx, vtx, rwx_ref, ccx_ref, hfx, hbx), unroll)

    def finish(hf, hb, o_ref, y_ref):
        def block(rows):
            hn = _pair_finish(hf, hb, rows, head_dim, mask_invalid=True)
            y_ref[rows, :] = (hn * ng_ref[...] * _sigmoid(o_ref[rows, :].astype(F32))).astype(BF16)

        _row_blocks(o_ref.shape[0], tt, block)

    finish(hfx, hbx, ox_ref, yx_ref)
    if write_ctx:
        finish(hfc, hbc, oc_ref, yc_ref)
    else:
        yc_ref[...] = jnp.zeros_like(yc_ref)


def _mlstm(px, pc, gx, gc, seq, seq_c, col0, conv_w, gate_b, norm_g, head_dim, write_ctx):
    batch = px.shape[0] // seq
    total = N_HEADS * head_dim
    npair = total // PAIR_W
    cb = col0 // PAIR_W
    tt = MXU_W if (seq % MXU_W == 0 and seq_c % MXU_W == 0) else CHUNK
    rows_x, cc_x = _mlstm_gates(gx, seq, tt, gate_b)
    rows_c, cc_c = _mlstm_gates(gc, seq_c, tt, gate_b)

    def pspecs(s):
        return [pl.BlockSpec((s, PAIR_W), lambda b, h, k=k: (b, cb + k * npair + h)) for k in range(4)]

    def gspecs(s):
        return [pl.BlockSpec((s // tt, N_GATES, tt), lambda b, h: (b, 0, 0)),
                pl.BlockSpec((s, LANE), lambda b, h: (b, h))]

    scr = lambda s, dt: pltpu.VMEM((2, s, WIN), dt)
    scr_t = lambda s: pltpu.VMEM((2, s // tt, WIN, tt), BF16)
    scr_q = lambda s: pltpu.VMEM((s, PAIR_W), BF16)
    scr_pad = lambda s: pltpu.VMEM((s + 2 * BF16_ROWS, PAIR_W), BF16)
    kern = functools.partial(_mlstm_kernel, head_dim=head_dim, write_ctx=write_ctx, unroll=2)
    return pl.pallas_call(
        kern,
        out_shape=(jax.ShapeDtypeStruct((batch * seq, total), BF16),
                   jax.ShapeDtypeStruct((batch * seq_c, total), BF16)),
        grid=(batch, npair),
        in_specs=(pspecs(seq) + gspecs(seq) + pspecs(seq_c) + gspecs(seq_c) + [
            pl.BlockSpec((3, PAIR_W), lambda b, h: (0, h)),
            pl.BlockSpec((3, PAIR_W), lambda b, h: (0, npair + h)),
            pl.BlockSpec((1, PAIR_W), lambda b, h: (0, h)),
        ]),
        out_specs=(pl.BlockSpec((seq, PAIR_W), lambda b, h: (b, h)),
                   pl.BlockSpec((seq_c, PAIR_W), lambda b, h: (b, h))),
        scratch_shapes=[scr_q(seq), scr(seq, BF16), scr_t(seq),
                        scr_q(seq_c), scr(seq_c, BF16), scr_t(seq_c),
                        scr(seq, F32), scr(seq, F32), scr(seq_c, F32), scr(seq_c, F32),
                        scr_pad(seq), scr_pad(seq), scr_pad(seq_c), scr_pad(seq_c),
                        pltpu.VMEM((4, WIN, WIN), F32),
                        pltpu.VMEM((4, 1, 1), F32)],
        compiler_params=_cparams(("arbitrary", "arbitrary")),
        name="mlstm",
    )(px, px, px, px, rows_x, cc_x, pc, pc, pc, pc, rows_c, cc_c,
      conv_w, conv_w, norm_g.reshape(1, total))


def _ret_chunk(q, k, v, dmask, q_decay, k_decay, c_decay, s_ref):
    pv = _dot((_dot_nt(q, k) * dmask).astype(BF16), v)
    kd = (k.astype(F32) * k_decay).astype(BF16)
    s_prev = s_ref[...]
    s_ref[...] = c_decay * s_prev + _dot_tn(kd, v)
    return pv + q_decay * _dot(q, s_prev.astype(BF16))


def _ret_kernel(ld_ref, qx_ref, kx_ref, vx_ref, gx_ref, qc_ref, kc_ref, vc_ref, gc_ref, cos_ref, sgn_ref,
                yx_ref, yc_ref,
                qsx, ksx, vsx, qsc, ksc, vsc, ofx, obx, ofc, obc, s_scr,
                *, key_dim, val_dim, write_ctx, unroll, t):
    pair = pl.program_id(1)
    qscale = key_dim ** -0.5
    half = key_dim // 2
    vwins = _head_windows(val_dim)
    lane = lax.broadcasted_iota(jnp.int32, (1, WIN), 1)
    klane = lax.broadcasted_iota(jnp.int32, (1, qx_ref.shape[1]), 1)
    first_half = (klane % key_dim) < half

    def rope(a, rows):
        width = a.shape[1]
        partner = jnp.where(first_half, pltpu.roll(a, width - half, axis=1), pltpu.roll(a, half, axis=1))
        return a * cos_ref[rows, :] + partner * sgn_ref[rows, :]

    def prep(q_ref, k_ref, v_ref, qs, ks, vs, rotary):
        def block(rows):
            qa, ka = q_ref[rows, :].astype(F32), k_ref[rows, :].astype(F32)
            if rotary:
                qa, ka = rope(qa, rows), rope(ka, rows)
            qs[rows, :] = (jnp.where(pair == 0, qa[:, :WIN], qa[:, LANE:LANE + WIN]) * qscale).astype(BF16)
            kw = jnp.where(pair == 0, ka[:, :WIN], ka[:, LANE:LANE + WIN])
            for j, (win, valid, _) in enumerate(vwins):
                start = (2 * pair + j) * key_dim - LANE * pair
                kvalid = jnp.logical_and(lane >= start, lane < start + key_dim)
                ks[j, rows, :] = jnp.where(kvalid, kw, 0.0).astype(BF16)
                vs[j, rows, :] = jnp.where(valid, v_ref[rows, win].astype(F32), 0.0).astype(BF16)

        _row_blocks(q_ref.shape[0], t, block)

    prep(qx_ref, kx_ref, vx_ref, qsx, ksx, vsx, True)
    prep(qc_ref, kc_ref, vc_ref, qsc, ksc, vsc, False)
    s_scr[...] = jnp.zeros_like(s_scr)

    r = lax.broadcasted_iota(jnp.int32, (t, t), 0)
    c = lax.broadcasted_iota(jnp.int32, (t, t), 1)
    idx = lax.broadcasted_iota(jnp.int32, (t, 1), 0).astype(F32)
    consts = []
    for j in range(2):
        for reverse in (False, True):
            lg = -jnp.exp(jnp.full((1, 1), ld_ref[1 if reverse else 0, pair * 2 + j], F32))
            rel = (c - r) if reverse else (r - c)
            dmask = jnp.where(rel >= 0, jnp.exp(lg * jnp.maximum(rel, 0).astype(F32)), 0.0)
            pos = (t - 1.0 - idx) if reverse else idx
            consts.append((dmask, jnp.exp(lg * (pos + 1.0)), jnp.exp(lg * (t - 1.0 - pos)), jnp.exp(lg * t)))

    def run(ci_f, ci_b, q_s, k_s, v_s, of, ob):
        for reverse, ci, oout in ((False, ci_f, of), (True, ci_b, ob)):
            rows = pl.ds(pl.multiple_of(ci * t, t), t)
            for j in range(2):
                chain = j * 2 + (1 if reverse else 0)
                dmask, q_decay, k_decay, c_decay = consts[chain]
                oout[j, rows, :] = _ret_chunk(q_s[rows, :], k_s[j, rows, :], v_s[j, rows, :],
                                              dmask, q_decay, k_decay, c_decay, s_scr.at[chain])

    _scan_loops(qc_ref.shape[0] // t, qx_ref.shape[0] // t,
                lambda f, b: run(f, b, qsc, ksc, vsc, ofc, obc),
                lambda f, b: run(f, b, qsx, ksx, vsx, ofx, obx), unroll)

    def finish(of, ob, g_ref, y_ref):
        def block(rows):
            on = _pair_finish(of, ob, rows, val_dim, mask_invalid=False)
            y_ref[rows, :] = (on * _silu(g_ref[rows, :].astype(F32))).astype(BF16)

        _row_blocks(g_ref.shape[0], t, block)

    finish(ofx, obx, gx_ref, yx_ref)
    if write_ctx:
        finish(ofc, obc, gc_ref, yc_ref)
    else:
        yc_ref[...] = jnp.zeros_like(yc_ref)


def _retention(px, pc, seq, seq_c, col0, log_decay, cosf, sgnf, key_dim, val_dim, write_ctx):
    batch = px.shape[0] // seq
    ktot, vtot = N_HEADS * key_dim, N_HEADS * val_dim
    assert ktot == PAIR_W
    npair = vtot // PAIR_W
    qb = col0 // PAIR_W
    vb = (col0 + 2 * ktot) // PAIR_W

    def pspecs(s):
        return [pl.BlockSpec((s, PAIR_W), lambda b, h: (b, qb)),
                pl.BlockSpec((s, PAIR_W), lambda b, h: (b, qb + 1)),
                pl.BlockSpec((s, PAIR_W), lambda b, h: (b, vb + h)),
                pl.BlockSpec((s, PAIR_W), lambda b, h: (b, vb + npair + h))]

    scr = lambda s, dt: pltpu.VMEM((2, s, WIN), dt)
    t = MXU_W if (seq % MXU_W == 0 and seq_c % MXU_W == 0) else CHUNK
    kern = functools.partial(_ret_kernel, key_dim=key_dim, val_dim=val_dim, write_ctx=write_ctx, unroll=2, t=t)
    return pl.pallas_call(
        kern,
        out_shape=(jax.ShapeDtypeStruct((batch * seq, vtot), BF16),
                   jax.ShapeDtypeStruct((batch * seq_c, vtot), BF16)),
        grid=(batch, npair),
        in_specs=([pl.BlockSpec(memory_space=pltpu.SMEM)] + pspecs(seq) + pspecs(seq_c) + [
            _resident((seq, PAIR_W), lambda b, h: (0, 0)),
            _resident((seq, PAIR_W), lambda b, h: (0, 0)),
        ]),
        out_specs=(pl.BlockSpec((seq, PAIR_W), lambda b, h: (b, h)),
                   pl.BlockSpec((seq_c, PAIR_W), lambda b, h: (b, h))),
        scratch_shapes=[pltpu.VMEM((seq, WIN), BF16), scr(seq, BF16), scr(seq, BF16),
                        pltpu.VMEM((seq_c, WIN), BF16), scr(seq_c, BF16), scr(seq_c, BF16),
                        scr(seq, F32), scr(seq, F32), scr(seq_c, F32), scr(seq_c, F32),
                        pltpu.VMEM((4, WIN, WIN), F32)],
        compiler_params=_cparams(("arbitrary", "arbitrary")),
        name="retention",
    )(log_decay.astype(F32), px, px, px, px, pc, pc, pc, pc, cosf, sgnf)


@functools.lru_cache(maxsize=None)
def _rope_tables(seq, key_dim):
    rows = seq // GRID_W
    r = np.repeat(np.arange(rows, dtype=np.float32), GRID_W)
    col = np.tile(np.arange(GRID_W, dtype=np.float32), rows)
    nf = key_dim // 4
    inv = (np.float32(ROPE_BASE) ** (-np.arange(nf, dtype=np.float32) / np.float32(nf))).astype(np.float32)
    ang = np.concatenate([r[:, None] * inv, col[:, None] * inv], axis=-1).astype(np.float32)
    cos, sin = np.cos(ang).astype(np.float32), np.sin(ang).astype(np.float32)
    return np.tile(np.concatenate([cos, cos], axis=1), (1, N_HEADS)), np.tile(np.concatenate([-sin, sin], axis=1), (1, N_HEADS))


def kernel(x, c, ctx, c_ctx, norm1_g, norm2_g, w_mod, b_mod, w_in, hy_conv_w, hy_f_w1, hy_f_b1, hy_f_w2, hy_f_b2, hy_f_w3, hy_f_freq, hy_bias, ml_conv_w, ml_gate_b, ml_norm_g, rt_log_decay, w_out, w_ff1, w_ff2, final_g):
    batch, seq, d = x.shape
    seq_c = ctx.shape[1]
    depth = w_mod.shape[0]
    hy_w = d // 4
    ml_w = 3 * d // 8
    rt_w = 3 * d // 8
    ml_dh = ml_w // N_HEADS
    rt_dv = rt_w // N_HEADS
    rt_dk = rt_dv // 2
    hy_cols, ml_cols = 3 * hy_w, 4 * ml_w
    gate0 = hy_cols + ml_cols
    rt0 = gate0 + N_GATES

    s_in = jnp.concatenate([c, c_ctx[None], jnp.zeros((MOD_ROWS - batch - 1, d), F32)], axis=0)
    mod = _modulation(s_in, w_mod, b_mod)

    grp_x = lambda tm: (lambda i: (i * tm) // seq)
    grp_c = lambda tm: (lambda i: batch + 0 * i)
    tm_x, tm_c = _pick_tile(seq, 1024), _pick_tile(batch * seq_c, 1024)
    tm_ox, tm_oc = _pick_tile(seq, 512), _pick_tile(batch * seq_c, 512)

    def dft(n):
        cm, sm, ct, st = _dft_tables(n)
        return jnp.asarray(cm).astype(BF16), jnp.asarray(sm).astype(BF16), ct, st

    cm_x, sm_x, ct_x, st_x = dft(seq)
    cm_c, sm_c, ct_c, st_c = dft(seq_c)
    cosf, sgnf = _rope_tables(seq, rt_dk)
    tc = _pick_tile(hy_w, MXU_W)
    tc_spec = _pick_tile(hy_w, LANE)

    xs = x.reshape(batch * seq, d)
    hc = ctx.reshape(batch * seq_c, d)
    w_a = w_in[:, :, :gate0].astype(BF16)
    w_b = w_in[:, :, rt0:].astype(BF16)
    w_g = jnp.pad(w_in[:, :, gate0:rt0], ((0, 0), (0, 0), (0, LANE - N_GATES))).astype(BF16)
    ml_col0, rt_col0 = hy_cols, gate0
    wo = w_out.astype(BF16)
    w1 = w_ff1.astype(BF16)
    w2 = w_ff2.astype(BF16)

    for l in range(depth):
        need_ctx = l < depth - 1
        mod3 = mod[l].reshape(MOD_ROWS * N_MOD, 1, d)

        px, gx = _in_proj(xs, norm1_g[l], mod3, grp_x(tm_x), w_a, w_b, w_g, l, tm_x)
        pc, gc = _in_proj(hc, norm1_g[l], mod3, grp_c(tm_c), w_a, w_b, w_g, l, tm_c)

        filt = (hy_f_w1[l], hy_f_b1[l], hy_f_w2[l], hy_f_b2[l], hy_f_w3[l], hy_f_freq[l])
        hf, hb = _hyena_filter(seq, *filt)
        hre, him = _hyena_spectrum(hf, hb, cm_x, sm_x, ct_x, st_x, tc_spec)
        y_hy = _hyena_conv(px, seq, hy_conv_w[l], hre, him, hy_bias[l], cm_x, sm_x, tc)
        y_ml, yc_ml = _mlstm(px, pc, gx, gc, seq, seq_c, ml_col0, ml_conv_w[l], ml_gate_b[l], ml_norm_g[l],
                             ml_dh, need_ctx)
        y_rt, yc_rt = _retention(px, pc, seq, seq_c, rt_col0, rt_log_decay[l], cosf, sgnf,
                                 rt_dk, rt_dv, need_ctx)
        xs, h2 = _out_proj(xs, y_hy, y_ml, y_rt, wo, l, norm2_g[l], mod3, grp_x(tm_ox), tm_ox, alias=l > 0)
        if need_ctx:
            hfc, hbc = _hyena_filter(seq_c, *filt)
            hre_c, him_c = _hyena_spectrum(hfc, hbc, cm_c, sm_c, ct_c, st_c, tc_spec)
            yc_hy = _hyena_conv(pc, seq_c, hy_conv_w[l], hre_c, him_c, hy_bias[l], cm_c, sm_c, tc)
            hc, h2c = _out_proj(hc, yc_hy, yc_ml, yc_rt, wo, l, norm2_g[l], mod3, grp_c(tm_oc), tm_oc, alias=l > 0)
            hc = _ffn(hc, h2c, mod3, grp_c(tm_c), w1, w2, l, tm_c, 1024)
        xs = _ffn(xs, h2, mod3, grp_x(tm_x), w1, w2, l, tm_x, 1024)
    return _final_norm(xs, final_g, tm_ox).reshape(batch, seq, d)
```

```python
import functools
import math

import jax
import jax.numpy as jnp
import numpy as np
from jax import lax
from jax.experimental import pallas as pl
from jax.experimental.pallas import tpu as pltpu

F32 = jnp.float32
BF16 = jnp.bfloat16

GRID_W = 64
HY_EMB = 33
HY_FF = 64
HY_DECAY_TARGET = 1e-2
HY_SHORT_PCT = 0.3
HY_LONG_PCT = 1.5
N_HEADS = 4
N_GATES = 4 * N_HEADS
CHUNK = 128
ROPE_BASE = 10000.0
N_MOD = 6
EPS = 1e-6

LANE = 128
MXU_W = 256
WIN = 2 * LANE
PAIR_W = 3 * LANE
MOD_ROWS = 16
VMEM_LIMIT = 56 * 1024 * 1024


def _cparams(sem):
    return pltpu.CompilerParams(dimension_semantics=sem, vmem_limit_bytes=VMEM_LIMIT)


def _sigmoid(x):
    return 0.5 * jnp.tanh(0.5 * x) + 0.5


def _silu(x):
    return x * _sigmoid(x)


def _log_sigmoid(x):
    return jnp.minimum(x, 0.0) - jnp.log(1.0 + jnp.exp(-jnp.abs(x)))


def _dot(a, b):
    return jnp.dot(a, b, preferred_element_type=F32)


def _dot_nt(a, b):
    return lax.dot_general(a, b, (((1,), (1,)), ((), ())), preferred_element_type=F32)


def _dot_tn(a, b):
    return lax.dot_general(a, b, (((0,), (0,)), ((), ())), preferred_element_type=F32)


def _split_bf16(a):
    hi = a.astype(BF16)
    lo = (a - hi.astype(F32)).astype(BF16)
    return hi, lo


def _dot_split(a, b_f32):
    hi, lo = _split_bf16(b_f32)
    return _dot(a, hi) + _dot(a, lo)


def _resident(shape, index_map):
    return pl.BlockSpec(shape, index_map, pipeline_mode=pl.Buffered(1))


def _pick_tile(n, target):
    t = min(n, target)
    while n % t:
        t //= 2
    return t


def _norm_mod(x, g, shift, scale):
    ms = jnp.mean(x * x, axis=-1, keepdims=True)
    y = x * lax.rsqrt(ms + EPS) * g
    return y * (1.0 + scale) + shift


def _mod_kernel(s_ref, w_ref, b_ref, o_ref):
    s = _silu(s_ref[...]).astype(BF16)
    w = w_ref[0].astype(BF16)
    o_ref[0] = _dot(s, w) + b_ref[0]


def _modulation(s_in, w_mod, b_mod):
    depth, d, n = w_mod.shape
    tn = _pick_tile(n, 1024)
    return pl.pallas_call(
        _mod_kernel,
        out_shape=jax.ShapeDtypeStruct((depth, MOD_ROWS, n), F32),
        grid=(depth, n // tn),
        in_specs=[
            pl.BlockSpec((MOD_ROWS, d), lambda l, j: (0, 0)),
            pl.BlockSpec((1, d, tn), lambda l, j: (l, 0, j)),
            pl.BlockSpec((1, 1, tn), lambda l, j: (l, 0, j)),
        ],
        out_specs=pl.BlockSpec((1, MOD_ROWS, tn), lambda l, j: (l, 0, j)),
        compiler_params=_cparams(("arbitrary", "arbitrary")),
        name="modulation",
    )(s_in, w_mod, b_mod.reshape(depth, 1, n))


def _inproj_kernel(x_ref, g_ref, shift_ref, scale_ref, wt_ref, wgt_ref, p_ref, gate_ref, h_scr):
    @pl.when(pl.program_id(1) == 0)
    def _():
        h = _norm_mod(x_ref[...], g_ref[...], shift_ref[0], scale_ref[0]).astype(BF16)
        h_scr[...] = h
        gate_ref[...] = _dot_nt(h, wgt_ref[...])

    p_ref[...] = _dot_nt(h_scr[...], wt_ref[...]).astype(BF16)


def _in_proj(x2, norm_g, mod3, group_of_tile, w_pt, w_gt, layer, tm):
    m, d = x2.shape
    n = w_pt.shape[1]
    tn = _pick_tile(n, 1024)
    return pl.pallas_call(
        _inproj_kernel,
        out_shape=(jax.ShapeDtypeStruct((m, n), BF16), jax.ShapeDtypeStruct((m, LANE), F32)),
        grid=(m // tm, n // tn),
        in_specs=[
            pl.BlockSpec((tm, d), lambda i, j: (i, 0)),
            pl.BlockSpec((1, d), lambda i, j: (0, 0)),
            pl.BlockSpec((1, 1, d), lambda i, j: (group_of_tile(i) * N_MOD + 0, 0, 0)),
            pl.BlockSpec((1, 1, d), lambda i, j: (group_of_tile(i) * N_MOD + 1, 0, 0)),
            pl.BlockSpec((None, tn, d), lambda i, j: (layer, j, 0)),
            pl.BlockSpec((None, LANE, d), lambda i, j: (layer, 0, 0)),
        ],
        out_specs=(pl.BlockSpec((tm, tn), lambda i, j: (i, j)),
                   pl.BlockSpec((tm, LANE), lambda i, j: (i, 0))),
        scratch_shapes=[pltpu.VMEM((tm, d), BF16)],
        compiler_params=_cparams(("arbitrary", "arbitrary")),
        name="in_proj",
    )(x2, norm_g.reshape(1, d), mod3, mod3, w_pt, w_gt)


def _outproj_kernel(x_ref, yh_ref, ym_ref, yr_ref, w_ref, gate_ref,
                    g2_ref, shift_ref, scale_ref, o_ref, h2_ref):
    kh, km = yh_ref.shape[1], ym_ref.shape[1]
    acc = _dot(yh_ref[...], w_ref[0:kh, :])
    acc = acc + _dot(ym_ref[...], w_ref[kh:kh + km, :])
    acc = acc + _dot(yr_ref[...], w_ref[kh + km:, :])
    xn = x_ref[...] + gate_ref[0] * acc
    o_ref[...] = xn
    h2_ref[...] = _norm_mod(xn, g2_ref[...], shift_ref[0], scale_ref[0]).astype(BF16)


def _out_proj(x2, y_hy, y_ml, y_rt, wo, layer, norm2_g, mod3, group_of_tile, tm, alias):
    m, d = x2.shape
    kh, km, kr = y_hy.shape[1], y_ml.shape[1], y_rt.shape[1]
    mrow = lambda k: pl.BlockSpec((1, 1, d), lambda i: (group_of_tile(i) * N_MOD + k, 0, 0))
    return pl.pallas_call(
        _outproj_kernel,
        out_shape=(jax.ShapeDtypeStruct((m, d), F32), jax.ShapeDtypeStruct((m, d), BF16)),
        grid=(m // tm,),
        in_specs=[
            pl.BlockSpec((tm, d), lambda i: (i, 0)),
            pl.BlockSpec((tm, kh), lambda i: (i, 0)),
            pl.BlockSpec((tm, km), lambda i: (i, 0)),
            pl.BlockSpec((tm, kr), lambda i: (i, 0)),
            _resident((None, kh + km + kr, d), lambda i: (layer, 0, 0)),
            mrow(2),
            pl.BlockSpec((1, d), lambda i: (0, 0)),
            mrow(3), mrow(4),
        ],
        out_specs=(pl.BlockSpec((tm, d), lambda i: (i, 0)), pl.BlockSpec((tm, d), lambda i: (i, 0))),
        input_output_aliases=({0: 0} if alias else {}),
        compiler_params=_cparams(("arbitrary",)),
        name="out_proj",
    )(x2, y_hy, y_ml, y_rt, wo, mod3, norm2_g.reshape(1, d), mod3, mod3)


def _ffn_kernel(h_ref, x_ref, gate_ref, w1_ref, w2_ref, o_ref, hid_scr, *, n1):
    j = pl.program_id(1)
    tf = w1_ref.shape[1]

    @pl.when(j < n1)
    def _():
        a = jnp.maximum(_dot(h_ref[...], w1_ref[...]), 0.0)
        hid_scr[jnp.minimum(j, n1 - 1)] = (a * a).astype(BF16)

    @pl.when(j >= n1)
    def _():
        acc = _dot(hid_scr[0], w2_ref[0:tf, :])
        for c in range(1, n1):
            acc = acc + _dot(hid_scr[c], w2_ref[c * tf:(c + 1) * tf, :])
        o_ref[...] = x_ref[...] + gate_ref[0] * acc


def _ffn(x2, h2, mod3, group_of_tile, w1, w2, layer, tm, tf):
    m, d = x2.shape
    f = w1.shape[2]
    n1 = f // tf
    tn = MXU_W
    n2 = d // tn
    col = lambda j: jnp.maximum(j - n1, 0)
    return pl.pallas_call(
        functools.partial(_ffn_kernel, n1=n1),
        out_shape=jax.ShapeDtypeStruct((m, d), F32),
        grid=(m // tm, n1 + n2),
        in_specs=[
            pl.BlockSpec((tm, d), lambda i, j: (i, 0)),
            pl.BlockSpec((tm, tn), lambda i, j: (i, col(j))),
            pl.BlockSpec((1, 1, tn), lambda i, j: (group_of_tile(i) * N_MOD + 5, 0, col(j))),
            pl.BlockSpec((None, d, tf), lambda i, j: (layer, 0, jnp.minimum(j, n1 - 1))),
            pl.BlockSpec((None, f, tn), lambda i, j: (layer, 0, col(j))),
        ],
        out_specs=pl.BlockSpec((tm, tn), lambda i, j: (i, col(j))),
        scratch_shapes=[pltpu.VMEM((n1, tm, tf), BF16)],
        compiler_params=_cparams(("arbitrary", "arbitrary")),
        name="ffn",
    )(h2, x2, mod3, w1, w2)


def _final_norm_kernel(x_ref, g_ref, o_ref):
    x = x_ref[...]
    ms = jnp.mean(x * x, axis=-1, keepdims=True)
    o_ref[...] = x * lax.rsqrt(ms + EPS) * g_ref[...]


def _final_norm(x2, g, tm):
    m, d = x2.shape
    return pl.pallas_call(
        _final_norm_kernel,
        out_shape=jax.ShapeDtypeStruct((m, d), F32),
        grid=(m // tm,),
        in_specs=[pl.BlockSpec((tm, d), lambda i: (i, 0)), pl.BlockSpec((1, d), lambda i: (0, 0))],
        out_specs=pl.BlockSpec((tm, d), lambda i: (i, 0)),
        compiler_params=_cparams(("arbitrary",)),
        name="final_norm",
    )(x2, g.reshape(1, d))


def _hy_filter_kernel(z_ref, t_ref, w1_ref, b1_ref, w2_ref, b2_ref, w3f_ref, w3b_ref, fr_ref, dl_ref,
                      hf_ref, hb_ref):
    hp = lax.Precision.HIGHEST
    dot = lambda a, b: jnp.dot(a, b, precision=hp, preferred_element_type=F32)
    fr = fr_ref[...]
    hdn = jnp.sin(fr * (dot(z_ref[...], w1_ref[...]) + b1_ref[...]))
    hdn = jnp.sin(fr * (dot(hdn, w2_ref[...]) + b2_ref[...]))
    win = jnp.exp(-t_ref[...] * dl_ref[...])
    hf = dot(hdn, w3f_ref[...]) * win
    hb = dot(hdn, w3b_ref[...]) * win
    row = lax.broadcasted_iota(jnp.int32, hb.shape, 0)
    hb = jnp.where(row == 0, 0.0, hb)
    inv = 1.0 / (jnp.sum(jnp.abs(hf), axis=0, keepdims=True) + jnp.sum(jnp.abs(hb), axis=0, keepdims=True))
    hf_ref[...] = hf * inv
    hb_ref[...] = hb * inv


@functools.lru_cache(maxsize=None)
def _filter_tables(seq, hy_w):
    t = np.linspace(0.0, 1.0, seq, dtype=np.float32)[:, None]
    w = (np.float32(2.0 * math.pi / seq) * np.arange(seq, dtype=np.float32))[:, None]
    nb = (HY_EMB - 1) // 2
    bands = np.linspace(1e-4, nb - 1, nb, dtype=np.float32)[None, :]
    z = np.concatenate([t, np.cos(bands * w), -np.sin(bands * w)], axis=-1).astype(np.float32)
    z = np.pad(z, ((0, 0), (0, LANE - HY_EMB)))
    deltas = np.abs(np.linspace(math.log(HY_DECAY_TARGET) / HY_LONG_PCT,
                                math.log(HY_DECAY_TARGET) / HY_SHORT_PCT, hy_w, dtype=np.float32))[None, :]
    return z, t, deltas


def _hyena_filter(seq, w1, b1, w2, b2, w3, freq):
    hy_w = w3.shape[1] // 2
    z, t, deltas = _filter_tables(seq, hy_w)
    pad_ff = LANE - HY_FF
    w1p = jnp.pad(w1, ((0, LANE - HY_EMB), (0, pad_ff)))
    w2p = jnp.pad(w2, ((0, pad_ff), (0, pad_ff)))
    w3p = jnp.pad(w3, ((0, pad_ff), (0, 0)))
    padv = lambda v: jnp.pad(v, (0, pad_ff)).reshape(1, LANE)
    out = jax.ShapeDtypeStruct((seq, hy_w), F32)
    return pl.pallas_call(
        _hy_filter_kernel,
        out_shape=(out, out),
        compiler_params=pltpu.CompilerParams(vmem_limit_bytes=VMEM_LIMIT),
        name="hyena_filter",
    )(z, t, w1p, padv(b1), w2p, padv(b2), w3p[:, :hy_w], w3p[:, hy_w:], padv(freq), deltas)


@functools.lru_cache(maxsize=None)
def _dft_tables(seq):
    k = np.arange(seq, dtype=np.int64)
    mm = ((2 * k[:, None] + 1) * (2 * k[None, :] + 1)) % (8 * seq)
    ang = mm.astype(np.float64) * (2.0 * math.pi / (8 * seq))
    theta = (k.astype(np.float64) + 0.5) * (math.pi / (2 * seq))
    return (np.cos(ang).astype(np.float32), np.sin(ang).astype(np.float32),
            np.cos(theta).astype(np.float32)[:, None], np.sin(theta).astype(np.float32)[:, None])


def _hy_spec_kernel(hf_ref, hb_ref, c_ref, s_ref, ct_ref, st_ref, hre_ref, him_ref):
    cm, sm = c_ref[...], s_ref[...]
    seq = cm.shape[0]
    tc = hf_ref.shape[1]
    taps = jnp.concatenate([hf_ref[...], hb_ref[...]], axis=1)
    ca, sa = _dot_split(cm, taps), _dot_split(sm, taps)
    a, a2 = ca[:, :tc], ca[:, tc:]
    b, b2 = sa[:, :tc], sa[:, tc:]
    ct, st = ct_ref[...], st_ref[...]
    scale = 1.0 / seq
    hre_ref[...] = ((a + a2) * ct + (b + b2) * st) * scale
    him_ref[...] = ((a - a2) * st - (b - b2) * ct) * scale


def _hyena_spectrum(hf, hb, cm, sm, ct, st, tc):
    seq, c = hf.shape
    out = jax.ShapeDtypeStruct((seq, c), F32)
    col = pl.BlockSpec((seq, tc), lambda j: (0, j))
    return pl.pallas_call(
        _hy_spec_kernel,
        out_shape=(out, out),
        grid=(c // tc,),
        in_specs=[col, col,
                  _resident((seq, seq), lambda j: (0, 0)), _resident((seq, seq), lambda j: (0, 0)),
                  pl.BlockSpec((seq, 1), lambda j: (0, 0)), pl.BlockSpec((seq, 1), lambda j: (0, 0))],
        out_specs=(col, col),
        compiler_params=_cparams(("arbitrary",)),
        name="hyena_spectrum",
    )(hf, hb, cm, sm, ct, st)


def _short_conv(p, w):
    n = p.shape[0]
    pz = jnp.concatenate([p, jnp.zeros((8, p.shape[1]), p.dtype)], axis=0)
    prev = pltpu.roll(pz, 1, axis=0)[:n]
    nxt = pltpu.roll(pz, n + 7, axis=0)[:n]
    return prev * w[0:1] + p * w[1:2] + nxt * w[2:3]


def _hy_conv_kernel(p0_ref, p1_ref, pv_ref, w0_ref, w1_ref, wv_ref, hre_ref, him_ref, bias_ref,
                    c_ref, s_ref, o_ref):
    x1 = _short_conv(p1_ref[...].astype(F32), w1_ref[...])
    v = _short_conv(pv_ref[...].astype(F32), wv_ref[...])
    u = v * x1
    ub = u.astype(BF16)
    cm, sm = c_ref[...], s_ref[...]
    zc = _dot(cm, ub)
    zs = _dot(sm, ub)
    hre, him = hre_ref[...], him_ref[...]
    yre = (hre * zc + him * zs).astype(BF16)
    yim = (him * zc - hre * zs).astype(BF16)
    y = _dot(cm, yre) - _dot(sm, yim)
    x0 = _short_conv(p0_ref[...].astype(F32), w0_ref[...])
    o_ref[...] = ((y + u * bias_ref[...]) * x0).astype(BF16)


def _hyena_conv(p, seq, conv_w, hre, him, bias, cm, sm, tc):
    m = p.shape[0]
    hy_w = hre.shape[1]
    nb = hy_w // tc
    pspec = lambda off: pl.BlockSpec((seq, tc), lambda j, b, off=off: (b, off * nb + j))
    wspec = lambda off: pl.BlockSpec((3, tc), lambda j, b, off=off: (0, off * nb + j))
    col = pl.BlockSpec((seq, tc), lambda j, b: (0, j))
    return pl.pallas_call(
        _hy_conv_kernel,
        out_shape=jax.ShapeDtypeStruct((m, hy_w), BF16),
        grid=(nb, m // seq),
        in_specs=[pspec(0), pspec(1), pspec(2), wspec(0), wspec(1), wspec(2), col, col,
                  pl.BlockSpec((1, tc), lambda j, b: (0, j)),
                  _resident((seq, seq), lambda j, b: (0, 0)), _resident((seq, seq), lambda j, b: (0, 0))],
        out_specs=pl.BlockSpec((seq, tc), lambda j, b: (b, j)),
        compiler_params=_cparams(("arbitrary", "arbitrary")),
        name="hyena_conv",
    )(p, p, p, conv_w, conv_w, conv_w, hre, him, bias.reshape(1, hy_w), cm, sm)


def _tri(n, upper):
    r = lax.broadcasted_iota(jnp.int32, (n, n), 0)
    c = lax.broadcasted_iota(jnp.int32, (n, n), 1)
    return (r <= c) if upper else (r >= c)


def _head_windows(head_dim):
    assert 2 * head_dim == PAIR_W
    lane = lax.broadcasted_iota(jnp.int32, (1, WIN), 1)
    return ((slice(0, WIN), lane < head_dim, head_dim),
            (slice(LANE, LANE + WIN), lane >= WIN - head_dim, 0))


def _pair_finish(acc_f, acc_b, rows, head_dim, mask_invalid):
    outs = []
    for j, (_, valid, _) in enumerate(_head_windows(head_dim)):
        h = acc_f[j, rows, :] + acc_b[j, rows, :]
        if mask_invalid:
            h = jnp.where(valid, h, 0.0)
        ms = jnp.sum(h * h, axis=-1, keepdims=True) * (1.0 / head_dim)
        outs.append(h * lax.rsqrt(ms + EPS))
    h0, h1 = outs
    return jnp.concatenate([h0[:, :LANE], h0[:, LANE:] + h1[:, :LANE], h1[:, LANE:]], axis=1)


def _row_blocks(n, blk, body):
    def step(i, carry):
        body(pl.ds(pl.multiple_of(i * blk, blk), blk))
        return carry

    lax.fori_loop(0, n // blk, step, 0)


BF16_ROWS = 16


def _conv_window(w, cw):
    n = w.shape[0]
    u = pltpu.roll(w, 1, axis=0) * cw[0:1] + w * cw[1:2] + pltpu.roll(w, n - 1, axis=0) * cw[2:3]
    return u[BF16_ROWS:n - BF16_ROWS]


def _fill_padded(pad_ref, src_ref):
    n = src_ref.shape[0]
    zero = jnp.zeros((BF16_ROWS, src_ref.shape[1]), src_ref.dtype)
    pad_ref[0:BF16_ROWS, :] = zero
    pad_ref[BF16_ROWS + n:2 * BF16_ROWS + n, :] = zero
    pad_ref[BF16_ROWS:BF16_ROWS + n, :] = src_ref[...]


def _scan_loops(nc_c, nc_x, run_c, run_x, unroll):
    def loop(nc, run):
        u = unroll if nc % unroll == 0 else 1

        def body(i, carry):
            for k in range(u):
                run(i * u + k, nc - 1 - (i * u + k))
            return carry

        lax.fori_loop(0, nc // u, body, 0)

    loop(nc_c, run_c)
    loop(nc_x, run_x)


@functools.lru_cache(maxsize=None)
def _gate_select_matrix():
    e = np.zeros((2 * LANE, 2 * LANE), np.float32)
    for pair in range(N_HEADS // 2):
        for j in range(2):
            for direction in range(2):
                icol = 2 * N_HEADS * direction + 2 * pair + j
                out = LANE * pair + 2 * j + direction
                e[icol, out] = 1.0
                e[LANE + icol + N_HEADS, out] = -1.0
    return e.astype(BF16)


def _ml_gate_kernel(gc_ref, gr_ref, gbc_ref, gbr_ref, esel_ref, rows_ref, cc_ref):
    nc, _, tt = gr_ref.shape
    tri_lo = _tri(tt, upper=False).astype(BF16)
    tri_up = _tri(tt, upper=True).astype(BF16)
    lane = lax.broadcasted_iota(jnp.int32, (1, LANE), 1)
    row = lax.broadcasted_iota(jnp.int32, (N_GATES, 1), 0)
    esel = esel_ref[...]
    for ci in range(nc):
        rs = slice(ci * tt, (ci + 1) * tt)
        gc = gc_ref[rs, :] + gbc_ref[...]
        gr = gr_ref[ci] + gbr_ref[...]
        ls = _log_sigmoid(gc)
        cum_c = jnp.where(lane >= 2 * N_HEADS, _dot_split(tri_up, ls), _dot_split(tri_lo, ls))
        hi, lo = _split_bf16(jnp.concatenate([gc, cum_c], axis=1))
        cc_ref[rs, :] = _dot(hi, esel) + _dot(lo, esel)
        hi, lo = _split_bf16(_log_sigmoid(gr))
        prefix = _dot(hi, tri_up) + _dot(lo, tri_up)
        suffix = _dot(hi, tri_lo) + _dot(lo, tri_lo)
        is_forget = (row % (2 * N_HEADS)) >= N_HEADS
        rows_ref[ci] = jnp.where(is_forget, jnp.where(row >= 2 * N_HEADS, suffix, prefix), gr)


def _mlstm_gates(g, seq, tt, gate_b):
    batch = g.shape[0] // seq
    nc = seq // tt
    grow = g[:, :N_GATES].reshape(-1, tt, N_GATES).transpose(0, 2, 1)
    gb = gate_b.astype(F32)
    return pl.pallas_call(
        _ml_gate_kernel,
        out_shape=(jax.ShapeDtypeStruct((batch * nc, N_GATES, tt), F32),
                   jax.ShapeDtypeStruct((batch * seq, 2 * LANE), F32)),
        grid=(batch,),
        in_specs=[pl.BlockSpec((seq, LANE), lambda b: (b, 0)),
                  pl.BlockSpec((nc, N_GATES, tt), lambda b: (b, 0, 0)),
                  pl.BlockSpec((1, LANE), lambda b: (0, 0)),
                  pl.BlockSpec((N_GATES, 1), lambda b: (0, 0)),
                  pl.BlockSpec((2 * LANE, 2 * LANE), lambda b: (0, 0))],
        out_specs=(pl.BlockSpec((nc, N_GATES, tt), lambda b: (b, 0, 0)),
                   pl.BlockSpec((seq, 2 * LANE), lambda b: (b, 0))),
        compiler_params=_cparams(("arbitrary",)),
        name="mlstm_gates",
    )(g, grow, jnp.pad(gb, (0, LANE - N_GATES)).reshape(1, LANE), gb.reshape(-1, 1), _gate_select_matrix())


def _mlstm_chunk(q, k, vt, ones_row, b_row, i_row, c_col, ct_ref, m_ref, reverse):
    tt = q.shape[0]
    mask = _tri(tt, upper=not reverse)
    end = 0 if reverse else tt - 1
    b_end = b_row[:, end:end + 1]
    dm = jnp.where(mask, c_col + b_row, -jnp.inf)
    dmax = jnp.max(dm, axis=0, keepdims=True)
    pt = (_dot_nt(k, q) * jnp.exp(dm - dmax)).astype(BF16)
    pv = _dot(vt, pt)
    g = b_end - b_row + i_row
    gmax = jnp.max(g, axis=1, keepdims=True)

    m_prev = m_ref[...]
    ct_prev = ct_ref[...]
    m_inter = b_row + m_prev
    m_t = jnp.maximum(m_inter, dmax)
    num = jnp.exp(dmax - m_t) * pv + jnp.exp(m_inter - m_t) * _dot_nt(ct_prev.astype(BF16), q)
    den = num[ones_row:ones_row + 1, :]
    ht = num * (1.0 / jnp.maximum(jnp.abs(den), jnp.exp(-m_t)))
    m_new = jnp.maximum(b_end + m_prev, gmax)
    vw = (vt.astype(F32) * jnp.exp(g - m_new)).astype(BF16)
    ct_ref[...] = jnp.exp(b_end + m_prev - m_new) * ct_prev + _dot(vw, k)
    m_ref[...] = m_new
    return ht


def _mlstm_kernel(qx_ref, kx_ref, vx_ref, ox_ref, rwx_ref, ccx_ref,
                  qc_ref, kc_ref, vc_ref, oc_ref, rwc_ref, ccc_ref,
                  wq_ref, wk_ref, ng_ref,
                  yx_ref, yc_ref,
                  qsx, ksx, vtx, qsc, ksc, vtc, hfx, hbx, hfc, hbc, qpx, kpx, qpc, kpc, ct_scr, m_scr,
                  *, head_dim, write_ctx, unroll):
    pair = pl.program_id(1)
    tt = rwx_ref.shape[2]
    kscale = head_dim ** -0.5
    wins = _head_windows(head_dim)
    pad_rows = WIN - head_dim
    ones_block = (lax.broadcasted_iota(jnp.int32, (pad_rows, tt), 0) == 0).astype(BF16)

    def prep(q_ref, k_ref, v_ref, qpad, kpad, qs, ks, vts):
        _fill_padded(qpad, q_ref)
        _fill_padded(kpad, k_ref)

        def block(i, carry):
            r0 = pl.multiple_of(i * tt, tt)
            rows, halo = pl.ds(r0, tt), pl.ds(r0, tt + 2 * BF16_ROWS)
            qs[rows, :] = _silu(_conv_window(qpad[halo, :].astype(F32), wq_ref[...])).astype(BF16)
            ka = _silu(_conv_window(kpad[halo, :].astype(F32), wk_ref[...])) * kscale
            for j, (win, valid, _) in enumerate(wins):
                ks[j, rows, :] = jnp.where(valid, ka[:, win], 0.0).astype(BF16)
                vt = v_ref[rows, win].T
                vts[j, i] = (jnp.concatenate([vt[:head_dim], ones_block], axis=0) if j == 0 else
                             jnp.concatenate([ones_block, vt[pad_rows:]], axis=0))
            return carry

        lax.fori_loop(0, q_ref.shape[0] // tt, block, 0)

    prep(qx_ref, kx_ref, vx_ref, qpx, kpx, qsx, ksx, vtx)
    prep(qc_ref, kc_ref, vc_ref, qpc, kpc, qsc, ksc, vtc)
    ct_scr[...] = jnp.zeros_like(ct_scr)
    m_scr[...] = jnp.zeros_like(m_scr)

    sel_r = lax.broadcasted_iota(jnp.int32, (N_GATES, 1), 0)
    pick_r = lambda a, c: jnp.sum(jnp.where(sel_r == c, a, 0.0), axis=0, keepdims=True)

    def run(ci_f, ci_b, q_s, k_s, v_t, rows_ref, cc_ref, hf, hb):
        for reverse, ci, hout in ((False, ci_f, hf), (True, ci_b, hb)):
            rows = pl.ds(pl.multiple_of(ci * tt, tt), tt)
            gates = rows_ref[ci]
            cc = cc_ref[rows, :]
            for j, (win, _, free) in enumerate(wins):
                irow = (2 * N_HEADS if reverse else 0) + pair * 2 + j
                lane_c = 2 * j + (1 if reverse else 0)
                chain = j * 2 + (1 if reverse else 0)
                ht = _mlstm_chunk(q_s[rows, win], k_s[j, rows, :], v_t[j, ci], free,
                                  pick_r(gates, irow + N_HEADS), pick_r(gates, irow),
                                  cc[:, lane_c:lane_c + 1], ct_scr.at[chain], m_scr.at[chain], reverse)
                hout[j, rows, :] = ht.T

    _scan_loops(qc_ref.shape[0] // tt, qx_ref.shape[0] // tt,
                lambda f, b: run(f, b, qsc, ksc, vtc, rwc_ref, ccc_ref, hfc, hbc),
                lambda f, b: run(f, b, qsx, ksx, vtx, rwx_ref, ccx_ref, hfx, hbx), unroll)

    def finish(hf, hb, o_ref, y_ref):
        def block(rows):
            hn = _pair_finish(hf, hb, rows, head_dim, mask_invalid=True)
            y_ref[rows, :] = (hn * ng_ref[...] * _sigmoid(o_ref[rows, :].astype(F32))).astype(BF16)

        _row_blocks(o_ref.shape[0], tt, block)

    finish(hfx, hbx, ox_ref, yx_ref)
    if write_ctx:
        finish(hfc, hbc, oc_ref, yc_ref)
    else:
        yc_ref[...] = jnp.zeros_like(yc_ref)


def _mlstm(px, pc, gx, gc, seq, seq_c, col0, conv_w, gate_b, norm_g, head_dim, write_ctx):
    batch = px.shape[0] // seq
    total = N_HEADS * head_dim
    npair = total // PAIR_W
    cb = col0 // PAIR_W
    tt = MXU_W if (seq % MXU_W == 0 and seq_c % MXU_W == 0) else CHUNK
    rows_x, cc_x = _mlstm_gates(gx, seq, tt, gate_b)
    rows_c, cc_c = _mlstm_gates(gc, seq_c, tt, gate_b)

    def pspecs(s):
        return [pl.BlockSpec((s, PAIR_W), lambda b, h, k=k: (b, cb + k * npair + h)) for k in range(4)]

    def gspecs(s):
        return [pl.BlockSpec((s // tt, N_GATES, tt), lambda b, h: (b, 0, 0)),
                pl.BlockSpec((s, LANE), lambda b, h: (b, h))]

    scr = lambda s, dt: pltpu.VMEM((2, s, WIN), dt)
    scr_t = lambda s: pltpu.VMEM((2, s // tt, WIN, tt), BF16)
    scr_q = lambda s: pltpu.VMEM((s, PAIR_W), BF16)
    scr_pad = lambda s: pltpu.VMEM((s + 2 * BF16_ROWS, PAIR_W), BF16)
    kern = functools.partial(_mlstm_kernel, head_dim=head_dim, write_ctx=write_ctx, unroll=2)
    return pl.pallas_call(
        kern,
        out_shape=(jax.ShapeDtypeStruct((batch * seq, total), BF16),
                   jax.ShapeDtypeStruct((batch * seq_c, total), BF16)),
        grid=(batch, npair),
        in_specs=(pspecs(seq) + gspecs(seq) + pspecs(seq_c) + gspecs(seq_c) + [
            pl.BlockSpec((3, PAIR_W), lambda b, h: (0, h)),
            pl.BlockSpec((3, PAIR_W), lambda b, h: (0, npair + h)),
            pl.BlockSpec((1, PAIR_W), lambda b, h: (0, h)),
        ]),
        out_specs=(pl.BlockSpec((seq, PAIR_W), lambda b, h: (b, h)),
                   pl.BlockSpec((seq_c, PAIR_W), lambda b, h: (b, h))),
        scratch_shapes=[scr_q(seq), scr(seq, BF16), scr_t(seq),
                        scr_q(seq_c), scr(seq_c, BF16), scr_t(seq_c),
                        scr(seq, F32), scr(seq, F32), scr(seq_c, F32), scr(seq_c, F32),
                        scr_pad(seq), scr_pad(seq), scr_pad(seq_c), scr_pad(seq_c),
                        pltpu.VMEM((4, WIN, WIN), F32),
                        pltpu.VMEM((4, 1, 1), F32)],
        compiler_params=_cparams(("arbitrary", "arbitrary")),
        name="mlstm",
    )(px, px, px, px, rows_x, cc_x, pc, pc, pc, pc, rows_c, cc_c,
      conv_w, conv_w, norm_g.reshape(1, total))


def _ret_chunk(q, k, v, dmask, q_decay, k_decay, c_decay, s_ref):
    pv = _dot((_dot_nt(q, k) * dmask).astype(BF16), v)
    kd = (k.astype(F32) * k_decay).astype(BF16)
    s_prev = s_ref[...]
    s_ref[...] = c_decay * s_prev + _dot_tn(kd, v)
    return pv + q_decay * _dot(q, s_prev.astype(BF16))


def _ret_kernel(ld_ref, qx_ref, kx_ref, vx_ref, gx_ref, qc_ref, kc_ref, vc_ref, gc_ref, cos_ref, sgn_ref,
                yx_ref, yc_ref,
                qsx, ksx, vsx, qsc, ksc, vsc, ofx, obx, ofc, obc, s_scr,
                *, key_dim, val_dim, write_ctx, unroll, t):
    pair = pl.program_id(1)
    qscale = key_dim ** -0.5
    half = key_dim // 2
    vwins = _head_windows(val_dim)
    lane = lax.broadcasted_iota(jnp.int32, (1, WIN), 1)
    klane = lax.broadcasted_iota(jnp.int32, (1, qx_ref.shape[1]), 1)
    first_half = (klane % key_dim) < half

    def rope(a, rows):
        width = a.shape[1]
        partner = jnp.where(first_half, pltpu.roll(a, width - half, axis=1), pltpu.roll(a, half, axis=1))
        return a * cos_ref[rows, :] + partner * sgn_ref[rows, :]

    def prep(q_ref, k_ref, v_ref, qs, ks, vs, rotary):
        def block(rows):
            qa, ka = q_ref[rows, :].astype(F32), k_ref[rows, :].astype(F32)
            if rotary:
                qa, ka = rope(qa, rows), rope(ka, rows)
            qs[rows, :] = (jnp.where(pair == 0, qa[:, :WIN], qa[:, LANE:LANE + WIN]) * qscale).astype(BF16)
            kw = jnp.where(pair == 0, ka[:, :WIN], ka[:, LANE:LANE + WIN])
            for j, (win, valid, _) in enumerate(vwins):
                start = (2 * pair + j) * key_dim - LANE * pair
                kvalid = jnp.logical_and(lane >= start, lane < start + key_dim)
                ks[j, rows, :] = jnp.where(kvalid, kw, 0.0).astype(BF16)
                vs[j, rows, :] = jnp.where(valid, v_ref[rows, win].astype(F32), 0.0).astype(BF16)

        _row_blocks(q_ref.shape[0], t, block)

    prep(qx_ref, kx_ref, vx_ref, qsx, ksx, vsx, True)
    prep(qc_ref, kc_ref, vc_ref, qsc, ksc, vsc, False)
    s_scr[...] = jnp.zeros_like(s_scr)

    r = lax.broadcasted_iota(jnp.int32, (t, t), 0)
    c = lax.broadcasted_iota(jnp.int32, (t, t), 1)
    idx = lax.broadcasted_iota(jnp.int32, (t, 1), 0).astype(F32)
    consts = []
    for j in range(2):
        for reverse in (False, True):
            lg = -jnp.exp(jnp.full((1, 1), ld_ref[1 if reverse else 0, pair * 2 + j], F32))
            rel = (c - r) if reverse else (r - c)
            dmask = jnp.where(rel >= 0, jnp.exp(lg * jnp.maximum(rel, 0).astype(F32)), 0.0)
            pos = (t - 1.0 - idx) if reverse else idx
            consts.append((dmask, jnp.exp(lg * (pos + 1.0)), jnp.exp(lg * (t - 1.0 - pos)), jnp.exp(lg * t)))

    def run(ci_f, ci_b, q_s, k_s, v_s, of, ob):
        for reverse, ci, oout in ((False, ci_f, of), (True, ci_b, ob)):
            rows = pl.ds(pl.multiple_of(ci * t, t), t)
            for j in range(2):
                chain = j * 2 + (1 if reverse else 0)
                dmask, q_decay, k_decay, c_decay = consts[chain]
                oout[j, rows, :] = _ret_chunk(q_s[rows, :], k_s[j, rows, :], v_s[j, rows, :],
                                              dmask, q_decay, k_decay, c_decay, s_scr.at[chain])

    _scan_loops(qc_ref.shape[0] // t, qx_ref.shape[0] // t,
                lambda f, b: run(f, b, qsc, ksc, vsc, ofc, obc),
                lambda f, b: run(f, b, qsx, ksx, vsx, ofx, obx), unroll)

    def finish(of, ob, g_ref, y_ref):
        def block(rows):
            on = _pair_finish(of, ob, rows, val_dim, mask_invalid=False)
            y_ref[rows, :] = (on * _silu(g_ref[rows, :].astype(F32))).astype(BF16)

        _row_blocks(g_ref.shape[0], t, block)

    finish(ofx, obx, gx_ref, yx_ref)
    if write_ctx:
        finish(ofc, obc, gc_ref, yc_ref)
    else:
        yc_ref[...] = jnp.zeros_like(yc_ref)


def _retention(px, pc, seq, seq_c, col0, log_decay, cosf, sgnf, key_dim, val_dim, write_ctx):
    batch = px.shape[0] // seq
    ktot, vtot = N_HEADS * key_dim, N_HEADS * val_dim
    assert ktot == PAIR_W
    npair = vtot // PAIR_W
    qb = col0 // PAIR_W
    vb = (col0 + 2 * ktot) // PAIR_W

    def pspecs(s):
        return [pl.BlockSpec((s, PAIR_W), lambda b, h: (b, qb)),
                pl.BlockSpec((s, PAIR_W), lambda b, h: (b, qb + 1)),
                pl.BlockSpec((s, PAIR_W), lambda b, h: (b, vb + h)),
                pl.BlockSpec((s, PAIR_W), lambda b, h: (b, vb + npair + h))]

    scr = lambda s, dt: pltpu.VMEM((2, s, WIN), dt)
    t = MXU_W if (seq % MXU_W == 0 and seq_c % MXU_W == 0) else CHUNK
    kern = functools.partial(_ret_kernel, key_dim=key_dim, val_dim=val_dim, write_ctx=write_ctx, unroll=2, t=t)
    return pl.pallas_call(
        kern,
        out_shape=(jax.ShapeDtypeStruct((batch * seq, vtot), BF16),
                   jax.ShapeDtypeStruct((batch * seq_c, vtot), BF16)),
        grid=(batch, npair),
        in_specs=([pl.BlockSpec(memory_space=pltpu.SMEM)] + pspecs(seq) + pspecs(seq_c) + [
            _resident((seq, PAIR_W), lambda b, h: (0, 0)),
            _resident((seq, PAIR_W), lambda b, h: (0, 0)),
        ]),
        out_specs=(pl.BlockSpec((seq, PAIR_W), lambda b, h: (b, h)),
                   pl.BlockSpec((seq_c, PAIR_W), lambda b, h: (b, h))),
        scratch_shapes=[pltpu.VMEM((seq, WIN), BF16), scr(seq, BF16), scr(seq, BF16),
                        pltpu.VMEM((seq_c, WIN), BF16), scr(seq_c, BF16), scr(seq_c, BF16),
                        scr(seq, F32), scr(seq, F32), scr(seq_c, F32), scr(seq_c, F32),
                        pltpu.VMEM((4, WIN, WIN), F32)],
        compiler_params=_cparams(("arbitrary", "arbitrary")),
        name="retention",
    )(log_decay.astype(F32), px, px, px, px, pc, pc, pc, pc, cosf, sgnf)


@functools.lru_cache(maxsize=None)
def _rope_tables(seq, key_dim):
    rows = seq // GRID_W
    r = np.repeat(np.arange(rows, dtype=np.float32), GRID_W)
    col = np.tile(np.arange(GRID_W, dtype=np.float32), rows)
    nf = key_dim // 4
    inv = (np.float32(ROPE_BASE) ** (-np.arange(nf, dtype=np.float32) / np.float32(nf))).astype(np.float32)
    ang = np.concatenate([r[:, None] * inv, col[:, None] * inv], axis=-1).astype(np.float32)
    cos, sin = np.cos(ang).astype(np.float32), np.sin(ang).astype(np.float32)
    return np.tile(np.concatenate([cos, cos], axis=1), (1, N_HEADS)), np.tile(np.concatenate([-sin, sin], axis=1), (1, N_HEADS))


def kernel(x, c, ctx, c_ctx, norm1_g, norm2_g, w_mod, b_mod, w_in, hy_conv_w, hy_f_w1, hy_f_b1, hy_f_w2, hy_f_b2, hy_f_w3, hy_f_freq, hy_bias, ml_conv_w, ml_gate_b, ml_norm_g, rt_log_decay, w_out, w_ff1, w_ff2, final_g):
    batch, seq, d = x.shape
    seq_c = ctx.shape[1]
    depth = w_mod.shape[0]
    hy_w = d // 4
    ml_w = 3 * d // 8
    rt_w = 3 * d // 8
    ml_dh = ml_w // N_HEADS
    rt_dv = rt_w // N_HEADS
    rt_dk = rt_dv // 2
    hy_cols, ml_cols = 3 * hy_w, 4 * ml_w
    gate0 = hy_cols + ml_cols
    rt0 = gate0 + N_GATES

    s_in = jnp.concatenate([c, c_ctx[None], jnp.zeros((MOD_ROWS - batch - 1, d), F32)], axis=0)
    mod = _modulation(s_in, w_mod, b_mod)

    grp_x = lambda tm: (lambda i: (i * tm) // seq)
    grp_c = lambda tm: (lambda i: batch + 0 * i)
    tm_x, tm_c = _pick_tile(seq, 1024), _pick_tile(batch * seq_c, 1024)
    tm_ox, tm_oc = _pick_tile(seq, 512), _pick_tile(batch * seq_c, 512)

    def dft(n):
        cm, sm, ct, st = _dft_tables(n)
        return jnp.asarray(cm).astype(BF16), jnp.asarray(sm).astype(BF16), ct, st

    cm_x, sm_x, ct_x, st_x = dft(seq)
    cm_c, sm_c, ct_c, st_c = dft(seq_c)
    cosf, sgnf = _rope_tables(seq, rt_dk)
    tc = _pick_tile(hy_w, MXU_W)
    tc_spec = _pick_tile(hy_w, LANE)

    xs = x.reshape(batch * seq, d)
    hc = ctx.reshape(batch * seq_c, d)
    wt = jnp.swapaxes(w_in, 1, 2)
    n_p = -(-(wt.shape[1] - N_GATES) // 1024) * 1024
    w_pt = jnp.concatenate([wt[:, :gate0], wt[:, rt0:],
                            jnp.zeros((depth, n_p - (wt.shape[1] - N_GATES), d), F32)], axis=1).astype(BF16)
    w_gt = jnp.pad(wt[:, gate0:rt0], ((0, 0), (0, LANE - N_GATES), (0, 0))).astype(BF16)
    ml_col0, rt_col0 = hy_cols, gate0
    wo = w_out.astype(BF16)
    w1 = w_ff1.astype(BF16)
    w2 = w_ff2.astype(BF16)

    for l in range(depth):
        need_ctx = l < depth - 1
        mod3 = mod[l].reshape(MOD_ROWS * N_MOD, 1, d)

        px, gx = _in_proj(xs, norm1_g[l], mod3, grp_x(tm_x), w_pt, w_gt, l, tm_x)
        pc, gc = _in_proj(hc, norm1_g[l], mod3, grp_c(tm_c), w_pt, w_gt, l, tm_c)

        filt = (hy_f_w1[l], hy_f_b1[l], hy_f_w2[l], hy_f_b2[l], hy_f_w3[l], hy_f_freq[l])
        hf, hb = _hyena_filter(seq, *filt)
        hre, him = _hyena_spectrum(hf, hb, cm_x, sm_x, ct_x, st_x, tc_spec)
        y_hy = _hyena_conv(px, seq, hy_conv_w[l], hre, him, hy_bias[l], cm_x, sm_x, tc)
        y_ml, yc_ml = _mlstm(px, pc, gx, gc, seq, seq_c, ml_col0, ml_conv_w[l], ml_gate_b[l], ml_norm_g[l],
                             ml_dh, need_ctx)
        y_rt, yc_rt = _retention(px, pc, seq, seq_c, rt_col0, rt_log_decay[l], cosf, sgnf,
                                 rt_dk, rt_dv, need_ctx)
        xs, h2 = _out_proj(xs, y_hy, y_ml, y_rt, wo, l, norm2_g[l], mod3, grp_x(tm_ox), tm_ox, alias=l > 0)
        if need_ctx:
            hfc, hbc = _hyena_filter(seq_c, *filt)
            hre_c, him_c = _hyena_spectrum(hfc, hbc, cm_c, sm_c, ct_c, st_c, tc_spec)
            yc_hy = _hyena_conv(pc, seq_c, hy_conv_w[l], hre_c, him_c, hy_bias[l], cm_c, sm_c, tc)
            hc, h2c = _out_proj(hc, yc_hy, yc_ml, yc_rt, wo, l, norm2_g[l], mod3, grp_c(tm_oc), tm_oc, alias=l > 0)
            hc = _ffn(hc, h2c, mod3, grp_c(tm_c), w1, w2, l, tm_c, 1024)
        xs = _ffn(xs, h2, mod3, grp_x(tm_x), w1, w2, l, tm_x, 1024)
    return _final_norm(xs, final_g, tm_ox).reshape(batch, seq, d)
```

```python
import functools
import math

import jax
import jax.numpy as jnp
import numpy as np
from jax import lax
from jax.experimental import pallas as pl
from jax.experimental.pallas import tpu as pltpu

F32 = jnp.float32
BF16 = jnp.bfloat16

GRID_W = 64
HY_EMB = 33
HY_FF = 64
HY_DECAY_TARGET = 1e-2
HY_SHORT_PCT = 0.3
HY_LONG_PCT = 1.5
N_HEADS = 4
N_GATES = 4 * N_HEADS
CHUNK = 128
ROPE_BASE = 10000.0
N_MOD = 6
EPS = 1e-6

LANE = 128
MXU_W = 256
WIN = 2 * LANE
PAIR_W = 3 * LANE
MOD_ROWS = 16
VMEM_LIMIT = 56 * 1024 * 1024


def _cparams(sem):
    return pltpu.CompilerParams(dimension_semantics=sem, vmem_limit_bytes=VMEM_LIMIT)


def _sigmoid(x):
    return 0.5 * jnp.tanh(0.5 * x) + 0.5


def _silu(x):
    return x * _sigmoid(x)


def _log_sigmoid(x):
    return jnp.minimum(x, 0.0) - jnp.log(1.0 + jnp.exp(-jnp.abs(x)))


def _dot(a, b):
    return jnp.dot(a, b, preferred_element_type=F32)


def _dot_nt(a, b):
    return lax.dot_general(a, b, (((1,), (1,)), ((), ())), preferred_element_type=F32)


def _dot_tn(a, b):
    return lax.dot_general(a, b, (((0,), (0,)), ((), ())), preferred_element_type=F32)


def _split_bf16(a):
    hi = a.astype(BF16)
    lo = (a - hi.astype(F32)).astype(BF16)
    return hi, lo


def _dot_split(a, b_f32):
    hi, lo = _split_bf16(b_f32)
    return _dot(a, hi) + _dot(a, lo)


def _resident(shape, index_map):
    return pl.BlockSpec(shape, index_map, pipeline_mode=pl.Buffered(1))


def _pick_tile(n, target):
    t = min(n, target)
    while n % t:
        t //= 2
    return t


def _norm_mod(x, g, shift, scale):
    ms = jnp.mean(x * x, axis=-1, keepdims=True)
    y = x * lax.rsqrt(ms + EPS) * g
    return y * (1.0 + scale) + shift


def _mod_kernel(s_ref, w_ref, b_ref, o_ref):
    s = _silu(s_ref[...]).astype(BF16)
    w = w_ref[0].astype(BF16)
    o_ref[0] = _dot(s, w) + b_ref[0]


def _modulation(s_in, w_mod, b_mod):
    depth, d, n = w_mod.shape
    tn = _pick_tile(n, 1024)
    return pl.pallas_call(
        _mod_kernel,
        out_shape=jax.ShapeDtypeStruct((depth, MOD_ROWS, n), F32),
        grid=(depth, n // tn),
        in_specs=[
            pl.BlockSpec((MOD_ROWS, d), lambda l, j: (0, 0)),
            pl.BlockSpec((1, d, tn), lambda l, j: (l, 0, j)),
            pl.BlockSpec((1, 1, tn), lambda l, j: (l, 0, j)),
        ],
        out_specs=pl.BlockSpec((1, MOD_ROWS, tn), lambda l, j: (l, 0, j)),
        compiler_params=_cparams(("arbitrary", "arbitrary")),
        name="modulation",
    )(s_in, w_mod, b_mod.reshape(depth, 1, n))


def _inproj_kernel(x_ref, g_ref, shift_ref, scale_ref, wt_ref, wgt_ref, p_ref, gate_ref, h_scr):
    @pl.when(pl.program_id(1) == 0)
    def _():
        h = _norm_mod(x_ref[...], g_ref[...], shift_ref[0], scale_ref[0]).astype(BF16)
        h_scr[...] = h
        gate_ref[...] = _dot_nt(h, wgt_ref[...])

    p_ref[...] = _dot_nt(h_scr[...], wt_ref[...]).astype(BF16)


def _in_proj(x2, norm_g, mod3, group_of_tile, w_pt, w_gt, layer, tm, skip0, skip1):
    m, d = x2.shape
    n0 = w_pt.shape[1]
    n = n0 - (skip1 - skip0)
    tn = 3 * MXU_W
    assert skip0 % tn == 0 and n % tn == 0 and skip1 % BF16_ROWS == 0
    n_before = skip0 // tn

    def w_row(j):
        return j * tn + (j >= n_before).astype(jnp.int32) * (skip1 - skip0)

    return pl.pallas_call(
        _inproj_kernel,
        out_shape=(jax.ShapeDtypeStruct((m, n), BF16), jax.ShapeDtypeStruct((m, LANE), F32)),
        grid=(m // tm, n // tn),
        in_specs=[
            pl.BlockSpec((tm, d), lambda i, j: (i, 0)),
            pl.BlockSpec((1, d), lambda i, j: (0, 0)),
            pl.BlockSpec((1, 1, d), lambda i, j: (group_of_tile(i) * N_MOD + 0, 0, 0)),
            pl.BlockSpec((1, 1, d), lambda i, j: (group_of_tile(i) * N_MOD + 1, 0, 0)),
            pl.BlockSpec((pl.Element(tn), pl.Element(d)),
                         lambda i, j: (pl.multiple_of(layer * n0 + w_row(j), BF16_ROWS), 0)),
            pl.BlockSpec((None, LANE, d), lambda i, j: (layer, 0, 0)),
        ],
        out_specs=(pl.BlockSpec((tm, tn), lambda i, j: (i, j)),
                   pl.BlockSpec((tm, LANE), lambda i, j: (i, 0))),
        scratch_shapes=[pltpu.VMEM((tm, d), BF16)],
        compiler_params=_cparams(("arbitrary", "arbitrary")),
        name="in_proj",
    )(x2, norm_g.reshape(1, d), mod3, mod3, w_pt.reshape(-1, d), w_gt)


def _outproj_kernel(x_ref, yh_ref, ym_ref, yr_ref, w_ref, gate_ref,
                    g2_ref, shift_ref, scale_ref, o_ref, h2_ref):
    kh, km = yh_ref.shape[1], ym_ref.shape[1]
    acc = _dot(yh_ref[...], w_ref[0:kh, :])
    acc = acc + _dot(ym_ref[...], w_ref[kh:kh + km, :])
    acc = acc + _dot(yr_ref[...], w_ref[kh + km:, :])
    xn = x_ref[...] + gate_ref[0] * acc
    o_ref[...] = xn
    h2_ref[...] = _norm_mod(xn, g2_ref[...], shift_ref[0], scale_ref[0]).astype(BF16)


def _out_proj(x2, y_hy, y_ml, y_rt, wo, layer, norm2_g, mod3, group_of_tile, tm, alias):
    m, d = x2.shape
    kh, km, kr = y_hy.shape[1], y_ml.shape[1], y_rt.shape[1]
    mrow = lambda k: pl.BlockSpec((1, 1, d), lambda i: (group_of_tile(i) * N_MOD + k, 0, 0))
    return pl.pallas_call(
        _outproj_kernel,
        out_shape=(jax.ShapeDtypeStruct((m, d), F32), jax.ShapeDtypeStruct((m, d), BF16)),
        grid=(m // tm,),
        in_specs=[
            pl.BlockSpec((tm, d), lambda i: (i, 0)),
            pl.BlockSpec((tm, kh), lambda i: (i, 0)),
            pl.BlockSpec((tm, km), lambda i: (i, 0)),
            pl.BlockSpec((tm, kr), lambda i: (i, 0)),
            _resident((None, kh + km + kr, d), lambda i: (layer, 0, 0)),
            mrow(2),
            pl.BlockSpec((1, d), lambda i: (0, 0)),
            mrow(3), mrow(4),
        ],
        out_specs=(pl.BlockSpec((tm, d), lambda i: (i, 0)), pl.BlockSpec((tm, d), lambda i: (i, 0))),
        input_output_aliases=({0: 0} if alias else {}),
        compiler_params=_cparams(("arbitrary",)),
        name="out_proj",
    )(x2, y_hy, y_ml, y_rt, wo, mod3, norm2_g.reshape(1, d), mod3, mod3)


def _ffn_kernel(h_ref, x_ref, gate_ref, w1_ref, w2_ref, o_ref, hid_scr, *, n1):
    j = pl.program_id(1)
    tf = w1_ref.shape[1]

    @pl.when(j < n1)
    def _():
        a = jnp.maximum(_dot(h_ref[...], w1_ref[...]), 0.0)
        hid_scr[jnp.minimum(j, n1 - 1)] = (a * a).astype(BF16)

    @pl.when(j >= n1)
    def _():
        acc = _dot(hid_scr[0], w2_ref[0:tf, :])
        for c in range(1, n1):
            acc = acc + _dot(hid_scr[c], w2_ref[c * tf:(c + 1) * tf, :])
        o_ref[...] = x_ref[...] + gate_ref[0] * acc


def _ffn(x2, h2, mod3, group_of_tile, w1, w2, layer, tm, tf):
    m, d = x2.shape
    f = w1.shape[2]
    n1 = f // tf
    tn = MXU_W
    n2 = d // tn
    col = lambda j: jnp.maximum(j - n1, 0)
    return pl.pallas_call(
        functools.partial(_ffn_kernel, n1=n1),
        out_shape=jax.ShapeDtypeStruct((m, d), F32),
        grid=(m // tm, n1 + n2),
        in_specs=[
            pl.BlockSpec((tm, d), lambda i, j: (i, 0)),
            pl.BlockSpec((tm, tn), lambda i, j: (i, col(j))),
            pl.BlockSpec((1, 1, tn), lambda i, j: (group_of_tile(i) * N_MOD + 5, 0, col(j))),
            pl.BlockSpec((None, d, tf), lambda i, j: (layer, 0, jnp.minimum(j, n1 - 1))),
            pl.BlockSpec((None, f, tn), lambda i, j: (layer, 0, col(j))),
        ],
        out_specs=pl.BlockSpec((tm, tn), lambda i, j: (i, col(j))),
        scratch_shapes=[pltpu.VMEM((n1, tm, tf), BF16)],
        compiler_params=_cparams(("arbitrary", "arbitrary")),
        name="ffn",
    )(h2, x2, mod3, w1, w2)


def _final_norm_kernel(x_ref, g_ref, o_ref):
    x = x_ref[...]
    ms = jnp.mean(x * x, axis=-1, keepdims=True)
    o_ref[...] = x * lax.rsqrt(ms + EPS) * g_ref[...]


def _final_norm(x2, g, tm):
    m, d = x2.shape
    return pl.pallas_call(
        _final_norm_kernel,
        out_shape=jax.ShapeDtypeStruct((m, d), F32),
        grid=(m // tm,),
        in_specs=[pl.BlockSpec((tm, d), lambda i: (i, 0)), pl.BlockSpec((1, d), lambda i: (0, 0))],
        out_specs=pl.BlockSpec((tm, d), lambda i: (i, 0)),
        compiler_params=_cparams(("arbitrary",)),
        name="final_norm",
    )(x2, g.reshape(1, d))


def _hy_filter_kernel(z_ref, t_ref, w1_ref, b1_ref, w2_ref, b2_ref, w3f_ref, w3b_ref, fr_ref, dl_ref,
                      hf_ref, hb_ref):
    hp = lax.Precision.HIGHEST
    dot = lambda a, b: jnp.dot(a, b, precision=hp, preferred_element_type=F32)
    fr = fr_ref[...]
    hdn = jnp.sin(fr * (dot(z_ref[...], w1_ref[...]) + b1_ref[...]))
    hdn = jnp.sin(fr * (dot(hdn, w2_ref[...]) + b2_ref[...]))
    win = jnp.exp(-t_ref[...] * dl_ref[...])
    hf = dot(hdn, w3f_ref[...]) * win
    hb = dot(hdn, w3b_ref[...]) * win
    row = lax.broadcasted_iota(jnp.int32, hb.shape, 0)
    hb = jnp.where(row == 0, 0.0, hb)
    inv = 1.0 / (jnp.sum(jnp.abs(hf), axis=0, keepdims=True) + jnp.sum(jnp.abs(hb), axis=0, keepdims=True))
    hf_ref[...] = hf * inv
    hb_ref[...] = hb * inv


@functools.lru_cache(maxsize=None)
def _filter_tables(seq, hy_w):
    t = np.linspace(0.0, 1.0, seq, dtype=np.float32)[:, None]
    w = (np.float32(2.0 * math.pi / seq) * np.arange(seq, dtype=np.float32))[:, None]
    nb = (HY_EMB - 1) // 2
    bands = np.linspace(1e-4, nb - 1, nb, dtype=np.float32)[None, :]
    z = np.concatenate([t, np.cos(bands * w), -np.sin(bands * w)], axis=-1).astype(np.float32)
    z = np.pad(z, ((0, 0), (0, LANE - HY_EMB)))
    deltas = np.abs(np.linspace(math.log(HY_DECAY_TARGET) / HY_LONG_PCT,
                                math.log(HY_DECAY_TARGET) / HY_SHORT_PCT, hy_w, dtype=np.float32))[None, :]
    return z, t, deltas


def _hyena_filter(seq, w1, b1, w2, b2, w3, freq):
    hy_w = w3.shape[1] // 2
    z, t, deltas = _filter_tables(seq, hy_w)
    pad_ff = LANE - HY_FF
    w1p = jnp.pad(w1, ((0, LANE - HY_EMB), (0, pad_ff)))
    w2p = jnp.pad(w2, ((0, pad_ff), (0, pad_ff)))
    w3p = jnp.pad(w3, ((0, pad_ff), (0, 0)))
    padv = lambda v: jnp.pad(v, (0, pad_ff)).reshape(1, LANE)
    out = jax.ShapeDtypeStruct((seq, hy_w), F32)
    return pl.pallas_call(
        _hy_filter_kernel,
        out_shape=(out, out),
        compiler_params=pltpu.CompilerParams(vmem_limit_bytes=VMEM_LIMIT),
        name="hyena_filter",
    )(z, t, w1p, padv(b1), w2p, padv(b2), w3p[:, :hy_w], w3p[:, hy_w:], padv(freq), deltas)


@functools.lru_cache(maxsize=None)
def _dft_tables(seq):
    k = np.arange(seq, dtype=np.int64)
    mm = ((2 * k[:, None] + 1) * (2 * k[None, :] + 1)) % (8 * seq)
    ang = mm.astype(np.float64) * (2.0 * math.pi / (8 * seq))
    theta = (k.astype(np.float64) + 0.5) * (math.pi / (2 * seq))
    return (np.cos(ang).astype(np.float32), np.sin(ang).astype(np.float32),
            np.cos(theta).astype(np.float32)[:, None], np.sin(theta).astype(np.float32)[:, None])


def _hy_spec_kernel(hf_ref, hb_ref, c_ref, s_ref, ct_ref, st_ref, hre_ref, him_ref):
    cm, sm = c_ref[...], s_ref[...]
    seq = cm.shape[0]
    tc = hf_ref.shape[1]
    taps = jnp.concatenate([hf_ref[...], hb_ref[...]], axis=1)
    ca, sa = _dot_split(cm, taps), _dot_split(sm, taps)
    a, a2 = ca[:, :tc], ca[:, tc:]
    b, b2 = sa[:, :tc], sa[:, tc:]
    ct, st = ct_ref[...], st_ref[...]
    scale = 1.0 / seq
    hre_ref[...] = ((a + a2) * ct + (b + b2) * st) * scale
    him_ref[...] = ((a - a2) * st - (b - b2) * ct) * scale


def _hyena_spectrum(hf, hb, cm, sm, ct, st, tc):
    seq, c = hf.shape
    out = jax.ShapeDtypeStruct((seq, c), F32)
    col = pl.BlockSpec((seq, tc), lambda j: (0, j))
    return pl.pallas_call(
        _hy_spec_kernel,
        out_shape=(out, out),
        grid=(c // tc,),
        in_specs=[col, col,
                  _resident((seq, seq), lambda j: (0, 0)), _resident((seq, seq), lambda j: (0, 0)),
                  pl.BlockSpec((seq, 1), lambda j: (0, 0)), pl.BlockSpec((seq, 1), lambda j: (0, 0))],
        out_specs=(col, col),
        compiler_params=_cparams(("arbitrary",)),
        name="hyena_spectrum",
    )(hf, hb, cm, sm, ct, st)


def _short_conv(p, w):
    n = p.shape[0]
    pz = jnp.concatenate([p, jnp.zeros((8, p.shape[1]), p.dtype)], axis=0)
    prev = pltpu.roll(pz, 1, axis=0)[:n]
    nxt = pltpu.roll(pz, n + 7, axis=0)[:n]
    return prev * w[0:1] + p * w[1:2] + nxt * w[2:3]


def _hy_conv_kernel(p0_ref, p1_ref, pv_ref, w0_ref, w1_ref, wv_ref, hre_ref, him_ref, bias_ref,
                    c_ref, s_ref, o_ref):
    x1 = _short_conv(p1_ref[...].astype(F32), w1_ref[...])
    v = _short_conv(pv_ref[...].astype(F32), wv_ref[...])
    u = v * x1
    ub = u.astype(BF16)
    cm, sm = c_ref[...], s_ref[...]
    zc = _dot(cm, ub)
    zs = _dot(sm, ub)
    hre, him = hre_ref[...], him_ref[...]
    yre = (hre * zc + him * zs).astype(BF16)
    yim = (him * zc - hre * zs).astype(BF16)
    y = _dot(cm, yre) - _dot(sm, yim)
    x0 = _short_conv(p0_ref[...].astype(F32), w0_ref[...])
    o_ref[...] = ((y + u * bias_ref[...]) * x0).astype(BF16)


def _hyena_conv(p, seq, conv_w, hre, him, bias, cm, sm, tc):
    m = p.shape[0]
    hy_w = hre.shape[1]
    nb = hy_w // tc
    pspec = lambda off: pl.BlockSpec((seq, tc), lambda j, b, off=off: (b, off * nb + j))
    wspec = lambda off: pl.BlockSpec((3, tc), lambda j, b, off=off: (0, off * nb + j))
    col = pl.BlockSpec((seq, tc), lambda j, b: (0, j))
    return pl.pallas_call(
        _hy_conv_kernel,
        out_shape=jax.ShapeDtypeStruct((m, hy_w), BF16),
        grid=(nb, m // seq),
        in_specs=[pspec(0), pspec(1), pspec(2), wspec(0), wspec(1), wspec(2), col, col,
                  pl.BlockSpec((1, tc), lambda j, b: (0, j)),
                  _resident((seq, seq), lambda j, b: (0, 0)), _resident((seq, seq), lambda j, b: (0, 0))],
        out_specs=pl.BlockSpec((seq, tc), lambda j, b: (b, j)),
        compiler_params=_cparams(("arbitrary", "arbitrary")),
        name="hyena_conv",
    )(p, p, p, conv_w, conv_w, conv_w, hre, him, bias.reshape(1, hy_w), cm, sm)


def _tri(n, upper):
    r = lax.broadcasted_iota(jnp.int32, (n, n), 0)
    c = lax.broadcasted_iota(jnp.int32, (n, n), 1)
    return (r <= c) if upper else (r >= c)


def _head_windows(head_dim):
    assert 2 * head_dim == PAIR_W
    lane = lax.broadcasted_iota(jnp.int32, (1, WIN), 1)
    return ((slice(0, WIN), lane < head_dim, head_dim),
            (slice(LANE, LANE + WIN), lane >= WIN - head_dim, 0))


def _pair_finish(acc_f, acc_b, rows, head_dim, mask_invalid):
    outs = []
    for j, (_, valid, _) in enumerate(_head_windows(head_dim)):
        h = acc_f[j, rows, :] + acc_b[j, rows, :]
        if mask_invalid:
            h = jnp.where(valid, h, 0.0)
        ms = jnp.sum(h * h, axis=-1, keepdims=True) * (1.0 / head_dim)
        outs.append(h * lax.rsqrt(ms + EPS))
    h0, h1 = outs
    return jnp.concatenate([h0[:, :LANE], h0[:, LANE:] + h1[:, :LANE], h1[:, LANE:]], axis=1)


def _row_blocks(n, blk, body):
    def step(i, carry):
        body(pl.ds(pl.multiple_of(i * blk, blk), blk))
        return carry

    lax.fori_loop(0, n // blk, step, 0)


BF16_ROWS = 16


def _conv_window(w, cw):
    n = w.shape[0]
    u = pltpu.roll(w, 1, axis=0) * cw[0:1] + w * cw[1:2] + pltpu.roll(w, n - 1, axis=0) * cw[2:3]
    return u[BF16_ROWS:n - BF16_ROWS]


def _fill_padded(pad_ref, src_ref):
    n = src_ref.shape[0]
    zero = jnp.zeros((BF16_ROWS, src_ref.shape[1]), src_ref.dtype)
    pad_ref[0:BF16_ROWS, :] = zero
    pad_ref[BF16_ROWS + n:2 * BF16_ROWS + n, :] = zero
    pad_ref[BF16_ROWS:BF16_ROWS + n, :] = src_ref[...]


def _scan_loops(nc_c, nc_x, run_c, run_x, unroll):
    def loop(nc, run):
        u = unroll if nc % unroll == 0 else 1

        def body(i, carry):
            for k in range(u):
                run(i * u + k, nc - 1 - (i * u + k))
            return carry

        lax.fori_loop(0, nc // u, body, 0)

    loop(nc_c, run_c)
    loop(nc_x, run_x)


@functools.lru_cache(maxsize=None)
def _gate_select_matrix():
    e = np.zeros((2 * LANE, 2 * LANE), np.float32)
    for pair in range(N_HEADS // 2):
        for j in range(2):
            for direction in range(2):
                icol = 2 * N_HEADS * direction + 2 * pair + j
                out = LANE * pair + 2 * j + direction
                e[icol, out] = 1.0
                e[LANE + icol + N_HEADS, out] = -1.0
    return e.astype(BF16)


def _ml_gate_kernel(gc_ref, gr_ref, gbc_ref, gbr_ref, esel_ref, rows_ref, cc_ref):
    nc, _, tt = gr_ref.shape
    tri_lo = _tri(tt, upper=False).astype(BF16)
    tri_up = _tri(tt, upper=True).astype(BF16)
    lane = lax.broadcasted_iota(jnp.int32, (1, LANE), 1)
    row = lax.broadcasted_iota(jnp.int32, (N_GATES, 1), 0)
    esel = esel_ref[...]
    for ci in range(nc):
        rs = slice(ci * tt, (ci + 1) * tt)
        gc = gc_ref[rs, :] + gbc_ref[...]
        gr = gr_ref[ci] + gbr_ref[...]
        ls = _log_sigmoid(gc)
        cum_c = jnp.where(lane >= 2 * N_HEADS, _dot_split(tri_up, ls), _dot_split(tri_lo, ls))
        hi, lo = _split_bf16(jnp.concatenate([gc, cum_c], axis=1))
        cc_ref[rs, :] = _dot(hi, esel) + _dot(lo, esel)
        hi, lo = _split_bf16(_log_sigmoid(gr))
        prefix = _dot(hi, tri_up) + _dot(lo, tri_up)
        suffix = _dot(hi, tri_lo) + _dot(lo, tri_lo)
        is_forget = (row % (2 * N_HEADS)) >= N_HEADS
        rows_ref[ci] = jnp.where(is_forget, jnp.where(row >= 2 * N_HEADS, suffix, prefix), gr)


def _mlstm_gates(g, seq, tt, gate_b):
    batch = g.shape[0] // seq
    nc = seq // tt
    grow = g[:, :N_GATES].reshape(-1, tt, N_GATES).transpose(0, 2, 1)
    gb = gate_b.astype(F32)
    return pl.pallas_call(
        _ml_gate_kernel,
        out_shape=(jax.ShapeDtypeStruct((batch * nc, N_GATES, tt), F32),
                   jax.ShapeDtypeStruct((batch * seq, 2 * LANE), F32)),
        grid=(batch,),
        in_specs=[pl.BlockSpec((seq, LANE), lambda b: (b, 0)),
                  pl.BlockSpec((nc, N_GATES, tt), lambda b: (b, 0, 0)),
                  pl.BlockSpec((1, LANE), lambda b: (0, 0)),
                  pl.BlockSpec((N_GATES, 1), lambda b: (0, 0)),
                  pl.BlockSpec((2 * LANE, 2 * LANE), lambda b: (0, 0))],
        out_specs=(pl.BlockSpec((nc, N_GATES, tt), lambda b: (b, 0, 0)),
                   pl.BlockSpec((seq, 2 * LANE), lambda b: (b, 0))),
        compiler_params=_cparams(("arbitrary",)),
        name="mlstm_gates",
    )(g, grow, jnp.pad(gb, (0, LANE - N_GATES)).reshape(1, LANE), gb.reshape(-1, 1), _gate_select_matrix())


def _mlstm_chunk(q, k, vt, ones_row, b_row, i_row, c_col, ct_ref, m_ref, reverse):
    tt = q.shape[0]
    mask = _tri(tt, upper=not reverse)
    end = 0 if reverse else tt - 1
    b_end = b_row[:, end:end + 1]
    dm = jnp.where(mask, c_col + b_row, -jnp.inf)
    dmax = jnp.max(dm, axis=0, keepdims=True)
    pt = (_dot_nt(k, q) * jnp.exp(dm - dmax)).astype(BF16)
    pv = _dot(vt, pt)
    g = b_end - b_row + i_row
    gmax = jnp.max(g, axis=1, keepdims=True)

    m_prev = m_ref[...]
    ct_prev = ct_ref[...]
    m_inter = b_row + m_prev
    m_t = jnp.maximum(m_inter, dmax)
    num = jnp.exp(dmax - m_t) * pv + jnp.exp(m_inter - m_t) * _dot_nt(ct_prev.astype(BF16), q)
    den = num[ones_row:ones_row + 1, :]
    ht = num * (1.0 / jnp.maximum(jnp.abs(den), jnp.exp(-m_t)))
    m_new = jnp.maximum(b_end + m_prev, gmax)
    vw = (vt.astype(F32) * jnp.exp(g - m_new)).astype(BF16)
    ct_ref[...] = jnp.exp(b_end + m_prev - m_new) * ct_prev + _dot(vw, k)
    m_ref[...] = m_new
    return ht


def _mlstm_kernel(qx_ref, kx_ref, vx_ref, ox_ref, rwx_ref, ccx_ref,
                  qc_ref, kc_ref, vc_ref, oc_ref, rwc_ref, ccc_ref,
                  wq_ref, wk_ref, ng_ref,
                  yx_ref, yc_ref,
                  qsx, ksx, vtx, qsc, ksc, vtc, hfx, hbx, hfc, hbc, qpx, kpx, qpc, kpc, ct_scr, m_scr,
                  *, head_dim, write_ctx, unroll):
    pair = pl.program_id(1)
    tt = rwx_ref.shape[2]
    kscale = head_dim ** -0.5
    wins = _head_windows(head_dim)
    pad_rows = WIN - head_dim
    ones_block = (lax.broadcasted_iota(jnp.int32, (pad_rows, tt), 0) == 0).astype(BF16)

    def prep(q_ref, k_ref, v_ref, qpad, kpad, qs, ks, vts):
        _fill_padded(qpad, q_ref)
        _fill_padded(kpad, k_ref)

        def block(i, carry):
            r0 = pl.multiple_of(i * tt, tt)
            rows, halo = pl.ds(r0, tt), pl.ds(r0, tt + 2 * BF16_ROWS)
            qs[rows, :] = _silu(_conv_window(qpad[halo, :].astype(F32), wq_ref[...])).astype(BF16)
            ka = _silu(_conv_window(kpad[halo, :].astype(F32), wk_ref[...])) * kscale
            for j, (win, valid, _) in enumerate(wins):
                ks[j, rows, :] = jnp.where(valid, ka[:, win], 0.0).astype(BF16)
                vt = v_ref[rows, win].T
                vts[j, i] = (jnp.concatenate([vt[:head_dim], ones_block], axis=0) if j == 0 else
                             jnp.concatenate([ones_block, vt[pad_rows:]], axis=0))
            return carry

        lax.fori_loop(0, q_ref.shape[0] // tt, block, 0)

    prep(qx_ref, kx_ref, vx_ref, qpx, kpx, qsx, ksx, vtx)
    prep(qc_ref, kc_ref, vc_ref, qpc, kpc, qsc, ksc, vtc)
    ct_scr[...] = jnp.zeros_like(ct_scr)
    m_scr[...] = jnp.zeros_like(m_scr)

    sel_r = lax.broadcasted_iota(jnp.int32, (N_GATES, 1), 0)
    pick_r = lambda a, c: jnp.sum(jnp.where(sel_r == c, a, 0.0), axis=0, keepdims=True)

    def run(ci_f, ci_b, q_s, k_s, v_t, rows_ref, cc_ref, hf, hb):
        for reverse, ci, hout in ((False, ci_f, hf), (True, ci_b, hb)):
            rows = pl.ds(pl.multiple_of(ci * tt, tt), tt)
            gates = rows_ref[ci]
            cc = cc_ref[rows, :]
            for j, (win, _, free) in enumerate(wins):
                irow = (2 * N_HEADS if reverse else 0) + pair * 2 + j
                lane_c = 2 * j + (1 if reverse else 0)
                chain = j * 2 + (1 if reverse else 0)
                ht = _mlstm_chunk(q_s[rows, win], k_s[j, rows, :], v_t[j, ci], free,
                                  pick_r(gates, irow + N_HEADS), pick_r(gates, irow),
                                  cc[:, lane_c:lane_c + 1], ct_scr.at[chain], m_scr.at[chain], reverse)
                hout[j, rows, :] = ht.T

    _scan_loops(qc_ref.shape[0] // tt, qx_ref.shape[0] // tt,
                lambda f, b: run(f, b, qsc, ksc, vtc, rwc_ref, ccc_ref, hfc, hbc),
                lambda f, b: run(f, b, qsx, ksx, vtx, rwx_ref, ccx_ref, hfx, hbx), unroll)

    def finish(hf, hb, o_ref, y_ref):
        def block(rows):
            hn = _pair_finish(hf, hb, rows, head_dim, mask_invalid=True)
            y_ref[rows, :] = (hn * ng_ref[...] * _sigmoid(o_ref[rows, :].astype(F32))).astype(BF16)

        _row_blocks(o_ref.shape[0], tt, block)

    finish(hfx, hbx, ox_ref, yx_ref)
    if write_ctx:
        finish(hfc, hbc, oc_ref, yc_ref)
    else:
        yc_ref[...] = jnp.zeros_like(yc_ref)


def _mlstm(px, pc, gx, gc, seq, seq_c, col0, conv_w, gate_b, norm_g, head_dim, write_ctx):
    batch = px.shape[0] // seq
    total = N_HEADS * head_dim
    npair = total // PAIR_W
    cb = col0 // PAIR_W
    tt = MXU_W if (seq % MXU_W == 0 and seq_c % MXU_W == 0) else CHUNK
    rows_x, cc_x = _mlstm_gates(gx, seq, tt, gate_b)
    rows_c, cc_c = _mlstm_gates(gc, seq_c, tt, gate_b)

    def pspecs(s):
        return [pl.BlockSpec((s, PAIR_W), lambda b, h, k=k: (b, cb + k * npair + h)) for k in range(4)]

    def gspecs(s):
        return [pl.BlockSpec((s // tt, N_GATES, tt), lambda b, h: (b, 0, 0)),
                pl.BlockSpec((s, LANE), lambda b, h: (b, h))]

    scr = lambda s, dt: pltpu.VMEM((2, s, WIN), dt)
    scr_t = lambda s: pltpu.VMEM((2, s // tt, WIN, tt), BF16)
    scr_q = lambda s: pltpu.VMEM((s, PAIR_W), BF16)
    scr_pad = lambda s: pltpu.VMEM((s + 2 * BF16_ROWS, PAIR_W), BF16)
    kern = functools.partial(_mlstm_kernel, head_dim=head_dim, write_ctx=write_ctx, unroll=2)
    return pl.pallas_call(
        kern,
        out_shape=(jax.ShapeDtypeStruct((batch * seq, total), BF16),
                   jax.ShapeDtypeStruct((batch * seq_c, total), BF16)),
        grid=(batch, npair),
        in_specs=(pspecs(seq) + gspecs(seq) + pspecs(seq_c) + gspecs(seq_c) + [
            pl.BlockSpec((3, PAIR_W), lambda b, h: (0, h)),
            pl.BlockSpec((3, PAIR_W), lambda b, h: (0, npair + h)),
            pl.BlockSpec((1, PAIR_W), lambda b, h: (0, h)),
        ]),
        out_specs=(pl.BlockSpec((seq, PAIR_W), lambda b, h: (b, h)),
                   pl.BlockSpec((seq_c, PAIR_W), lambda b, h: (b, h))),
        scratch_shapes=[scr_q(seq), scr(seq, BF16), scr_t(seq),
                        scr_q(seq_c), scr(seq_c, BF16), scr_t(seq_c),
                        scr(seq, F32), scr(seq, F32), scr(seq_c, F32), scr(seq_c, F32),
                        scr_pad(seq), scr_pad(seq), scr_pad(seq_c), scr_pad(seq_c),
                        pltpu.VMEM((4, WIN, WIN), F32),
                        pltpu.VMEM((4, 1, 1), F32)],
        compiler_params=_cparams(("arbitrary", "arbitrary")),
        name="mlstm",
    )(px, px, px, px, rows_x, cc_x, pc, pc, pc, pc, rows_c, cc_c,
      conv_w, conv_w, norm_g.reshape(1, total))


def _ret_chunk(q, k, v, dmask, q_decay, k_decay, c_decay, s_ref):
    pv = _dot((_dot_nt(q, k) * dmask).astype(BF16), v)
    kd = (k.astype(F32) * k_decay).astype(BF16)
    s_prev = s_ref[...]
    s_ref[...] = c_decay * s_prev + _dot_tn(kd, v)
    return pv + q_decay * _dot(q, s_prev.astype(BF16))


def _ret_kernel(ld_ref, qx_ref, kx_ref, vx_ref, gx_ref, qc_ref, kc_ref, vc_ref, gc_ref, cos_ref, sgn_ref,
                yx_ref, yc_ref,
                qsx, ksx, vsx, qsc, ksc, vsc, ofx, obx, ofc, obc, s_scr,
                *, key_dim, val_dim, write_ctx, unroll, t):
    pair = pl.program_id(1)
    qscale = key_dim ** -0.5
    half = key_dim // 2
    vwins = _head_windows(val_dim)
    lane = lax.broadcasted_iota(jnp.int32, (1, WIN), 1)
    klane = lax.broadcasted_iota(jnp.int32, (1, qx_ref.shape[1]), 1)
    first_half = (klane % key_dim) < half

    def rope(a, rows):
        width = a.shape[1]
        partner = jnp.where(first_half, pltpu.roll(a, width - half, axis=1), pltpu.roll(a, half, axis=1))
        return a * cos_ref[rows, :] + partner * sgn_ref[rows, :]

    def prep(q_ref, k_ref, v_ref, qs, ks, vs, rotary):
        def block(rows):
            qa, ka = q_ref[rows, :].astype(F32), k_ref[rows, :].astype(F32)
            if rotary:
                qa, ka = rope(qa, rows), rope(ka, rows)
            qs[rows, :] = (jnp.where(pair == 0, qa[:, :WIN], qa[:, LANE:LANE + WIN]) * qscale).astype(BF16)
            kw = jnp.where(pair == 0, ka[:, :WIN], ka[:, LANE:LANE + WIN])
            for j, (win, valid, _) in enumerate(vwins):
                start = (2 * pair + j) * key_dim - LANE * pair
                kvalid = jnp.logical_and(lane >= start, lane < start + key_dim)
                ks[j, rows, :] = jnp.where(kvalid, kw, 0.0).astype(BF16)
                vs[j, rows, :] = jnp.where(valid, v_ref[rows, win].astype(F32), 0.0).astype(BF16)

        _row_blocks(q_ref.shape[0], t, block)

    prep(qx_ref, kx_ref, vx_ref, qsx, ksx, vsx, True)
    prep(qc_ref, kc_ref, vc_ref, qsc, ksc, vsc, False)
    s_scr[...] = jnp.zeros_like(s_scr)

    r = lax.broadcasted_iota(jnp.int32, (t, t), 0)
    c = lax.broadcasted_iota(jnp.int32, (t, t), 1)
    idx = lax.broadcasted_iota(jnp.int32, (t, 1), 0).astype(F32)
    consts = []
    for j in range(2):
        for reverse in (False, True):
            lg = -jnp.exp(jnp.full((1, 1), ld_ref[1 if reverse else 0, pair * 2 + j], F32))
            rel = (c - r) if reverse else (r - c)
            dmask = jnp.where(rel >= 0, jnp.exp(lg * jnp.maximum(rel, 0).astype(F32)), 0.0)
            pos = (t - 1.0 - idx) if reverse else idx
            consts.append((dmask, jnp.exp(lg * (pos + 1.0)), jnp.exp(lg * (t - 1.0 - pos)), jnp.exp(lg * t)))

    def run(ci_f, ci_b, q_s, k_s, v_s, of, ob):
        for reverse, ci, oout in ((False, ci_f, of), (True, ci_b, ob)):
            rows = pl.ds(pl.multiple_of(ci * t, t), t)
            for j in range(2):
                chain = j * 2 + (1 if reverse else 0)
                dmask, q_decay, k_decay, c_decay = consts[chain]
                oout[j, rows, :] = _ret_chunk(q_s[rows, :], k_s[j, rows, :], v_s[j, rows, :],
                                              dmask, q_decay, k_decay, c_decay, s_scr.at[chain])

    _scan_loops(qc_ref.shape[0] // t, qx_ref.shape[0] // t,
                lambda f, b: run(f, b, qsc, ksc, vsc, ofc, obc),
                lambda f, b: run(f, b, qsx, ksx, vsx, ofx, obx), unroll)

    def finish(of, ob, g_ref, y_ref):
        def block(rows):
            on = _pair_finish(of, ob, rows, val_dim, mask_invalid=False)
            y_ref[rows, :] = (on * _silu(g_ref[rows, :].astype(F32))).astype(BF16)

        _row_blocks(g_ref.shape[0], t, block)

    finish(ofx, obx, gx_ref, yx_ref)
    if write_ctx:
        finish(ofc, obc, gc_ref, yc_ref)
    else:
        yc_ref[...] = jnp.zeros_like(yc_ref)


def _retention(px, pc, seq, seq_c, col0, log_decay, cosf, sgnf, key_dim, val_dim, write_ctx):
    batch = px.shape[0] // seq
    ktot, vtot = N_HEADS * key_dim, N_HEADS * val_dim
    assert ktot == PAIR_W
    npair = vtot // PAIR_W
    qb = col0 // PAIR_W
    vb = (col0 + 2 * ktot) // PAIR_W

    def pspecs(s):
        return [pl.BlockSpec((s, PAIR_W), lambda b, h: (b, qb)),
                pl.BlockSpec((s, PAIR_W), lambda b, h: (b, qb + 1)),
                pl.BlockSpec((s, PAIR_W), lambda b, h: (b, vb + h)),
                pl.BlockSpec((s, PAIR_W), lambda b, h: (b, vb + npair + h))]

    scr = lambda s, dt: pltpu.VMEM((2, s, WIN), dt)
    t = MXU_W if (seq % MXU_W == 0 and seq_c % MXU_W == 0) else CHUNK
    kern = functools.partial(_ret_kernel, key_dim=key_dim, val_dim=val_dim, write_ctx=write_ctx, unroll=2, t=t)
    return pl.pallas_call(
        kern,
        out_shape=(jax.ShapeDtypeStruct((batch * seq, vtot), BF16),
                   jax.ShapeDtypeStruct((batch * seq_c, vtot), BF16)),
        grid=(batch, npair),
        in_specs=([pl.BlockSpec(memory_space=pltpu.SMEM)] + pspecs(seq) + pspecs(seq_c) + [
            _resident((seq, PAIR_W), lambda b, h: (0, 0)),
            _resident((seq, PAIR_W), lambda b, h: (0, 0)),
        ]),
        out_specs=(pl.BlockSpec((seq, PAIR_W), lambda b, h: (b, h)),
                   pl.BlockSpec((seq_c, PAIR_W), lambda b, h: (b, h))),
        scratch_shapes=[pltpu.VMEM((seq, WIN), BF16), scr(seq, BF16), scr(seq, BF16),
                        pltpu.VMEM((seq_c, WIN), BF16), scr(seq_c, BF16), scr(seq_c, BF16),
                        scr(seq, F32), scr(seq, F32), scr(seq_c, F32), scr(seq_c, F32),
                        pltpu.VMEM((4, WIN, WIN), F32)],
        compiler_params=_cparams(("arbitrary", "arbitrary")),
        name="retention",
    )(log_decay.astype(F32), px, px, px, px, pc, pc, pc, pc, cosf, sgnf)


@functools.lru_cache(maxsize=None)
def _rope_tables(seq, key_dim):
    rows = seq // GRID_W
    r = np.repeat(np.arange(rows, dtype=np.float32), GRID_W)
    col = np.tile(np.arange(GRID_W, dtype=np.float32), rows)
    nf = key_dim // 4
    inv = (np.float32(ROPE_BASE) ** (-np.arange(nf, dtype=np.float32) / np.float32(nf))).astype(np.float32)
    ang = np.concatenate([r[:, None] * inv, col[:, None] * inv], axis=-1).astype(np.float32)
    cos, sin = np.cos(ang).astype(np.float32), np.sin(ang).astype(np.float32)
    return np.tile(np.concatenate([cos, cos], axis=1), (1, N_HEADS)), np.tile(np.concatenate([-sin, sin], axis=1), (1, N_HEADS))


def kernel(x, c, ctx, c_ctx, norm1_g, norm2_g, w_mod, b_mod, w_in, hy_conv_w, hy_f_w1, hy_f_b1, hy_f_w2, hy_f_b2, hy_f_w3, hy_f_freq, hy_bias, ml_conv_w, ml_gate_b, ml_norm_g, rt_log_decay, w_out, w_ff1, w_ff2, final_g):
    batch, seq, d = x.shape
    seq_c = ctx.shape[1]
    depth = w_mod.shape[0]
    hy_w = d // 4
    ml_w = 3 * d // 8
    rt_w = 3 * d // 8
    ml_dh = ml_w // N_HEADS
    rt_dv = rt_w // N_HEADS
    rt_dk = rt_dv // 2
    hy_cols, ml_cols = 3 * hy_w, 4 * ml_w
    gate0 = hy_cols + ml_cols
    rt0 = gate0 + N_GATES

    s_in = jnp.concatenate([c, c_ctx[None], jnp.zeros((MOD_ROWS - batch - 1, d), F32)], axis=0)
    mod = _modulation(s_in, w_mod, b_mod)

    grp_x = lambda tm: (lambda i: (i * tm) // seq)
    grp_c = lambda tm: (lambda i: batch + 0 * i)
    tm_x, tm_c = _pick_tile(seq, 1024), _pick_tile(batch * seq_c, 1024)
    tm_ox, tm_oc = _pick_tile(seq, 512), _pick_tile(batch * seq_c, 512)

    def dft(n):
        cm, sm, ct, st = _dft_tables(n)
        return jnp.asarray(cm).astype(BF16), jnp.asarray(sm).astype(BF16), ct, st

    cm_x, sm_x, ct_x, st_x = dft(seq)
    cm_c, sm_c, ct_c, st_c = dft(seq_c)
    cosf, sgnf = _rope_tables(seq, rt_dk)
    tc = _pick_tile(hy_w, MXU_W)
    tc_spec = _pick_tile(hy_w, LANE)

    xs = x.reshape(batch * seq, d)
    hc = ctx.reshape(batch * seq_c, d)
    wt = jnp.swapaxes(w_in, 1, 2)
    w_pt = wt.astype(BF16)
    w_gt = jnp.pad(wt[:, gate0:rt0], ((0, 0), (0, LANE - N_GATES), (0, 0))).astype(BF16)
    ml_col0, rt_col0 = hy_cols, gate0
    wo = w_out.astype(BF16)
    w1 = w_ff1.astype(BF16)
    w2 = w_ff2.astype(BF16)

    for l in range(depth):
        need_ctx = l < depth - 1
        mod3 = mod[l].reshape(MOD_ROWS * N_MOD, 1, d)

        px, gx = _in_proj(xs, norm1_g[l], mod3, grp_x(tm_x), w_pt, w_gt, l, tm_x, gate0, rt0)
        pc, gc = _in_proj(hc, norm1_g[l], mod3, grp_c(tm_c), w_pt, w_gt, l, tm_c, gate0, rt0)

        filt = (hy_f_w1[l], hy_f_b1[l], hy_f_w2[l], hy_f_b2[l], hy_f_w3[l], hy_f_freq[l])
        hf, hb = _hyena_filter(seq, *filt)
        hre, him = _hyena_spectrum(hf, hb, cm_x, sm_x, ct_x, st_x, tc_spec)
        y_hy = _hyena_conv(px, seq, hy_conv_w[l], hre, him, hy_bias[l], cm_x, sm_x, tc)
        y_ml, yc_ml = _mlstm(px, pc, gx, gc, seq, seq_c, ml_col0, ml_conv_w[l], ml_gate_b[l], ml_norm_g[l],
                             ml_dh, need_ctx)
        y_rt, yc_rt = _retention(px, pc, seq, seq_c, rt_col0, rt_log_decay[l], cosf, sgnf,
                                 rt_dk, rt_dv, need_ctx)
        xs, h2 = _out_proj(xs, y_hy, y_ml, y_rt, wo, l, norm2_g[l], mod3, grp_x(tm_ox), tm_ox, alias=l > 0)
        if need_ctx:
            hfc, hbc = _hyena_filter(seq_c, *filt)
            hre_c, him_c = _hyena_spectrum(hfc, hbc, cm_c, sm_c, ct_c, st_c, tc_spec)
            yc_hy = _hyena_conv(pc, seq_c, hy_conv_w[l], hre_c, him_c, hy_bias[l], cm_c, sm_c, tc)
            hc, h2c = _out_proj(hc, yc_hy, yc_ml, yc_rt, wo, l, norm2_g[l], mod3, grp_c(tm_oc), tm_oc, alias=l > 0)
            hc = _ffn(hc, h2c, mod3, grp_c(tm_c), w1, w2, l, tm_c, 1024)
        xs = _ffn(xs, h2, mod3, grp_x(tm_x), w1, w2, l, tm_x, 1024)
    return _final_norm(xs, final_g, tm_ox).reshape(batch, seq, d)
```

```python
import functools
import math

import jax
import jax.numpy as jnp
import numpy as np
from jax import lax
from jax.experimental import pallas as pl
from jax.experimental.pallas import tpu as pltpu

F32 = jnp.float32
BF16 = jnp.bfloat16

GRID_W = 64
HY_EMB = 33
HY_FF = 64
HY_DECAY_TARGET = 1e-2
HY_SHORT_PCT = 0.3
HY_LONG_PCT = 1.5
N_HEADS = 4
N_GATES = 4 * N_HEADS
CHUNK = 128
ROPE_BASE = 10000.0
N_MOD = 6
EPS = 1e-6

LANE = 128
MXU_W = 256
WIN = 2 * LANE
PAIR_W = 3 * LANE
MOD_ROWS = 16
VMEM_LIMIT = 56 * 1024 * 1024


def _cparams(sem):
    return pltpu.CompilerParams(dimension_semantics=sem, vmem_limit_bytes=VMEM_LIMIT)


def _sigmoid(x):
    return 0.5 * jnp.tanh(0.5 * x) + 0.5


def _silu(x):
    return x * _sigmoid(x)


def _log_sigmoid(x):
    return jnp.minimum(x, 0.0) - jnp.log(1.0 + jnp.exp(-jnp.abs(x)))


def _dot(a, b):
    return jnp.dot(a, b, preferred_element_type=F32)


def _dot_nt(a, b):
    return lax.dot_general(a, b, (((1,), (1,)), ((), ())), preferred_element_type=F32)


def _dot_tn(a, b):
    return lax.dot_general(a, b, (((0,), (0,)), ((), ())), preferred_element_type=F32)


def _split_bf16(a):
    hi = a.astype(BF16)
    lo = (a - hi.astype(F32)).astype(BF16)
    return hi, lo


def _dot_split(a, b_f32):
    hi, lo = _split_bf16(b_f32)
    return _dot(a, hi) + _dot(a, lo)


def _resident(shape, index_map):
    return pl.BlockSpec(shape, index_map, pipeline_mode=pl.Buffered(1))


def _pick_tile(n, target):
    t = min(n, target)
    while n % t:
        t //= 2
    return t


def _norm_mod(x, g, shift, scale):
    ms = jnp.mean(x * x, axis=-1, keepdims=True)
    y = x * lax.rsqrt(ms + EPS) * g
    return y * (1.0 + scale) + shift


def _mod_kernel(s_ref, w_ref, b_ref, o_ref):
    s = _silu(s_ref[...]).astype(BF16)
    w = w_ref[0].astype(BF16)
    o_ref[0] = _dot(s, w) + b_ref[0]


def _modulation(s_in, w_mod, b_mod):
    depth, d, n = w_mod.shape
    tn = _pick_tile(n, 1024)
    return pl.pallas_call(
        _mod_kernel,
        out_shape=jax.ShapeDtypeStruct((depth, MOD_ROWS, n), F32),
        grid=(depth, n // tn),
        in_specs=[
            pl.BlockSpec((MOD_ROWS, d), lambda l, j: (0, 0)),
            pl.BlockSpec((1, d, tn), lambda l, j: (l, 0, j)),
            pl.BlockSpec((1, 1, tn), lambda l, j: (l, 0, j)),
        ],
        out_specs=pl.BlockSpec((1, MOD_ROWS, tn), lambda l, j: (l, 0, j)),
        compiler_params=_cparams(("arbitrary", "arbitrary")),
        name="modulation",
    )(s_in, w_mod, b_mod.reshape(depth, 1, n))


def _inproj_kernel(x_ref, g_ref, shift_ref, scale_ref, wt_ref, wgt_ref, p_ref, gate_ref, h_scr):
    @pl.when(pl.program_id(1) == 0)
    def _():
        h = _norm_mod(x_ref[...], g_ref[...], shift_ref[0], scale_ref[0]).astype(BF16)
        h_scr[...] = h
        gate_ref[...] = _dot_nt(h, wgt_ref[...])

    p_ref[...] = _dot_nt(h_scr[...], wt_ref[...]).astype(BF16)


def _in_proj(x2, norm_g, mod3, group_of_tile, w_pt, w_gt, layer, tm, skip0, skip1):
    m, d = x2.shape
    n0 = w_pt.shape[1]
    n = n0 - (skip1 - skip0)
    tn = 3 * MXU_W
    assert skip0 % tn == 0 and n % tn == 0 and skip1 % BF16_ROWS == 0
    n_before = skip0 // tn

    def w_row(j):
        return j * tn + (j >= n_before).astype(jnp.int32) * (skip1 - skip0)

    return pl.pallas_call(
        _inproj_kernel,
        out_shape=(jax.ShapeDtypeStruct((m, n), BF16), jax.ShapeDtypeStruct((m, LANE), F32)),
        grid=(m // tm, n // tn),
        in_specs=[
            pl.BlockSpec((tm, d), lambda i, j: (i, 0)),
            pl.BlockSpec((1, d), lambda i, j: (0, 0)),
            pl.BlockSpec((1, 1, d), lambda i, j: (group_of_tile(i) * N_MOD + 0, 0, 0)),
            pl.BlockSpec((1, 1, d), lambda i, j: (group_of_tile(i) * N_MOD + 1, 0, 0)),
            pl.BlockSpec((pl.Element(tn), pl.Element(d)),
                         lambda i, j: (pl.multiple_of(layer * n0 + w_row(j), BF16_ROWS), 0)),
            pl.BlockSpec((None, LANE, d), lambda i, j: (layer, 0, 0)),
        ],
        out_specs=(pl.BlockSpec((tm, tn), lambda i, j: (i, j)),
                   pl.BlockSpec((tm, LANE), lambda i, j: (i, 0))),
        scratch_shapes=[pltpu.VMEM((tm, d), BF16)],
        compiler_params=_cparams(("arbitrary", "arbitrary")),
        name="in_proj",
    )(x2, norm_g.reshape(1, d), mod3, mod3, w_pt.reshape(-1, d), w_gt)


def _outproj_kernel(x_ref, yh_ref, ym_ref, yr_ref, w_ref, gate_ref,
                    g2_ref, shift_ref, scale_ref, o_ref, h2_ref):
    kh, km = yh_ref.shape[1], ym_ref.shape[1]
    acc = _dot(yh_ref[...], w_ref[0:kh, :])
    acc = acc + _dot(ym_ref[...], w_ref[kh:kh + km, :])
    acc = acc + _dot(yr_ref[...], w_ref[kh + km:, :])
    xn = x_ref[...] + gate_ref[0] * acc
    o_ref[...] = xn
    h2_ref[...] = _norm_mod(xn, g2_ref[...], shift_ref[0], scale_ref[0]).astype(BF16)


def _out_proj(x2, y_hy, y_ml, y_rt, wo, layer, norm2_g, mod3, group_of_tile, tm, alias):
    m, d = x2.shape
    kh, km, kr = y_hy.shape[1], y_ml.shape[1], y_rt.shape[1]
    mrow = lambda k: pl.BlockSpec((1, 1, d), lambda i: (group_of_tile(i) * N_MOD + k, 0, 0))
    return pl.pallas_call(
        _outproj_kernel,
        out_shape=(jax.ShapeDtypeStruct((m, d), F32), jax.ShapeDtypeStruct((m, d), BF16)),
        grid=(m // tm,),
        in_specs=[
            pl.BlockSpec((tm, d), lambda i: (i, 0)),
            pl.BlockSpec((tm, kh), lambda i: (i, 0)),
            pl.BlockSpec((tm, km), lambda i: (i, 0)),
            pl.BlockSpec((tm, kr), lambda i: (i, 0)),
            _resident((None, kh + km + kr, d), lambda i: (layer, 0, 0)),
            mrow(2),
            pl.BlockSpec((1, d), lambda i: (0, 0)),
            mrow(3), mrow(4),
        ],
        out_specs=(pl.BlockSpec((tm, d), lambda i: (i, 0)), pl.BlockSpec((tm, d), lambda i: (i, 0))),
        input_output_aliases=({0: 0} if alias else {}),
        compiler_params=_cparams(("arbitrary",)),
        name="out_proj",
    )(x2, y_hy, y_ml, y_rt, wo, mod3, norm2_g.reshape(1, d), mod3, mod3)


def _ffn_kernel(h_ref, x_ref, gate_ref, w1_ref, w2_ref, o_ref, hid_scr, *, n1):
    j = pl.program_id(1)
    tf = w1_ref.shape[1]

    @pl.when(j < n1)
    def _():
        a = jnp.maximum(_dot(h_ref[...], w1_ref[...]), 0.0)
        hid_scr[jnp.minimum(j, n1 - 1)] = (a * a).astype(BF16)

    @pl.when(j >= n1)
    def _():
        acc = _dot(hid_scr[0], w2_ref[0:tf, :])
        for c in range(1, n1):
            acc = acc + _dot(hid_scr[c], w2_ref[c * tf:(c + 1) * tf, :])
        o_ref[...] = x_ref[...] + gate_ref[0] * acc


def _ffn(x2, h2, mod3, group_of_tile, w1, w2, layer, tm, tf):
    m, d = x2.shape
    f = w1.shape[2]
    n1 = f // tf
    tn = MXU_W
    n2 = d // tn
    col = lambda j: jnp.maximum(j - n1, 0)
    return pl.pallas_call(
        functools.partial(_ffn_kernel, n1=n1),
        out_shape=jax.ShapeDtypeStruct((m, d), F32),
        grid=(m // tm, n1 + n2),
        in_specs=[
            pl.BlockSpec((tm, d), lambda i, j: (i, 0)),
            pl.BlockSpec((tm, tn), lambda i, j: (i, col(j))),
            pl.BlockSpec((1, 1, tn), lambda i, j: (group_of_tile(i) * N_MOD + 5, 0, col(j))),
            pl.BlockSpec((None, d, tf), lambda i, j: (layer, 0, jnp.minimum(j, n1 - 1))),
            pl.BlockSpec((None, f, tn), lambda i, j: (layer, 0, col(j))),
        ],
        out_specs=pl.BlockSpec((tm, tn), lambda i, j: (i, col(j))),
        scratch_shapes=[pltpu.VMEM((n1, tm, tf), BF16)],
        compiler_params=_cparams(("arbitrary", "arbitrary")),
        name="ffn",
    )(h2, x2, mod3, w1, w2)


def _final_norm_kernel(x_ref, g_ref, o_ref):
    x = x_ref[...]
    ms = jnp.mean(x * x, axis=-1, keepdims=True)
    o_ref[...] = x * lax.rsqrt(ms + EPS) * g_ref[...]


def _final_norm(x2, g, tm):
    m, d = x2.shape
    return pl.pallas_call(
        _final_norm_kernel,
        out_shape=jax.ShapeDtypeStruct((m, d), F32),
        grid=(m // tm,),
        in_specs=[pl.BlockSpec((tm, d), lambda i: (i, 0)), pl.BlockSpec((1, d), lambda i: (0, 0))],
        out_specs=pl.BlockSpec((tm, d), lambda i: (i, 0)),
        compiler_params=_cparams(("arbitrary",)),
        name="final_norm",
    )(x2, g.reshape(1, d))


def _hy_filter_kernel(z_ref, t_ref, w1_ref, b1_ref, w2_ref, b2_ref, w3f_ref, w3b_ref, fr_ref, dl_ref,
                      hf_ref, hb_ref):
    hp = lax.Precision.HIGHEST
    dot = lambda a, b: jnp.dot(a, b, precision=hp, preferred_element_type=F32)
    fr = fr_ref[...]
    hdn = jnp.sin(fr * (dot(z_ref[...], w1_ref[...]) + b1_ref[...]))
    hdn = jnp.sin(fr * (dot(hdn, w2_ref[...]) + b2_ref[...]))
    win = jnp.exp(-t_ref[...] * dl_ref[...])
    hf = dot(hdn, w3f_ref[...]) * win
    hb = dot(hdn, w3b_ref[...]) * win
    row = lax.broadcasted_iota(jnp.int32, hb.shape, 0)
    hb = jnp.where(row == 0, 0.0, hb)
    inv = 1.0 / (jnp.sum(jnp.abs(hf), axis=0, keepdims=True) + jnp.sum(jnp.abs(hb), axis=0, keepdims=True))
    hf_ref[...] = hf * inv
    hb_ref[...] = hb * inv


@functools.lru_cache(maxsize=None)
def _filter_tables(seq, hy_w):
    t = np.linspace(0.0, 1.0, seq, dtype=np.float32)[:, None]
    w = (np.float32(2.0 * math.pi / seq) * np.arange(seq, dtype=np.float32))[:, None]
    nb = (HY_EMB - 1) // 2
    bands = np.linspace(1e-4, nb - 1, nb, dtype=np.float32)[None, :]
    z = np.concatenate([t, np.cos(bands * w), -np.sin(bands * w)], axis=-1).astype(np.float32)
    z = np.pad(z, ((0, 0), (0, LANE - HY_EMB)))
    deltas = np.abs(np.linspace(math.log(HY_DECAY_TARGET) / HY_LONG_PCT,
                                math.log(HY_DECAY_TARGET) / HY_SHORT_PCT, hy_w, dtype=np.float32))[None, :]
    return z, t, deltas


def _hyena_filter(seq, w1, b1, w2, b2, w3, freq):
    hy_w = w3.shape[1] // 2
    z, t, deltas = _filter_tables(seq, hy_w)
    pad_ff = LANE - HY_FF
    w1p = jnp.pad(w1, ((0, LANE - HY_EMB), (0, pad_ff)))
    w2p = jnp.pad(w2, ((0, pad_ff), (0, pad_ff)))
    w3p = jnp.pad(w3, ((0, pad_ff), (0, 0)))
    padv = lambda v: jnp.pad(v, (0, pad_ff)).reshape(1, LANE)
    out = jax.ShapeDtypeStruct((seq, hy_w), F32)
    return pl.pallas_call(
        _hy_filter_kernel,
        out_shape=(out, out),
        compiler_params=pltpu.CompilerParams(vmem_limit_bytes=VMEM_LIMIT),
        name="hyena_filter",
    )(z, t, w1p, padv(b1), w2p, padv(b2), w3p[:, :hy_w], w3p[:, hy_w:], padv(freq), deltas)


@functools.lru_cache(maxsize=None)
def _dft_tables(seq):
    k = np.arange(seq, dtype=np.int64)
    mm = ((2 * k[:, None] + 1) * (2 * k[None, :] + 1)) % (8 * seq)
    ang = mm.astype(np.float64) * (2.0 * math.pi / (8 * seq))
    theta = (k.astype(np.float64) + 0.5) * (math.pi / (2 * seq))
    return (np.cos(ang).astype(np.float32), np.sin(ang).astype(np.float32),
            np.cos(theta).astype(np.float32)[:, None], np.sin(theta).astype(np.float32)[:, None])


def _hy_spec_kernel(hf_ref, hb_ref, c_ref, s_ref, ct_ref, st_ref, hre_ref, him_ref):
    cm, sm = c_ref[...], s_ref[...]
    seq = cm.shape[0]
    tc = hf_ref.shape[1]
    taps = jnp.concatenate([hf_ref[...], hb_ref[...]], axis=1)
    ca, sa = _dot_split(cm, taps), _dot_split(sm, taps)
    a, a2 = ca[:, :tc], ca[:, tc:]
    b, b2 = sa[:, :tc], sa[:, tc:]
    ct, st = ct_ref[...], st_ref[...]
    scale = 1.0 / seq
    hre_ref[...] = ((a + a2) * ct + (b + b2) * st) * scale
    him_ref[...] = ((a - a2) * st - (b - b2) * ct) * scale


def _hyena_spectrum(hf, hb, cm, sm, ct, st, tc):
    seq, c = hf.shape
    out = jax.ShapeDtypeStruct((seq, c), F32)
    col = pl.BlockSpec((seq, tc), lambda j: (0, j))
    return pl.pallas_call(
        _hy_spec_kernel,
        out_shape=(out, out),
        grid=(c // tc,),
        in_specs=[col, col,
                  _resident((seq, seq), lambda j: (0, 0)), _resident((seq, seq), lambda j: (0, 0)),
                  pl.BlockSpec((seq, 1), lambda j: (0, 0)), pl.BlockSpec((seq, 1), lambda j: (0, 0))],
        out_specs=(col, col),
        compiler_params=_cparams(("arbitrary",)),
        name="hyena_spectrum",
    )(hf, hb, cm, sm, ct, st)


def _short_conv(p, w):
    n = p.shape[0]
    pz = jnp.concatenate([p, jnp.zeros((8, p.shape[1]), p.dtype)], axis=0)
    prev = pltpu.roll(pz, 1, axis=0)[:n]
    nxt = pltpu.roll(pz, n + 7, axis=0)[:n]
    return prev * w[0:1] + p * w[1:2] + nxt * w[2:3]


def _hy_conv_kernel(p0_ref, p1_ref, pv_ref, w0_ref, w1_ref, wv_ref, hre_ref, him_ref, bias_ref,
                    c_ref, s_ref, o_ref):
    x1 = _short_conv(p1_ref[...].astype(F32), w1_ref[...])
    v = _short_conv(pv_ref[...].astype(F32), wv_ref[...])
    u = v * x1
    ub = u.astype(BF16)
    cm, sm = c_ref[...], s_ref[...]
    zc = _dot(cm, ub)
    zs = _dot(sm, ub)
    hre, him = hre_ref[...], him_ref[...]
    yre = (hre * zc + him * zs).astype(BF16)
    yim = (him * zc - hre * zs).astype(BF16)
    y = _dot(cm, yre) - _dot(sm, yim)
    x0 = _short_conv(p0_ref[...].astype(F32), w0_ref[...])
    o_ref[...] = ((y + u * bias_ref[...]) * x0).astype(BF16)


def _hyena_conv(p, seq, conv_w, hre, him, bias, cm, sm, tc):
    m = p.shape[0]
    hy_w = hre.shape[1]
    nb = hy_w // tc
    pspec = lambda off: pl.BlockSpec((seq, tc), lambda j, b, off=off: (b, off * nb + j))
    wspec = lambda off: pl.BlockSpec((3, tc), lambda j, b, off=off: (0, off * nb + j))
    col = pl.BlockSpec((seq, tc), lambda j, b: (0, j))
    return pl.pallas_call(
        _hy_conv_kernel,
        out_shape=jax.ShapeDtypeStruct((m, hy_w), BF16),
        grid=(nb, m // seq),
        in_specs=[pspec(0), pspec(1), pspec(2), wspec(0), wspec(1), wspec(2), col, col,
                  pl.BlockSpec((1, tc), lambda j, b: (0, j)),
                  _resident((seq, seq), lambda j, b: (0, 0)), _resident((seq, seq), lambda j, b: (0, 0))],
        out_specs=pl.BlockSpec((seq, tc), lambda j, b: (b, j)),
        compiler_params=_cparams(("arbitrary", "arbitrary")),
        name="hyena_conv",
    )(p, p, p, conv_w, conv_w, conv_w, hre, him, bias.reshape(1, hy_w), cm, sm)


def _tri(n, upper):
    r = lax.broadcasted_iota(jnp.int32, (n, n), 0)
    c = lax.broadcasted_iota(jnp.int32, (n, n), 1)
    return (r <= c) if upper else (r >= c)


def _head_windows(head_dim):
    assert 2 * head_dim == PAIR_W
    lane = lax.broadcasted_iota(jnp.int32, (1, WIN), 1)
    return ((slice(0, WIN), lane < head_dim, head_dim),
            (slice(LANE, LANE + WIN), lane >= WIN - head_dim, 0))


def _pair_finish(acc_f, acc_b, rows, head_dim, mask_invalid):
    outs = []
    for j, (_, valid, _) in enumerate(_head_windows(head_dim)):
        h = acc_f[j, rows, :] + acc_b[j, rows, :]
        if mask_invalid:
            h = jnp.where(valid, h, 0.0)
        ms = jnp.sum(h * h, axis=-1, keepdims=True) * (1.0 / head_dim)
        outs.append(h * lax.rsqrt(ms + EPS))
    h0, h1 = outs
    return jnp.concatenate([h0[:, :LANE], h0[:, LANE:] + h1[:, :LANE], h1[:, LANE:]], axis=1)


def _row_blocks(n, blk, body):
    def step(i, carry):
        body(pl.ds(pl.multiple_of(i * blk, blk), blk))
        return carry

    lax.fori_loop(0, n // blk, step, 0)


BF16_ROWS = 16


def _conv_window(w, cw):
    n = w.shape[0]
    u = pltpu.roll(w, 1, axis=0) * cw[0:1] + w * cw[1:2] + pltpu.roll(w, n - 1, axis=0) * cw[2:3]
    return u[BF16_ROWS:n - BF16_ROWS]


def _fill_padded(pad_ref, src_ref):
    n = src_ref.shape[0]
    zero = jnp.zeros((BF16_ROWS, src_ref.shape[1]), src_ref.dtype)
    pad_ref[0:BF16_ROWS, :] = zero
    pad_ref[BF16_ROWS + n:2 * BF16_ROWS + n, :] = zero
    pad_ref[BF16_ROWS:BF16_ROWS + n, :] = src_ref[...]


def _scan_loops(nc_c, nc_x, run_c, run_x, unroll):
    def loop(nc, run):
        u = unroll if nc % unroll == 0 else 1

        def body(i, carry):
            for k in range(u):
                run(i * u + k, nc - 1 - (i * u + k))
            return carry

        lax.fori_loop(0, nc // u, body, 0)

    loop(nc_c, run_c)
    loop(nc_x, run_x)


@functools.lru_cache(maxsize=None)
def _gate_select_matrix():
    e = np.zeros((2 * LANE, 2 * LANE), np.float32)
    for pair in range(N_HEADS // 2):
        for j in range(2):
            for direction in range(2):
                icol = 2 * N_HEADS * direction + 2 * pair + j
                out = LANE * pair + 2 * j + direction
                e[icol, out] = 1.0
                e[LANE + icol + N_HEADS, out] = -1.0
    return e.astype(BF16)


def _ml_gate_kernel(gc_ref, gr_ref, gbc_ref, gbr_ref, esel_ref, rows_ref, cc_ref):
    nc, _, tt = gr_ref.shape
    tri_lo = _tri(tt, upper=False).astype(BF16)
    tri_up = _tri(tt, upper=True).astype(BF16)
    lane = lax.broadcasted_iota(jnp.int32, (1, LANE), 1)
    row = lax.broadcasted_iota(jnp.int32, (N_GATES, 1), 0)
    esel = esel_ref[...]
    for ci in range(nc):
        rs = slice(ci * tt, (ci + 1) * tt)
        gc = gc_ref[rs, :] + gbc_ref[...]
        gr = gr_ref[ci] + gbr_ref[...]
        ls = _log_sigmoid(gc)
        cum_c = jnp.where(lane >= 2 * N_HEADS, _dot_split(tri_up, ls), _dot_split(tri_lo, ls))
        hi, lo = _split_bf16(jnp.concatenate([gc, cum_c], axis=1))
        cc_ref[rs, :] = _dot(hi, esel) + _dot(lo, esel)
        hi, lo = _split_bf16(_log_sigmoid(gr))
        prefix = _dot(hi, tri_up) + _dot(lo, tri_up)
        suffix = _dot(hi, tri_lo) + _dot(lo, tri_lo)
        is_forget = (row % (2 * N_HEADS)) >= N_HEADS
        rows_ref[ci] = jnp.where(is_forget, jnp.where(row >= 2 * N_HEADS, suffix, prefix), gr)


def _mlstm_gates(g, seq, tt, gate_b):
    batch = g.shape[0] // seq
    nc = seq // tt
    grow = g[:, :N_GATES].reshape(-1, tt, N_GATES).transpose(0, 2, 1)
    gb = gate_b.astype(F32)
    return pl.pallas_call(
        _ml_gate_kernel,
        out_shape=(jax.ShapeDtypeStruct((batch * nc, N_GATES, tt), F32),
                   jax.ShapeDtypeStruct((batch * seq, 2 * LANE), F32)),
        grid=(batch,),
        in_specs=[pl.BlockSpec((seq, LANE), lambda b: (b, 0)),
                  pl.BlockSpec((nc, N_GATES, tt), lambda b: (b, 0, 0)),
                  pl.BlockSpec((1, LANE), lambda b: (0, 0)),
                  pl.BlockSpec((N_GATES, 1), lambda b: (0, 0)),
                  pl.BlockSpec((2 * LANE, 2 * LANE), lambda b: (0, 0))],
        out_specs=(pl.BlockSpec((nc, N_GATES, tt), lambda b: (b, 0, 0)),
                   pl.BlockSpec((seq, 2 * LANE), lambda b: (b, 0))),
        compiler_params=_cparams(("arbitrary",)),
        name="mlstm_gates",
    )(g, grow, jnp.pad(gb, (0, LANE - N_GATES)).reshape(1, LANE), gb.reshape(-1, 1), _gate_select_matrix())


def _mlstm_chunk(q, k, vt, ones_row, b_row, i_row, c_col, ct_ref, m_ref, reverse):
    tt = q.shape[0]
    mask = _tri(tt, upper=not reverse)
    end = 0 if reverse else tt - 1
    b_end = b_row[:, end:end + 1]
    dm = jnp.where(mask, c_col + b_row, -jnp.inf)
    dmax = jnp.max(dm, axis=0, keepdims=True)
    pt = (_dot_nt(k, q) * jnp.exp(dm - dmax)).astype(BF16)
    pv = _dot(vt, pt)
    g = b_end - b_row + i_row
    gmax = jnp.max(g, axis=1, keepdims=True)

    m_prev = m_ref[...]
    ct_prev = ct_ref[...]
    m_inter = b_row + m_prev
    m_t = jnp.maximum(m_inter, dmax)
    num = jnp.exp(dmax - m_t) * pv + jnp.exp(m_inter - m_t) * _dot_nt(ct_prev.astype(BF16), q)
    den = num[ones_row:ones_row + 1, :]
    ht = num * (1.0 / jnp.maximum(jnp.abs(den), jnp.exp(-m_t)))
    m_new = jnp.maximum(b_end + m_prev, gmax)
    vw = (vt.astype(F32) * jnp.exp(g - m_new)).astype(BF16)
    ct_ref[...] = jnp.exp(b_end + m_prev - m_new) * ct_prev + _dot(vw, k)
    m_ref[...] = m_new
    return ht


def _mlstm_kernel(qx_ref, kx_ref, vx_ref, ox_ref, rwx_ref, ccx_ref,
                  qc_ref, kc_ref, vc_ref, oc_ref, rwc_ref, ccc_ref,
                  wq_ref, wk_ref, ng_ref,
                  yx_ref, yc_ref,
                  qsx, ksx, vtx, qsc, ksc, vtc, hfx, hbx, hfc, hbc, qpx, kpx, qpc, kpc, ct_scr, m_scr,
                  *, head_dim, write_ctx, unroll):
    pair = pl.program_id(1)
    tt = rwx_ref.shape[2]
    kscale = head_dim ** -0.5
    wins = _head_windows(head_dim)
    pad_rows = WIN - head_dim
    ones_block = (lax.broadcasted_iota(jnp.int32, (pad_rows, tt), 0) == 0).astype(BF16)

    def prep(q_ref, k_ref, v_ref, qpad, kpad, qs, ks, vts):
        _fill_padded(qpad, q_ref)
        _fill_padded(kpad, k_ref)

        def block(i, carry):
            r0 = pl.multiple_of(i * tt, tt)
            rows, halo = pl.ds(r0, tt), pl.ds(r0, tt + 2 * BF16_ROWS)
            qs[rows, :] = _silu(_conv_window(qpad[halo, :].astype(F32), wq_ref[...])).astype(BF16)
            ka = _silu(_conv_window(kpad[halo, :].astype(F32), wk_ref[...])) * kscale
            for j, (win, valid, _) in enumerate(wins):
                ks[j, rows, :] = jnp.where(valid, ka[:, win], 0.0).astype(BF16)
                vt = v_ref[rows, win].T
                vts[j, i] = (jnp.concatenate([vt[:head_dim], ones_block], axis=0) if j == 0 else
                             jnp.concatenate([ones_block, vt[pad_rows:]], axis=0))
            return carry

        lax.fori_loop(0, q_ref.shape[0] // tt, block, 0)

    prep(qx_ref, kx_ref, vx_ref, qpx, kpx, qsx, ksx, vtx)
    prep(qc_ref, kc_ref, vc_ref, qpc, kpc, qsc, ksc, vtc)
    ct_scr[...] = jnp.zeros_like(ct_scr)
    m_scr[...] = jnp.zeros_like(m_scr)

    sel_r = lax.broadcasted_iota(jnp.int32, (N_GATES, 1), 0)
    pick_r = lambda a, c: jnp.sum(jnp.where(sel_r == c, a, 0.0), axis=0, keepdims=True)

    def run(ci_f, ci_b, q_s, k_s, v_t, rows_ref, cc_ref, hf, hb):
        for reverse, ci, hout in ((False, ci_f, hf), (True, ci_b, hb)):
            rows = pl.ds(pl.multiple_of(ci * tt, tt), tt)
            gates = rows_ref[ci]
            cc = cc_ref[rows, :]
            for j, (win, _, free) in enumerate(wins):
                irow = (2 * N_HEADS if reverse else 0) + pair * 2 + j
                lane_c = 2 * j + (1 if reverse else 0)
                chain = j * 2 + (1 if reverse else 0)
                ht = _mlstm_chunk(q_s[rows, win], k_s[j, rows, :], v_t[j, ci], free,
                                  pick_r(gates, irow + N_HEADS), pick_r(gates, irow),
                                  cc[:, lane_c:lane_c + 1], ct_scr.at[chain], m_scr.at[chain], reverse)
                hout[j, rows, :] = ht.T

    _scan_loops(qc_ref.shape[0] // tt, qx_ref.shape[0] // tt,
                lambda f, b: run(f, b, qsc, ksc, vtc, rwc_ref, ccc_ref, hfc, hbc),
                lambda f, b: run(f, b, qsx, ksx, vtx, rwx_ref, ccx_ref, hfx, hbx), unroll)

    def finish(hf, hb, o_ref, y_ref):
        def block(rows):
            hn = _pair_finish(hf, hb, rows, head_dim, mask_invalid=True)
            y_ref[rows, :] = (hn * ng_ref[...] * _sigmoid(o_ref[rows, :].astype(F32))).astype(BF16)

        _row_blocks(o_ref.shape[0], tt, block)

    finish(hfx, hbx, ox_ref, yx_ref)
    if write_ctx:
        finish(hfc, hbc, oc_ref, yc_ref)
    else:
        yc_ref[...] = jnp.zeros_like(yc_ref)


def _mlstm(px, pc, gx, gc, seq, seq_c, col0, conv_w, gate_b, norm_g, head_dim, write_ctx):
    batch = px.shape[0] // seq
    total = N_HEADS * head_dim
    npair = total // PAIR_W
    cb = col0 // PAIR_W
    tt = MXU_W if (seq % MXU_W == 0 and seq_c % MXU_W == 0) else CHUNK
    rows_x, cc_x = _mlstm_gates(gx, seq, tt, gate_b)
    rows_c, cc_c = _mlstm_gates(gc, seq_c, tt, gate_b)

    def pspecs(s):
        return [pl.BlockSpec((s, PAIR_W), lambda b, h, k=k: (b, cb + k * npair + h)) for k in range(4)]

    def gspecs(s):
        return [pl.BlockSpec((s // tt, N_GATES, tt), lambda b, h: (b, 0, 0)),
                pl.BlockSpec((s, LANE), lambda b, h: (b, h))]

    scr = lambda s, dt: pltpu.VMEM((2, s, WIN), dt)
    scr_t = lambda s: pltpu.VMEM((2, s // tt, WIN, tt), BF16)
    scr_q = lambda s: pltpu.VMEM((s, PAIR_W), BF16)
    scr_pad = lambda s: pltpu.VMEM((s + 2 * BF16_ROWS, PAIR_W), BF16)
    kern = functools.partial(_mlstm_kernel, head_dim=head_dim, write_ctx=write_ctx, unroll=2)
    return pl.pallas_call(
        kern,
        out_shape=(jax.ShapeDtypeStruct((batch * seq, total), BF16),
                   jax.ShapeDtypeStruct((batch * seq_c, total), BF16)),
        grid=(batch, npair),
        in_specs=(pspecs(seq) + gspecs(seq) + pspecs(seq_c) + gspecs(seq_c) + [
            pl.BlockSpec((3, PAIR_W), lambda b, h: (0, h)),
            pl.BlockSpec((3, PAIR_W), lambda b, h: (0, npair + h)),
            pl.BlockSpec((1, PAIR_W), lambda b, h: (0, h)),
        ]),
        out_specs=(pl.BlockSpec((seq, PAIR_W), lambda b, h: (b, h)),
                   pl.BlockSpec((seq_c, PAIR_W), lambda b, h: (b, h))),
        scratch_shapes=[scr_q(seq), scr(seq, BF16), scr_t(seq),
                        scr_q(seq_c), scr(seq_c, BF16), scr_t(seq_c),
                        scr(seq, F32), scr(seq, F32), scr(seq_c, F32), scr(seq_c, F32),
                        scr_pad(seq), scr_pad(seq), scr_pad(seq_c), scr_pad(seq_c),
                        pltpu.VMEM((4, WIN, WIN), F32),
                        pltpu.VMEM((4, 1, 1), F32)],
        compiler_params=_cparams(("arbitrary", "arbitrary")),
        name="mlstm",
    )(px, px, px, px, rows_x, cc_x, pc, pc, pc, pc, rows_c, cc_c,
      conv_w, conv_w, norm_g.reshape(1, total))


def _ret_chunk(q, k, v, dmask, q_decay, k_decay, c_decay, s_ref):
    pv = _dot((_dot_nt(q, k) * dmask).astype(BF16), v)
    kd = (k.astype(F32) * k_decay).astype(BF16)
    s_prev = s_ref[...]
    s_ref[...] = c_decay * s_prev + _dot_tn(kd, v)
    return pv + q_decay * _dot(q, s_prev.astype(BF16))


def _ret_kernel(ld_ref, qx_ref, kx_ref, vx_ref, gx_ref, qc_ref, kc_ref, vc_ref, gc_ref, cos_ref, sgn_ref,
                yx_ref, yc_ref,
                qsx, ksx, vsx, qsc, ksc, vsc, ofx, obx, ofc, obc, s_scr,
                *, key_dim, val_dim, write_ctx, unroll, t):
    pair = pl.program_id(1)
    qscale = key_dim ** -0.5
    half = key_dim // 2
    vwins = _head_windows(val_dim)
    lane = lax.broadcasted_iota(jnp.int32, (1, WIN), 1)
    klane = lax.broadcasted_iota(jnp.int32, (1, WIN), 1) + LANE * pair
    first_half = (klane % key_dim) < half

    def rope(a, rows):
        width = a.shape[1]
        partner = jnp.where(first_half, pltpu.roll(a, width - half, axis=1), pltpu.roll(a, half, axis=1))
        return a * cos_ref[rows, :] + partner * sgn_ref[rows, :]

    def prep(q_ref, k_ref, v_ref, qs, ks, vs, rotary):
        def block(rows):
            qb, kb = q_ref[rows, :], k_ref[rows, :]
            qa = jnp.where(pair == 0, qb[:, :WIN], qb[:, LANE:LANE + WIN]).astype(F32)
            kw = jnp.where(pair == 0, kb[:, :WIN], kb[:, LANE:LANE + WIN]).astype(F32)
            if rotary:
                qa, kw = rope(qa, rows), rope(kw, rows)
            qs[rows, :] = (qa * qscale).astype(BF16)
            for j, (win, valid, _) in enumerate(vwins):
                start = (2 * pair + j) * key_dim - LANE * pair
                kvalid = jnp.logical_and(lane >= start, lane < start + key_dim)
                ks[j, rows, :] = jnp.where(kvalid, kw, 0.0).astype(BF16)
                vs[j, rows, :] = jnp.where(valid, v_ref[rows, win].astype(F32), 0.0).astype(BF16)

        _row_blocks(q_ref.shape[0], t, block)

    prep(qx_ref, kx_ref, vx_ref, qsx, ksx, vsx, True)
    prep(qc_ref, kc_ref, vc_ref, qsc, ksc, vsc, False)
    s_scr[...] = jnp.zeros_like(s_scr)

    r = lax.broadcasted_iota(jnp.int32, (t, t), 0)
    c = lax.broadcasted_iota(jnp.int32, (t, t), 1)
    idx = lax.broadcasted_iota(jnp.int32, (t, 1), 0).astype(F32)
    consts = []
    for j in range(2):
        for reverse in (False, True):
            lg = -jnp.exp(jnp.full((1, 1), ld_ref[1 if reverse else 0, pair * 2 + j], F32))
            rel = (c - r) if reverse else (r - c)
            dmask = jnp.where(rel >= 0, jnp.exp(lg * jnp.maximum(rel, 0).astype(F32)), 0.0)
            pos = (t - 1.0 - idx) if reverse else idx
            consts.append((dmask, jnp.exp(lg * (pos + 1.0)), jnp.exp(lg * (t - 1.0 - pos)), jnp.exp(lg * t)))

    def run(ci_f, ci_b, q_s, k_s, v_s, of, ob):
        for reverse, ci, oout in ((False, ci_f, of), (True, ci_b, ob)):
            rows = pl.ds(pl.multiple_of(ci * t, t), t)
            for j in range(2):
                chain = j * 2 + (1 if reverse else 0)
                dmask, q_decay, k_decay, c_decay = consts[chain]
                oout[j, rows, :] = _ret_chunk(q_s[rows, :], k_s[j, rows, :], v_s[j, rows, :],
                                              dmask, q_decay, k_decay, c_decay, s_scr.at[chain])

    _scan_loops(qc_ref.shape[0] // t, qx_ref.shape[0] // t,
                lambda f, b: run(f, b, qsc, ksc, vsc, ofc, obc),
                lambda f, b: run(f, b, qsx, ksx, vsx, ofx, obx), unroll)

    def finish(of, ob, g_ref, y_ref):
        def block(rows):
            on = _pair_finish(of, ob, rows, val_dim, mask_invalid=False)
            y_ref[rows, :] = (on * _silu(g_ref[rows, :].astype(F32))).astype(BF16)

        _row_blocks(g_ref.shape[0], t, block)

    finish(ofx, obx, gx_ref, yx_ref)
    if write_ctx:
        finish(ofc, obc, gc_ref, yc_ref)
    else:
        yc_ref[...] = jnp.zeros_like(yc_ref)


def _retention(px, pc, seq, seq_c, col0, log_decay, cosf, sgnf, key_dim, val_dim, write_ctx):
    batch = px.shape[0] // seq
    ktot, vtot = N_HEADS * key_dim, N_HEADS * val_dim
    assert ktot == PAIR_W
    npair = vtot // PAIR_W
    qb = col0 // PAIR_W
    vb = (col0 + 2 * ktot) // PAIR_W

    def pspecs(s):
        return [pl.BlockSpec((s, PAIR_W), lambda b, h: (b, qb)),
                pl.BlockSpec((s, PAIR_W), lambda b, h: (b, qb + 1)),
                pl.BlockSpec((s, PAIR_W), lambda b, h: (b, vb + h)),
                pl.BlockSpec((s, PAIR_W), lambda b, h: (b, vb + npair + h))]

    scr = lambda s, dt: pltpu.VMEM((2, s, WIN), dt)
    t = MXU_W if (seq % MXU_W == 0 and seq_c % MXU_W == 0) else CHUNK
    kern = functools.partial(_ret_kernel, key_dim=key_dim, val_dim=val_dim, write_ctx=write_ctx, unroll=2, t=t)
    return pl.pallas_call(
        kern,
        out_shape=(jax.ShapeDtypeStruct((batch * seq, vtot), BF16),
                   jax.ShapeDtypeStruct((batch * seq_c, vtot), BF16)),
        grid=(batch, npair),
        in_specs=([pl.BlockSpec(memory_space=pltpu.SMEM)] + pspecs(seq) + pspecs(seq_c) + [
            pl.BlockSpec((None, seq, WIN), lambda b, h: (h, 0, 0)),
            pl.BlockSpec((None, seq, WIN), lambda b, h: (h, 0, 0)),
        ]),
        out_specs=(pl.BlockSpec((seq, PAIR_W), lambda b, h: (b, h)),
                   pl.BlockSpec((seq_c, PAIR_W), lambda b, h: (b, h))),
        scratch_shapes=[pltpu.VMEM((seq, WIN), BF16), scr(seq, BF16), scr(seq, BF16),
                        pltpu.VMEM((seq_c, WIN), BF16), scr(seq_c, BF16), scr(seq_c, BF16),
                        scr(seq, F32), scr(seq, F32), scr(seq_c, F32), scr(seq_c, F32),
                        pltpu.VMEM((4, WIN, WIN), F32)],
        compiler_params=_cparams(("arbitrary", "arbitrary")),
        name="retention",
    )(log_decay.astype(F32), px, px, px, px, pc, pc, pc, pc,
      np.stack([cosf[:, :WIN], cosf[:, LANE:]]), np.stack([sgnf[:, :WIN], sgnf[:, LANE:]]))


@functools.lru_cache(maxsize=None)
def _rope_tables(seq, key_dim):
    rows = seq // GRID_W
    r = np.repeat(np.arange(rows, dtype=np.float32), GRID_W)
    col = np.tile(np.arange(GRID_W, dtype=np.float32), rows)
    nf = key_dim // 4
    inv = (np.float32(ROPE_BASE) ** (-np.arange(nf, dtype=np.float32) / np.float32(nf))).astype(np.float32)
    ang = np.concatenate([r[:, None] * inv, col[:, None] * inv], axis=-1).astype(np.float32)
    cos, sin = np.cos(ang).astype(np.float32), np.sin(ang).astype(np.float32)
    return np.tile(np.concatenate([cos, cos], axis=1), (1, N_HEADS)), np.tile(np.concatenate([-sin, sin], axis=1), (1, N_HEADS))


def kernel(x, c, ctx, c_ctx, norm1_g, norm2_g, w_mod, b_mod, w_in, hy_conv_w, hy_f_w1, hy_f_b1, hy_f_w2, hy_f_b2, hy_f_w3, hy_f_freq, hy_bias, ml_conv_w, ml_gate_b, ml_norm_g, rt_log_decay, w_out, w_ff1, w_ff2, final_g):
    batch, seq, d = x.shape
    seq_c = ctx.shape[1]
    depth = w_mod.shape[0]
    hy_w = d // 4
    ml_w = 3 * d // 8
    rt_w = 3 * d // 8
    ml_dh = ml_w // N_HEADS
    rt_dv = rt_w // N_HEADS
    rt_dk = rt_dv // 2
    hy_cols, ml_cols = 3 * hy_w, 4 * ml_w
    gate0 = hy_cols + ml_cols
    rt0 = gate0 + N_GATES

    s_in = jnp.concatenate([c, c_ctx[None], jnp.zeros((MOD_ROWS - batch - 1, d), F32)], axis=0)
    mod = _modulation(s_in, w_mod, b_mod)

    grp_x = lambda tm: (lambda i: (i * tm) // seq)
    grp_c = lambda tm: (lambda i: batch + 0 * i)
    tm_x, tm_c = _pick_tile(seq, 1024), _pick_tile(batch * seq_c, 1024)
    tm_ox, tm_oc = _pick_tile(seq, 512), _pick_tile(batch * seq_c, 512)

    def dft(n):
        cm, sm, ct, st = _dft_tables(n)
        return jnp.asarray(cm).astype(BF16), jnp.asarray(sm).astype(BF16), ct, st

    cm_x, sm_x, ct_x, st_x = dft(seq)
    cm_c, sm_c, ct_c, st_c = dft(seq_c)
    cosf, sgnf = _rope_tables(seq, rt_dk)
    tc = _pick_tile(hy_w, MXU_W)
    tc_spec = _pick_tile(hy_w, LANE)

    xs = x.reshape(batch * seq, d)
    hc = ctx.reshape(batch * seq_c, d)
    wt = jnp.swapaxes(w_in, 1, 2)
    w_pt = wt.astype(BF16)
    w_gt = jnp.pad(wt[:, gate0:rt0], ((0, 0), (0, LANE - N_GATES), (0, 0))).astype(BF16)
    ml_col0, rt_col0 = hy_cols, gate0
    wo = w_out.astype(BF16)
    w1 = w_ff1.astype(BF16)
    w2 = w_ff2.astype(BF16)

    for l in range(depth):
        need_ctx = l < depth - 1
        mod3 = mod[l].reshape(MOD_ROWS * N_MOD, 1, d)

        px, gx = _in_proj(xs, norm1_g[l], mod3, grp_x(tm_x), w_pt, w_gt, l, tm_x, gate0, rt0)
        pc, gc = _in_proj(hc, norm1_g[l], mod3, grp_c(tm_c), w_pt, w_gt, l, tm_c, gate0, rt0)

        filt = (hy_f_w1[l], hy_f_b1[l], hy_f_w2[l], hy_f_b2[l], hy_f_w3[l], hy_f_freq[l])
        hf, hb = _hyena_filter(seq, *filt)
        hre, him = _hyena_spectrum(hf, hb, cm_x, sm_x, ct_x, st_x, tc_spec)
        y_hy = _hyena_conv(px, seq, hy_conv_w[l], hre, him, hy_bias[l], cm_x, sm_x, tc)
        y_ml, yc_ml = _mlstm(px, pc, gx, gc, seq, seq_c, ml_col0, ml_conv_w[l], ml_gate_b[l], ml_norm_g[l],
                             ml_dh, need_ctx)
        y_rt, yc_rt = _retention(px, pc, seq, seq_c, rt_col0, rt_log_decay[l], cosf, sgnf,
                                 rt_dk, rt_dv, need_ctx)
        xs, h2 = _out_proj(xs, y_hy, y_ml, y_rt, wo, l, norm2_g[l], mod3, grp_x(tm_ox), tm_ox, alias=l > 0)
        if need_ctx:
            hfc, hbc = _hyena_filter(seq_c, *filt)
            hre_c, him_c = _hyena_spectrum(hfc, hbc, cm_c, sm_c, ct_c, st_c, tc_spec)
            yc_hy = _hyena_conv(pc, seq_c, hy_conv_w[l], hre_c, him_c, hy_bias[l], cm_c, sm_c, tc)
            hc, h2c = _out_proj(hc, yc_hy, yc_ml, yc_rt, wo, l, norm2_g[l], mod3, grp_c(tm_oc), tm_oc, alias=l > 0)
            hc = _ffn(hc, h2c, mod3, grp_c(tm_c), w1, w2, l, tm_c, 1024)
        xs = _ffn(xs, h2, mod3, grp_x(tm_x), w1, w2, l, tm_x, 1024)
    return _final_norm(xs, final_g, tm_ox).reshape(batch, seq, d)
```
